```python
import math
import jax
import jax.numpy as jnp
from jax import lax
import numpy as np

D_MODEL = 2048
BATCH = 2
SEQ = 16384
DEPTH = 4

GRID_W = 64
HEAD_DIM = 64
MIX_WIDTH = D_MODEL
NA_WIDTH = MIX_WIDTH // 4
DIL_WIDTH = MIX_WIDTH // 4
SSM_WIDTH = MIX_WIDTH // 2
NA_HEADS = NA_WIDTH // HEAD_DIM
NA_KH = 8
NA_KW = 16
DIL_HEADS = DIL_WIDTH // HEAD_DIM
DIL_BRANCHES = ((128, 1), (512, 4), (2048, 16))
DIL_QBLK = 128
SSM_HEAD_DIM = 64
SSM_HEADS = SSM_WIDTH // SSM_HEAD_DIM
SSM_GROUPS = 2
SSM_STATE = 128
SSM_CONV = 5
SSM_CHUNK = 128
SSM_CONV_CH = SSM_WIDTH + 2 * SSM_GROUPS * SSM_STATE
IN_COLS = 3 * NA_WIDTH + 3 * DIL_WIDTH + SSM_WIDTH + SSM_CONV_CH + 2 * SSM_HEADS
FFN_DIM = 5632
N_EXPERTS = 8
TOP_K = 2
EXPERT_DIM = 5632
MOE_BLOCK = 128
ROPE_THETA = 10000.0
NORM_EPS = 1e-6

kernel_name = 'hybrid_natten_dilated_ssd_moe_encoder'


def rms_norm(x, g):
    xf = x.astype(jnp.float32)
    y = xf * lax.rsqrt(jnp.mean(xf * xf, axis=-1, keepdims=True) + NORM_EPS)
    return (y * g.astype(jnp.float32)).astype(x.dtype)


def rope(x):
    s, hd = x.shape[1], x.shape[3]
    inv = ROPE_THETA ** (-jnp.arange(0, hd, 2, dtype=jnp.float32) / hd)
    ang = jnp.arange(s, dtype=jnp.float32)[:, None] * inv[None, :]
    cos = jnp.cos(ang)[None, :, None, :]
    sin = jnp.sin(ang)[None, :, None, :]
    xf = x.astype(jnp.float32)
    x1, x2 = xf[..., : hd // 2], xf[..., hd // 2:]
    return jnp.concatenate([x1 * cos - x2 * sin, x2 * cos + x1 * sin], axis=-1).astype(x.dtype)


def _col_windows():
    c = np.arange(GRID_W)
    cs = np.clip(c - NA_KW // 2, 0, GRID_W - NA_KW)
    idx = cs[:, None] + np.arange(NA_KW)[None, :]
    return idx.astype(np.int32), (idx - c[:, None]).astype(np.int32)


def neighbourhood_attention(q, k, v, rpb):
    bsz, s, h, hd = q.shape
    rows = s // GRID_W
    kh = min(NA_KH, rows)
    qg = (q * (hd ** -0.5)).reshape(bsz, rows, GRID_W, h, hd)
    kg = k.reshape(bsz, rows, GRID_W, h, hd)
    vg = v.reshape(bsz, rows, GRID_W, h, hd)
    col_idx_np, dc_np = _col_windows()
    col_idx = jnp.asarray(col_idx_np)
    dc_idx = jnp.asarray(dc_np + NA_KW - 1)

    def one_row(r):
        rs = jnp.clip(r - kh // 2, 0, rows - kh)
        q_r = lax.dynamic_index_in_dim(qg, r, axis=1, keepdims=False)
        k_w = lax.dynamic_slice_in_dim(kg, rs, kh, axis=1)[:, :, col_idx]
        v_w = lax.dynamic_slice_in_dim(vg, rs, kh, axis=1)[:, :, col_idx]
        dr_idx = rs + jnp.arange(kh) - r + NA_KH - 1
        bias = rpb[:, dr_idx[:, None, None], dc_idx[None, :, :]]
        bias = jnp.transpose(bias, (0, 2, 1, 3)).astype(jnp.float32)
        sc = jnp.einsum('bqhd,biqjhd->bhqij', q_r, k_w).astype(jnp.float32) + bias[None]
        p = jax.nn.softmax(sc.reshape(bsz, h, GRID_W, kh * NA_KW), axis=-1).reshape(sc.shape)
        return jnp.einsum('bhqij,biqjhd->bqhd', p.astype(v.dtype), v_w)

    out = lax.map(one_row, jnp.arange(rows))
    return jnp.moveaxis(out, 0, 1).reshape(bsz, s, h * hd)


def _dilated_offsets():
    offs = [np.arange(-((w // 2) // d), (w // 2) // d + 1) * d for w, d in DIL_BRANCHES]
    return np.stack(offs).astype(np.int32)


def dilated_attention(q, k, v):
    bsz, s, h, hd = q.shape
    offs = jnp.asarray(_dilated_offsets())
    qs = q * (hd ** -0.5)

    def one_block(s0):
        qb = lax.dynamic_slice_in_dim(qs, s0, DIL_QBLK, axis=1)
        pos = s0 + jnp.arange(DIL_QBLK)[:, None, None] + offs[None]
        valid = (pos >= 0) & (pos < s)
        pos = jnp.clip(pos, 0, s - 1)
        kb = jnp.take(k, pos, axis=1)
        vb = jnp.take(v, pos, axis=1)
        sc = jnp.einsum('bqhd,bqnjhd->bhqnj', qb, kb).astype(jnp.float32)
        sc = jnp.where(valid[None, None], sc, -jnp.inf)
        lse = jax.nn.logsumexp(sc, axis=-1)
        p = jnp.exp(sc - lse[..., None])
        o = jnp.einsum('bhqnj,bqnjhd->bqnhd', p.astype(v.dtype), vb)
        w = jax.nn.softmax(lse, axis=-1)
        return jnp.einsum('bhqn,bqnhd->bqhd', w.astype(v.dtype), o)

    out = lax.map(one_block, jnp.arange(0, s, DIL_QBLK))
    return jnp.moveaxis(out, 0, 1).reshape(bsz, s, h * hd)


def centred_depthwise_conv(x, w, b):
    kw, c = w.shape
    y = lax.conv_general_dilated(x, w[:, None, :].astype(x.dtype), window_strides=(1,),
                                 padding=((kw // 2, kw // 2),),
                                 dimension_numbers=('NWC', 'WIO', 'NWC'), feature_group_count=c)
    return y + b.astype(x.dtype)


def ssd_chunked(x, dt, a, bm, cm):
    bsz, l, h, p = x.shape
    g, n = bm.shape[2], bm.shape[3]
    r = h // g
    q = SSM_CHUNK
    nc = l // q
    f32 = jnp.float32
    x, dt, a, bm, cm = x.astype(f32), dt.astype(f32), a.astype(f32), bm.astype(f32), cm.astype(f32)
    xc = (x * dt[..., None]).reshape(bsz, nc, q, g, r, p)
    acs = jnp.cumsum((dt * a).reshape(bsz, nc, q, g, r), axis=2)
    bc = bm.reshape(bsz, nc, q, g, n)
    cc = cm.reshape(bsz, nc, q, g, n)
    causal = jnp.tril(jnp.ones((q, q), dtype=bool))
    seg = acs[:, :, :, None] - acs[:, :, None, :]
    decay = jnp.exp(jnp.where(causal[None, None, :, :, None, None], seg, -jnp.inf))
    cb = jnp.einsum('bclgn,bcsgn->bclsg', cc, bc)
    y_diag = jnp.einsum('bclsgr,bcsgrp->bclgrp', cb[..., None] * decay, xc)
    decay_states = jnp.exp(acs[:, :, -1:] - acs)
    states = jnp.einsum('bclgn,bclgrp->bcgrpn', bc, xc * decay_states[..., None])
    chunk_decay = jnp.exp(acs[:, :, -1])

    def step(hs, inp):
        s_c, d_c = inp
        return d_c[..., None, None] * hs + s_c, hs

    h0 = jnp.zeros((bsz, g, r, p, n), f32)
    _, prev = lax.scan(step, h0, (jnp.moveaxis(states, 1, 0), jnp.moveaxis(chunk_decay, 1, 0)))
    prev = jnp.moveaxis(prev, 0, 1)
    y_off = jnp.einsum('bclgn,bcgrpn->bclgrp', cc, prev) * jnp.exp(acs)[..., None]
    return (y_diag + y_off).reshape(bsz, l, h, p)


def bidirectional_ssd(z, xbc, dt_raw, conv_w, conv_b, dt_bias, a_log, d_skip, norm_g):
    bsz, s, _ = z.shape
    f32 = jnp.float32
    xbc = jax.nn.silu(centred_depthwise_conv(xbc, conv_w, conv_b))
    gn = SSM_GROUPS * SSM_STATE
    xs = xbc[..., :SSM_WIDTH].reshape(bsz, s, SSM_HEADS, SSM_HEAD_DIM)
    bm = xbc[..., SSM_WIDTH:SSM_WIDTH + gn].reshape(bsz, s, SSM_GROUPS, SSM_STATE)
    cm = xbc[..., SSM_WIDTH + gn:].reshape(bsz, s, SSM_GROUPS, SSM_STATE)
    dt = jax.nn.softplus(dt_raw.reshape(bsz, s, 2, SSM_HEADS).astype(f32) + dt_bias.astype(f32))
    a = -jnp.exp(a_log.astype(f32))
    y_fwd = ssd_chunked(xs, dt[:, :, 0], a[0], bm, cm)
    fl = lambda t: jnp.flip(t, axis=1)
    y_bwd = fl(ssd_chunked(fl(xs), fl(dt[:, :, 1]), a[1], fl(bm), fl(cm)))
    y = y_fwd + y_bwd + d_skip.astype(f32)[:, None] * xs.astype(f32)
    y = y.reshape(bsz, s, SSM_WIDTH) * jax.nn.silu(z.astype(f32))
    yg = y.reshape(bsz, s, SSM_GROUPS, SSM_WIDTH // SSM_GROUPS)
    yg = yg * lax.rsqrt(jnp.mean(yg * yg, axis=-1, keepdims=True) + NORM_EPS)
    return (yg.reshape(bsz, s, SSM_WIDTH) * norm_g.astype(f32)).astype(z.dtype)


def swiglu(h, wg, wu, wd):
    return jnp.einsum('bsf,fd->bsd', jax.nn.silu(jnp.einsum('bsd,df->bsf', h, wg)) * jnp.einsum('bsd,df->bsf', h, wu), wd)


def moe_swiglu(h, w_router, wg, wu, wd):
    bsz, s, d = h.shape
    t = bsz * s
    ht = h.reshape(t, d)
    logits = jnp.einsum('td,de->te', ht, w_router).astype(jnp.float32)
    top_val, top_idx = lax.top_k(logits, TOP_K)
    gates = jax.nn.softmax(top_val, axis=-1)
    n_assign = t * TOP_K
    e_flat = top_idx.reshape(-1)
    tok_flat = jnp.arange(n_assign) // TOP_K
    order = jnp.argsort(e_flat)
    e_sorted = e_flat[order]
    tok_sorted = tok_flat[order]
    g_sorted = gates.reshape(-1)[order]
    counts = jnp.bincount(e_flat, length=N_EXPERTS)
    padded = ((counts + MOE_BLOCK - 1) // MOE_BLOCK) * MOE_BLOCK
    gstart = jnp.cumsum(counts) - counts
    pend = jnp.cumsum(padded)
    pstart = pend - padded
    dest = pstart[e_sorted] + (jnp.arange(n_assign) - gstart[e_sorted])
    n_rows = n_assign + N_EXPERTS * MOE_BLOCK
    n_blocks = n_rows // MOE_BLOCK
    row_tok = jnp.full((n_rows,), t, dtype=jnp.int32).at[dest].set(tok_sorted.astype(jnp.int32))
    row_gate = jnp.zeros((n_rows,), h.dtype).at[dest].set(g_sorted.astype(h.dtype))
    blk_expert = jnp.minimum(jnp.searchsorted(pend, jnp.arange(n_blocks) * MOE_BLOCK, side='right'), N_EXPERTS - 1)
    x_pad = jnp.concatenate([ht, jnp.zeros((1, d), h.dtype)], axis=0)

    def expert_block(args):
        tok, e = args
        xb = x_pad[tok]
        return (jax.nn.silu(xb @ wg[e]) * (xb @ wu[e])) @ wd[e]

    yb = lax.map(expert_block, (row_tok.reshape(n_blocks, MOE_BLOCK), blk_expert)).reshape(n_rows, d)
    y = jnp.zeros((t + 1, d), h.dtype).at[row_tok].add(yb * row_gate[:, None])[:t]
    return y.reshape(bsz, s, d)


def setup_inputs(seed: int = 0) -> dict:
    key = jax.random.key(seed)
    ks = jax.random.split(key, 20)
    f32 = jnp.float32
    n_dense = (DEPTH + 1) // 2
    n_moe = DEPTH // 2
    nrm = lambda k, shape, scale: jax.random.normal(k, shape, f32) * scale
    x = nrm(ks[0], (BATCH, SEQ, D_MODEL), 1.0)
    norm_mix = 1.0 + nrm(ks[1], (DEPTH, D_MODEL), 0.02)
    w_in = nrm(ks[2], (DEPTH, D_MODEL, IN_COLS), D_MODEL ** -0.5)
    na_rpb = nrm(ks[3], (DEPTH, NA_HEADS, 2 * NA_KH - 1, 2 * NA_KW - 1), 0.2)
    conv_w = nrm(ks[4], (DEPTH, SSM_CONV, SSM_CONV_CH), SSM_CONV ** -0.5)
    conv_b = nrm(ks[5], (DEPTH, SSM_CONV_CH), 0.01)
    dt0 = jnp.exp(jax.random.uniform(ks[6], (DEPTH, 2, SSM_HEADS), f32, math.log(1e-3), math.log(0.1)))
    dt_bias = dt0 + jnp.log(-jnp.expm1(-dt0))
    a_log = jnp.log(jax.random.uniform(ks[7], (DEPTH, 2, SSM_HEADS), f32, 1.0, 16.0))
    d_skip = 1.0 + nrm(ks[8], (DEPTH, SSM_HEADS), 0.1)
    ssm_norm = 1.0 + nrm(ks[9], (DEPTH, SSM_WIDTH), 0.02)
    w_out = nrm(ks[10], (DEPTH, MIX_WIDTH, D_MODEL), MIX_WIDTH ** -0.5)
    norm_ffn = 1.0 + nrm(ks[11], (DEPTH, D_MODEL), 0.02)
    ffn_w_gate = nrm(ks[12], (n_dense, D_MODEL, FFN_DIM), D_MODEL ** -0.5)
    ffn_w_up = nrm(ks[13], (n_dense, D_MODEL, FFN_DIM), D_MODEL ** -0.5)
    ffn_w_down = nrm(ks[14], (n_dense, FFN_DIM, D_MODEL), FFN_DIM ** -0.5)
    router_w = nrm(ks[15], (n_moe, D_MODEL, N_EXPERTS), D_MODEL ** -0.5)
    exp_w_gate = nrm(ks[16], (n_moe, N_EXPERTS, D_MODEL, EXPERT_DIM), D_MODEL ** -0.5)
    exp_w_up = nrm(ks[17], (n_moe, N_EXPERTS, D_MODEL, EXPERT_DIM), D_MODEL ** -0.5)
    exp_w_down = nrm(ks[18], (n_moe, N_EXPERTS, EXPERT_DIM, D_MODEL), EXPERT_DIM ** -0.5)
    norm_final = 1.0 + nrm(ks[19], (D_MODEL,), 0.02)
    return {'x': x, 'norm_mix': norm_mix, 'w_in': w_in, 'na_rpb': na_rpb, 'conv_w': conv_w,
            'conv_b': conv_b, 'dt_bias': dt_bias, 'a_log': a_log, 'd_skip': d_skip,
            'ssm_norm': ssm_norm, 'w_out': w_out, 'norm_ffn': norm_ffn, 'ffn_w_gate': ffn_w_gate,
            'ffn_w_up': ffn_w_up, 'ffn_w_down': ffn_w_down, 'router_w': router_w,
            'exp_w_gate': exp_w_gate, 'exp_w_up': exp_w_up, 'exp_w_down': exp_w_down,
            'norm_final': norm_final}


def reference(x, norm_mix, w_in, na_rpb, conv_w, conv_b, dt_bias, a_log, d_skip, ssm_norm, w_out,
              norm_ffn, ffn_w_gate, ffn_w_up, ffn_w_down, router_w, exp_w_gate, exp_w_up, exp_w_down,
              norm_final):
    bsz, s, _ = x.shape
    sizes = [NA_WIDTH] * 3 + [DIL_WIDTH] * 3 + [SSM_WIDTH, SSM_CONV_CH, 2 * SSM_HEADS]
    split_at = [int(c) for c in np.cumsum(sizes)[:-1]]
    heads = lambda tt, nh: tt.reshape(bsz, s, nh, HEAD_DIM)
    for layer in range(DEPTH):
        h = rms_norm(x, norm_mix[layer])
        proj = jnp.einsum('bsd,dc->bsc', h, w_in[layer])
        qa, ka, va, qb, kb, vb, z, xbc, dt_raw = jnp.split(proj, split_at, axis=-1)
        y_na = neighbourhood_attention(heads(qa, NA_HEADS), heads(ka, NA_HEADS), heads(va, NA_HEADS), na_rpb[layer])
        y_dil = dilated_attention(rope(heads(qb, DIL_HEADS)), rope(heads(kb, DIL_HEADS)), heads(vb, DIL_HEADS))
        y_ssd = bidirectional_ssd(z, xbc, dt_raw, conv_w[layer], conv_b[layer], dt_bias[layer],
                                  a_log[layer], d_skip[layer], ssm_norm[layer])
        mixed = jnp.concatenate([y_na, y_dil, y_ssd], axis=-1)
        x = x + jnp.einsum('bsc,cd->bsd', mixed, w_out[layer])
        h = rms_norm(x, norm_ffn[layer])
        j = layer // 2
        if layer % 2 == 0:
            x = x + swiglu(h, ffn_w_gate[j], ffn_w_up[j], ffn_w_down[j])
        else:
            x = x + moe_swiglu(h, router_w[j], exp_w_gate[j], exp_w_up[j], exp_w_down[j])
    return rms_norm(x, norm_final)
```

```python
import functools

import numpy as np
import jax
import jax.numpy as jnp
from jax import lax
from jax.experimental import pallas as pl
from jax.experimental.pallas import tpu as pltpu

F32 = jnp.float32
BF16 = jnp.bfloat16
I32 = jnp.int32

GRID_W = 64
HEAD_DIM = 64
NA_KH = 8
NA_KW = 16
N_HEADS = 8
ATT_W = N_HEADS * HEAD_DIM
DIL_BRANCHES = ((128, 1), (512, 4), (2048, 16))
DIL_QBLK = 128
DIL_HALF = 64
SSM_HEADS = 16
SSM_GROUPS = 2
SSM_STATE = 128
SSM_CONV = 5
SSM_CHUNK = 128
SSM_W = SSM_HEADS * HEAD_DIM
SSM_BC = SSM_GROUPS * SSM_STATE
SSM_CONV_CH = SSM_W + 2 * SSM_BC
N_EXPERTS = 8
ROPE_THETA = 10000.0
NORM_EPS = 1e-6
NEG = -1e30

LANES = 128
VMEM_LIMIT = 56 * 1024 * 1024
PROJ_TN = 512
MAIN_COLS = 11 * PROJ_TN
NA_RB = 4


def _cparams(sem):
    return pltpu.CompilerParams(dimension_semantics=sem, vmem_limit_bytes=VMEM_LIMIT)


def _split3(x):
    hi = x.astype(BF16)
    r = x - hi.astype(F32)
    mid = r.astype(BF16)
    lo = (r - mid.astype(F32)).astype(BF16)
    return hi, mid, lo


def _dot01_l(m01, x):
    hi, mid, lo = _split3(x)
    d = lambda a: jnp.dot(m01, a, preferred_element_type=F32)
    return d(hi) + d(mid) + d(lo)


def _dot01_r(x, m01):
    hi, mid, lo = _split3(x)
    d = lambda a: jnp.dot(a, m01, preferred_element_type=F32)
    return d(hi) + d(mid) + d(lo)


def _silu(x):
    return x / (1.0 + jnp.exp(-x))


def _inproj_body(x_ref, g_ref, w_ref, wdt_ref, cos_ref, sin_ref, o_ref, odt_ref, xn_ref):
    j = pl.program_id(1)

    @pl.when(j == 0)
    def _():
        x = x_ref[...]
        ms = jnp.mean(x * x, axis=-1, keepdims=True)
        xn = (x * lax.rsqrt(ms + NORM_EPS) * g_ref[...]).astype(BF16)
        xn_ref[...] = xn
        odt_ref[...] = jnp.dot(xn, wdt_ref[...], preferred_element_type=F32)

    acc = jnp.dot(xn_ref[...], w_ref[...], preferred_element_type=F32)
    acc = acc * jnp.where((j == 0) | (j == 3), HEAD_DIM ** -0.5, 1.0).astype(F32)
    is_rope = (j == 3) | (j == 4)

    @pl.when(is_rope)
    def _():
        cos = cos_ref[...]
        sin = sin_ref[...]
        lane = lax.broadcasted_iota(I32, cos.shape, 1)
        first = (lane % HEAD_DIM) < (HEAD_DIM // 2)
        for c in range(PROJ_TN // LANES):
            xs = acc[:, c * LANES:(c + 1) * LANES]
            partner = jnp.where(first, pltpu.roll(xs, LANES - HEAD_DIM // 2, 1),
                                pltpu.roll(xs, HEAD_DIM // 2, 1))
            o_ref[:, c * LANES:(c + 1) * LANES] = (xs * cos + partner * sin).astype(BF16)

    @pl.when(jnp.logical_not(is_rope))
    def _():
        o_ref[...] = acc.astype(BF16)


def _inproj(x2, g, w_main, w_dt, cos_t, sin_t, seq, tm):
    t, d = x2.shape
    n_s = seq // tm
    return pl.pallas_call(
        _inproj_body,
        grid=(t // tm, MAIN_COLS // PROJ_TN),
        in_specs=[
            pl.BlockSpec((tm, d), lambda i, j: (i, 0)),
            pl.BlockSpec((1, d), lambda i, j: (0, 0)),
            pl.BlockSpec((d, PROJ_TN), lambda i, j: (0, j)),
            pl.BlockSpec((d, LANES), lambda i, j: (0, 0)),
            pl.BlockSpec((tm, LANES), lambda i, j: (i % n_s, 0)),
            pl.BlockSpec((tm, LANES), lambda i, j: (i % n_s, 0)),
        ],
        out_specs=[
            pl.BlockSpec((tm, PROJ_TN), lambda i, j: (i, j)),
            pl.BlockSpec((tm, LANES), lambda i, j: (i, 0)),
        ],
        out_shape=[jax.ShapeDtypeStruct((t, MAIN_COLS), BF16),
                   jax.ShapeDtypeStruct((t, LANES), F32)],
        scratch_shapes=[pltpu.VMEM((tm, d), BF16)],
        compiler_params=_cparams(("arbitrary", "arbitrary")),
        name="inproj",
    )(x2, g, w_main, w_dt, cos_t, sin_t)


def _na_bias_tables(rpb, rows):
    nb = rows // NA_RB
    c = np.arange(GRID_W)
    cs = np.clip(c - NA_KW // 2, 0, GRID_W - NA_KW)
    ck = np.arange(GRID_W)
    col_ok = (ck[None, :] >= cs[:, None]) & (ck[None, :] < cs[:, None] + NA_KW)
    dc = np.clip(ck[None, :] - c[:, None] + NA_KW - 1, 0, 2 * NA_KW - 2)
    tabs = []
    for i in (0, 1, nb - 1):
        r = NA_RB * i + np.arange(NA_RB)
        rs = np.clip(r - NA_KH // 2, 0, rows - NA_KH)
        absrow = NA_RB * (i - 1) + np.arange(3 * NA_RB)
        row_ok = (absrow[None, :] >= rs[:, None]) & (absrow[None, :] < rs[:, None] + NA_KH)
        dr = np.clip(absrow[None, :] - r[:, None] + NA_KH - 1, 0, 2 * NA_KH - 2)
        ok = row_ok[:, None, :, None] & col_ok[None, :, None, :]
        shape = ok.shape
        dr_f = np.broadcast_to(dr[:, None, :, None], shape)
        dc_f = np.broadcast_to(dc[None, :, None, :], shape)
        b = rpb.astype(F32)[:, dr_f, dc_f]
        b = jnp.where(ok[None], b, NEG)
        tabs.append(b.reshape(rpb.shape[0], NA_RB * GRID_W, 3 * NA_RB * GRID_W))
    return jnp.stack(tabs)


def _na_body(q_ref, kp_ref, kc_ref, kn_ref, vp_ref, vc_ref, vn_ref, b_ref, o_ref):
    for h in range(N_HEADS):
        sl = slice(h * HEAD_DIM, (h + 1) * HEAD_DIM)
        q = q_ref[:, sl]
        k = jnp.concatenate([kp_ref[:, sl], kc_ref[:, sl], kn_ref[:, sl]], axis=0)
        v = jnp.concatenate([vp_ref[:, sl], vc_ref[:, sl], vn_ref[:, sl]], axis=0)
        s = lax.dot_general(q, k, (((1,), (1,)), ((), ())), preferred_element_type=F32)
        s = s + b_ref[0, h]
        m = jnp.max(s, axis=-1, keepdims=True)
        p = jnp.exp(s - m)
        l = jnp.sum(p, axis=-1, keepdims=True)
        o = jnp.dot(p.astype(BF16), v, preferred_element_type=F32) / l
        o_ref[:, sl] = o.astype(BF16)


def _na_attention(proj, bias_tabs, bsz, seq):
    t = bsz * seq
    tq = NA_RB * GRID_W
    nb = seq // tq
    assert nb >= 3 and seq // GRID_W >= NA_KH
    sel = lambda i: jnp.where(i == 0, 0, jnp.where(i == nb - 1, 2, 1))
    prev = lambda b, i: b * nb + jnp.maximum(i - 1, 0)
    cur = lambda b, i: b * nb + i
    nxt = lambda b, i: b * nb + jnp.minimum(i + 1, nb - 1)
    blk = lambda rowf, col: pl.BlockSpec((tq, ATT_W), lambda b, i: (rowf(b, i), col))
    return pl.pallas_call(
        _na_body,
        grid=(bsz, nb),
        in_specs=[blk(cur, 0), blk(prev, 1), blk(cur, 1), blk(nxt, 1),
                  blk(prev, 2), blk(cur, 2), blk(nxt, 2),
                  pl.BlockSpec((1, N_HEADS, tq, 3 * tq), lambda b, i: (sel(i), 0, 0, 0))],
        out_specs=pl.BlockSpec((tq, ATT_W), lambda b, i: (cur(b, i), 0)),
        out_shape=jax.ShapeDtypeStruct((t, ATT_W), BF16),
        compiler_params=_cparams(("arbitrary", "arbitrary")),
        name="na_attn",
    )(proj, proj, proj, proj, proj, proj, proj, bias_tabs)


def _dil_body(*refs, first, last, n_sub):
    if first:
        q_ref, kp_ref, kc_ref, kn_ref, vp_ref, vc_ref, vn_ref = refs[:7]
        outs = refs[7:]
    else:
        q_ref, kp_ref, kc_ref, kn_ref, vp_ref, vc_ref, vn_ref, orun_ref, lrun_ref = refs[:9]
        outs = refs[9:]
    o_ref = outs[0]
    u = pl.program_id(1)
    nk = 2 * DIL_QBLK
    a = lax.broadcasted_iota(I32, (DIL_QBLK, nk), 0)
    kk = lax.broadcasted_iota(I32, (DIL_QBLK, nk), 1)
    uk = u * DIL_QBLK - DIL_HALF + kk
    ok = (kk >= a) & (kk <= a + 2 * DIL_HALF) & (uk >= 0) & (uk < n_sub)
    lane = lax.broadcasted_iota(I32, (DIL_QBLK, LANES), 1)
    lse_tile = jnp.zeros((DIL_QBLK, LANES), F32)
    if not first:
        lrun = lrun_ref[...]
    for h in range(N_HEADS):
        sl = slice(h * HEAD_DIM, (h + 1) * HEAD_DIM)
        q = q_ref[:, sl]
        k = jnp.concatenate([kp_ref[DIL_HALF:, sl], kc_ref[:, sl], kn_ref[:DIL_HALF, sl]], axis=0)
        v = jnp.concatenate([vp_ref[DIL_HALF:, sl], vc_ref[:, sl], vn_ref[:DIL_HALF, sl]], axis=0)
        s = lax.dot_general(q, k, (((1,), (1,)), ((), ())), preferred_element_type=F32)
        s = jnp.where(ok, s, NEG)
        m = jnp.max(s, axis=-1, keepdims=True)
        p = jnp.exp(s - m)
        l = jnp.sum(p, axis=-1, keepdims=True)
        o = jnp.dot(p.astype(BF16), v, preferred_element_type=F32) / l
        lse = m + jnp.log(l)
        if not first:
            lse_r = lrun[:, h:h + 1]
            mx = jnp.maximum(lse_r, lse)
            lse_new = mx + jnp.log(jnp.exp(lse_r - mx) + jnp.exp(lse - mx))
            o = orun_ref[:, sl] * jnp.exp(lse_r - lse_new) + o * jnp.exp(lse - lse_new)
            lse = lse_new
        o_ref[:, sl] = o.astype(o_ref.dtype)
        if not last:
            lse_tile = jnp.where(lane == h, lse, lse_tile)
    if not last:
        outs[1][...] = lse_tile


def _dil_branch(proj, run, bsz, seq, dil, first, last):
    t = bsz * seq
    n_sub = seq // dil
    nu = n_sub // DIL_QBLK
    ncol = MAIN_COLS // ATT_W
    pv = proj.reshape(t // dil, dil * MAIN_COLS)
    prev = lambda b, u: b * nu + jnp.maximum(u - 1, 0)
    cur = lambda b, u: b * nu + u
    nxt = lambda b, u: b * nu + jnp.minimum(u + 1, nu - 1)
    blk = lambda rowf, col: pl.BlockSpec((DIL_QBLK, ATT_W), lambda b, u, r: (rowf(b, u), ncol * r + col))
    o_spec = pl.BlockSpec((DIL_QBLK, ATT_W), lambda b, u, r: (cur(b, u), r))
    l_spec = pl.BlockSpec((DIL_QBLK, LANES), lambda b, u, r: (cur(b, u), r))
    in_specs = [blk(cur, 3), blk(prev, 4), blk(cur, 4), blk(nxt, 4), blk(prev, 5), blk(cur, 5), blk(nxt, 5)]
    args = [pv] * 7
    if not first:
        in_specs += [o_spec, l_spec]
        args += [run[0].reshape(t // dil, dil * ATT_W), run[1].reshape(t // dil, dil * LANES)]
    if last:
        out_specs = [o_spec]
        out_shape = [jax.ShapeDtypeStruct((t // dil, dil * ATT_W), BF16)]
    else:
        out_specs = [o_spec, l_spec]
        out_shape = [jax.ShapeDtypeStruct((t // dil, dil * ATT_W), F32),
                     jax.ShapeDtypeStruct((t // dil, dil * LANES), F32)]
    outs = pl.pallas_call(
        functools.partial(_dil_body, first=first, last=last, n_sub=n_sub),
        grid=(bsz, nu, dil),
        in_specs=in_specs, out_specs=out_specs, out_shape=out_shape,
        compiler_params=_cparams(("arbitrary", "arbitrary", "arbitrary")),
        name=f"dil_attn_d{dil}",
    )(*args)
    if last:
        return outs[0].reshape(t, ATT_W)
    return outs[0].reshape(t, ATT_W), outs[1].reshape(t, LANES)


def _dilated_attention(proj, bsz, seq):
    run = None
    nbr = len(DIL_BRANCHES)
    for n, (win, dil) in enumerate(DIL_BRANCHES):
        assert win // 2 // dil == DIL_HALF and seq % (dil * DIL_QBLK) == 0
        run = _dil_branch(proj, run, bsz, seq, dil, n == 0, n == nbr - 1)
    return run


CONV_HALO = 16


def _conv_body(xp_ref, xc_ref, xn_ref, w_ref, b_ref, o_ref, *, blocks_per_seq):
    i = pl.program_id(0)
    r = xc_ref.shape[0]
    pos = i % blocks_per_seq
    keep_p = jnp.where(pos == 0, 0.0, 1.0).astype(F32)
    keep_n = jnp.where(pos == blocks_per_seq - 1, 0.0, 1.0).astype(F32)
    ext = jnp.concatenate([xp_ref[...].astype(F32) * keep_p, xc_ref[...].astype(F32),
                           xn_ref[...].astype(F32) * keep_n], axis=0)
    w = w_ref[...]
    half = SSM_CONV // 2
    y = b_ref[...] + jnp.zeros((r, w.shape[1]), F32)
    for k in range(SSM_CONV):
        st = CONV_HALO + k - half
        y = y + ext[st:st + r, :] * w[k:k + 1, :]
    o_ref[...] = _silu(y).astype(BF16)


def _conv_silu(proj, conv_w, conv_b, seq, rblk):
    t = proj.shape[0]
    ncol = SSM_CONV_CH // PROJ_TN
    col0 = (MAIN_COLS - SSM_CONV_CH) // PROJ_TN
    rh = rblk // CONV_HALO
    nh = t // CONV_HALO
    return pl.pallas_call(
        functools.partial(_conv_body, blocks_per_seq=seq // rblk),
        grid=(t // rblk, ncol),
        in_specs=[
            pl.BlockSpec((CONV_HALO, PROJ_TN), lambda i, c: (jnp.maximum(i * rh - 1, 0), col0 + c)),
            pl.BlockSpec((rblk, PROJ_TN), lambda i, c: (i, col0 + c)),
            pl.BlockSpec((CONV_HALO, PROJ_TN), lambda i, c: (jnp.minimum((i + 1) * rh, nh - 1), col0 + c)),
            pl.BlockSpec((SSM_CONV, PROJ_TN), lambda i, c: (0, c)),
            pl.BlockSpec((1, PROJ_TN), lambda i, c: (0, c)),
        ],
        out_specs=pl.BlockSpec((rblk, PROJ_TN), lambda i, c: (i, c)),
        out_shape=jax.ShapeDtypeStruct((t, SSM_CONV_CH), BF16),
        compiler_params=_cparams(("arbitrary", "arbitrary")),
        name="ssd_conv",
    )(proj, proj, proj, conv_w, conv_b)


def _ssd_body(*refs, reverse, final):
    if final:
        (x_ref, b_ref, c_ref, dt_ref, bias_ref, a_ref, e_ref,
         yf_ref, z_ref, dsk_ref, ng_ref, o_ref, state_ref, y_ref) = refs
    else:
        x_ref, b_ref, c_ref, dt_ref, bias_ref, a_ref, e_ref, o_ref, state_ref, y_ref = refs
    q = SSM_CHUNK
    gw = SSM_W // SSM_GROUPS
    hpg = SSM_HEADS // SSM_GROUPS
    off = SSM_HEADS if reverse else 0
    end = 0 if reverse else q - 1

    @pl.when(pl.program_id(1) == 0)
    def _():
        state_ref[...] = jnp.zeros_like(state_ref)

    xr = dt_ref[...] + bias_ref[...]
    dt = jnp.maximum(xr, 0.0) + jnp.log1p(jnp.exp(-jnp.abs(xr)))
    dta = dt * a_ref[...]
    row = lax.broadcasted_iota(I32, (q, q), 0)
    col = lax.broadcasted_iota(I32, (q, q), 1)
    tri = (col >= row) if reverse else (row >= col)
    acs = _dot01_l(jnp.where(tri, 1.0, 0.0).astype(BF16), dta)
    acs_t = acs.T
    wide = _dot01_r(jnp.concatenate([dt, acs], axis=0), e_ref[...])
    dt_x = wide[:q]
    acs_x = wide[q:]
    e_acs = jnp.exp(acs_x)
    xs = x_ref[...].astype(F32)
    xdt = xs * dt_x
    xds = (xdt * jnp.exp(acs_x[end:end + 1, :] - acs_x)).astype(BF16)
    xdt = xdt.astype(BF16)

    for g in range(SSM_GROUPS):
        gs = slice(g * gw, (g + 1) * gw)
        bg = b_ref[:, g * SSM_STATE:(g + 1) * SSM_STATE]
        cg = c_ref[:, g * SSM_STATE:(g + 1) * SSM_STATE]
        cb = lax.dot_general(cg, bg, (((1,), (1,)), ((), ())), preferred_element_type=F32)
        for r in range(hpg):
            hh = g * hpg + r
            hs = slice(hh * HEAD_DIM, (hh + 1) * HEAD_DIM)
            diff = acs[:, off + hh:off + hh + 1] - acs_t[off + hh:off + hh + 1, :]
            decay = jnp.exp(jnp.where(tri, diff, -jnp.inf))
            y_ref[:, hs] = jnp.dot((cb * decay).astype(BF16), xdt[:, hs], preferred_element_type=F32)
        prev = state_ref[g]
        y_off = jnp.dot(cg, prev.astype(BF16), preferred_element_type=F32) * e_acs[:, gs]
        y_ref[:, gs] = y_ref[:, gs] + y_off
        bt = bg.astype(F32).T.astype(BF16)
        st = jnp.dot(bt, xds[:, gs], preferred_element_type=F32)
        state_ref[g] = e_acs[end:end + 1, gs] * prev + st

    if final:
        y = yf_ref[...] + y_ref[...] + dsk_ref[...] * xs
        y = y * _silu(z_ref[...].astype(F32))
        for g in range(SSM_GROUPS):
            gs = slice(g * gw, (g + 1) * gw)
            yg = y[:, gs]
            yg = yg * lax.rsqrt(jnp.mean(yg * yg, axis=-1, keepdims=True) + NORM_EPS)
            o_ref[:, gs] = (yg * ng_ref[:, gs]).astype(o_ref.dtype)
    else:
        o_ref[...] = y_ref[...]


def _ssd_scan(xc, dt_raw, bias_row, a_row, e_mat, bsz, seq, reverse, extra=None):
    t = bsz * seq
    q = SSM_CHUNK
    nc = seq // q
    final = extra is not None
    rowf = (lambda b, c: b * nc + (nc - 1 - c)) if reverse else (lambda b, c: b * nc + c)
    const = lambda w: pl.BlockSpec((1, w), lambda b, c: (0, 0))
    in_specs = [
        pl.BlockSpec((q, SSM_W), lambda b, c: (rowf(b, c), 0)),
        pl.BlockSpec((q, SSM_BC), lambda b, c: (rowf(b, c), SSM_W // SSM_BC)),
        pl.BlockSpec((q, SSM_BC), lambda b, c: (rowf(b, c), SSM_W // SSM_BC + 1)),
        pl.BlockSpec((q, LANES), lambda b, c: (rowf(b, c), 0)),
        const(LANES), const(LANES),
        pl.BlockSpec((LANES, SSM_W), lambda b, c: (0, 0)),
    ]
    args = [xc, xc, xc, dt_raw, bias_row, a_row, e_mat]
    if final:
        y_fwd, proj, dsk_row, ng_row = extra
        zcol = (3 * ATT_W * 2) // SSM_W
        in_specs += [pl.BlockSpec((q, SSM_W), lambda b, c: (rowf(b, c), 0)),
                     pl.BlockSpec((q, SSM_W), lambda b, c: (rowf(b, c), zcol)),
                     const(SSM_W), const(SSM_W)]
        args += [y_fwd, proj, dsk_row, ng_row]
    return pl.pallas_call(
        functools.partial(_ssd_body, reverse=reverse, final=final),
        grid=(bsz, nc),
        in_specs=in_specs,
        out_specs=pl.BlockSpec((q, SSM_W), lambda b, c: (rowf(b, c), 0)),
        out_shape=jax.ShapeDtypeStruct((t, SSM_W), BF16 if final else F32),
        scratch_shapes=[pltpu.VMEM((SSM_GROUPS, SSM_STATE, SSM_W // SSM_GROUPS), F32),
                        pltpu.VMEM((q, SSM_W), F32)],
        compiler_params=_cparams(("arbitrary", "arbitrary")),
        name="ssd_bwd" if reverse else "ssd_fwd",
    )(*args)


def _expand_matrix(reverse):
    e = np.zeros((LANES, SSM_W), np.float32)
    off = SSM_HEADS if reverse else 0
    for h in range(SSM_HEADS):
        e[off + h, h * HEAD_DIM:(h + 1) * HEAD_DIM] = 1.0
    return jnp.asarray(e, BF16)


def _outproj_body(*refs, router):
    if router:
        yna_ref, ydil_ref, yssd_ref, x_ref, w_ref, g_ref, wr_ref, xo_ref, hn_ref, rt_ref = refs
    else:
        yna_ref, ydil_ref, yssd_ref, x_ref, w_ref, g_ref, xo_ref, hn_ref = refs
    acc = x_ref[...]
    acc = acc + jnp.dot(yna_ref[...], w_ref[0:ATT_W, :], preferred_element_type=F32)
    acc = acc + jnp.dot(ydil_ref[...], w_ref[ATT_W:2 * ATT_W, :], preferred_element_type=F32)
    acc = acc + jnp.dot(yssd_ref[...], w_ref[2 * ATT_W:, :], preferred_element_type=F32)
    xo_ref[...] = acc
    ms = jnp.mean(acc * acc, axis=-1, keepdims=True)
    h = acc * lax.rsqrt(ms + NORM_EPS) * g_ref[...]
    hn_ref[...] = h.astype(hn_ref.dtype)
    if router:
        h_hi = h.astype(BF16)
        h_lo = (h - h_hi.astype(F32)).astype(BF16)
        wr = wr_ref[...]
        w_hi = wr.astype(BF16)
        w_lo = (wr - w_hi.astype(F32)).astype(BF16)
        d = lambda a, b: jnp.dot(a, b, preferred_element_type=F32)
        logits = d(h_hi, w_hi) + d(h_hi, w_lo) + d(h_lo, w_hi)
        lane = lax.broadcasted_iota(I32, logits.shape, 1)
        logits = jnp.where(lane < N_EXPERTS, logits, -jnp.inf)
        m1 = jnp.max(logits, axis=-1, keepdims=True)
        i1 = jnp.min(jnp.where(logits == m1, lane, LANES), axis=-1, keepdims=True)
        rest = jnp.where(lane == i1, -jnp.inf, logits)
        m2 = jnp.max(rest, axis=-1, keepdims=True)
        i2 = jnp.min(jnp.where(rest == m2, lane, LANES), axis=-1, keepdims=True)
        e = jnp.exp(m2 - m1)
        g1 = 1.0 / (1.0 + e)
        g2 = e / (1.0 + e)
        rt = jnp.where(lane == 0, i1.astype(F32),
                       jnp.where(lane == 1, i2.astype(F32),
                                 jnp.where(lane == 2, g1, jnp.where(lane == 3, g2, 0.0))))
        rt_ref[...] = rt


def _outproj(y_na, y_dil, y_ssd, x2, w_out, g, w_router, tm, hn_dtype):
    t, d = x2.shape
    router = w_router is not None
    row = lambda w: pl.BlockSpec((tm, w), lambda i: (i, 0))
    in_specs = [row(ATT_W), row(ATT_W), row(SSM_W), row(d),
                pl.BlockSpec((d, d), lambda i: (0, 0)),
                pl.BlockSpec((1, d), lambda i: (0, 0))]
    args = [y_na, y_dil, y_ssd, x2, w_out, g]
    out_specs = [row(d), row(d)]
    out_shape = [jax.ShapeDtypeStruct((t, d), F32), jax.ShapeDtypeStruct((t, d), hn_dtype)]
    if router:
        in_specs.append(pl.BlockSpec((d, LANES), lambda i: (0, 0)))
        args.append(w_router)
        out_specs.append(row(LANES))
        out_shape.append(jax.ShapeDtypeStruct((t, LANES), F32))
    return pl.pallas_call(
        functools.partial(_outproj_body, router=router),
        grid=(t // tm,),
        in_specs=in_specs, out_specs=out_specs, out_shape=out_shape,
        compiler_params=_cparams(("arbitrary",)),
        name="outproj_router" if router else "outproj",
    )(*args)


def _ffn_body(hn_ref, x_ref, wg_ref, wu_ref, wd_ref, o_ref, acc_ref):
    f = pl.program_id(1)

    @pl.when(f == 0)
    def _():
        acc_ref[...] = x_ref[...]

    h = hn_ref[...]
    g = jnp.dot(h, wg_ref[...], preferred_element_type=F32)
    u = jnp.dot(h, wu_ref[...], preferred_element_type=F32)
    a = (_silu(g) * u).astype(BF16)
    acc_ref[...] += jnp.dot(a, wd_ref[...], preferred_element_type=F32)

    @pl.when(f == pl.num_programs(1) - 1)
    def _():
        o_ref[...] = acc_ref[...]


def _ffn(hn, x2, wg, wu, wd, tm, tf):
    t, d = x2.shape
    fdim = wg.shape[1]
    return pl.pallas_call(
        _ffn_body,
        grid=(t // tm, fdim // tf),
        in_specs=[pl.BlockSpec((tm, d), lambda i, f: (i, 0)),
                  pl.BlockSpec((tm, d), lambda i, f: (i, 0)),
                  pl.BlockSpec((d, tf), lambda i, f: (0, f)),
                  pl.BlockSpec((d, tf), lambda i, f: (0, f)),
                  pl.BlockSpec((tf, d), lambda i, f: (f, 0))],
        out_specs=pl.BlockSpec((tm, d), lambda i, f: (i, 0)),
        out_shape=jax.ShapeDtypeStruct((t, d), F32),
        scratch_shapes=[pltpu.VMEM((tm, d), F32)],
        compiler_params=_cparams(("arbitrary", "arbitrary")),
        name="ffn",
    )(hn, x2, wg, wu, wd)


def _moe_row_copy(h_hbm, xg_ref, sem, tok, r):
    return pltpu.make_async_copy(h_hbm.at[pl.ds(tok, 1), :], xg_ref.at[pl.ds(r, 1), :], sem)


def _moe_ffn_body(tok_ref, be_ref, h_hbm, wg_ref, wu_ref, wd_ref, o_ref, xg_ref, xb_ref, acc_ref, sem):
    i = pl.program_id(0)
    f = pl.program_id(1)
    tm = xg_ref.shape[0]

    @pl.when(f == 0)
    def _():
        def issue(r, carry):
            _moe_row_copy(h_hbm, xg_ref, sem, tok_ref[i * tm + r], r).start()
            return carry
        lax.fori_loop(0, tm, issue, 0)

        def drain(r, carry):
            _moe_row_copy(h_hbm, xg_ref, sem, 0, r).wait()
            return carry
        lax.fori_loop(0, tm, drain, 0)
        xb_ref[...] = xg_ref[...].astype(BF16)
        acc_ref[...] = jnp.zeros_like(acc_ref)

    h = xb_ref[...]
    g = jnp.dot(h, wg_ref[0], preferred_element_type=F32)
    u = jnp.dot(h, wu_ref[0], preferred_element_type=F32)
    a = (_silu(g) * u).astype(BF16)
    acc_ref[...] += jnp.dot(a, wd_ref[0], preferred_element_type=F32)

    @pl.when(f == pl.num_programs(1) - 1)
    def _():
        o_ref[...] = acc_ref[...]


def _moe_ffn(row_tok, blk_expert, h32, wg, wu, wd, tm, tf):
    n_rows = row_tok.shape[0]
    d = h32.shape[1]
    fdim = wg.shape[2]
    grid_spec = pltpu.PrefetchScalarGridSpec(
        num_scalar_prefetch=2,
        grid=(n_rows // tm, fdim // tf),
        in_specs=[pl.BlockSpec(memory_space=pl.ANY),
                  pl.BlockSpec((1, d, tf), lambda i, f, tok, be: (be[i], 0, f)),
                  pl.BlockSpec((1, d, tf), lambda i, f, tok, be: (be[i], 0, f)),
                  pl.BlockSpec((1, tf, d), lambda i, f, tok, be: (be[i], f, 0))],
        out_specs=pl.BlockSpec((tm, d), lambda i, f, tok, be: (i, 0)),
        scratch_shapes=[pltpu.VMEM((tm, d), F32), pltpu.VMEM((tm, d), BF16),
                        pltpu.VMEM((tm, d), F32), pltpu.SemaphoreType.DMA(())],
    )
    return pl.pallas_call(
        _moe_ffn_body,
        grid_spec=grid_spec,
        out_shape=jax.ShapeDtypeStruct((n_rows, d), F32),
        compiler_params=_cparams(("arbitrary", "arbitrary")),
        name="moe_ffn",
    )(row_tok, blk_expert, h32, wg, wu, wd)


def _combine_row_copy(y_hbm, buf_ref, sem, src, k, r):
    return pltpu.make_async_copy(y_hbm.at[pl.ds(src, 1), :], buf_ref.at[k, pl.ds(r, 1), :], sem)


def _combine_body(pos_ref, y_hbm, x_ref, rt_ref, gf_ref, o_ref, buf_ref, sem, *, final):
    i = pl.program_id(0)
    tm = x_ref.shape[0]

    def issue(r, carry):
        for k in range(2):
            _combine_row_copy(y_hbm, buf_ref, sem, pos_ref[2 * (i * tm + r) + k], k, r).start()
        return carry
    lax.fori_loop(0, tm, issue, 0)

    def drain(r, carry):
        for k in range(2):
            _combine_row_copy(y_hbm, buf_ref, sem, 0, k, r).wait()
        return carry
    lax.fori_loop(0, tm, drain, 0)

    rt = rt_ref[...]
    out = x_ref[...] + rt[:, 2:3] * buf_ref[0] + rt[:, 3:4] * buf_ref[1]
    if final:
        ms = jnp.mean(out * out, axis=-1, keepdims=True)
        out = out * lax.rsqrt(ms + NORM_EPS) * gf_ref[...]
    o_ref[...] = out


def _combine(pos_flat, yb, x2, route, g_final, tm, final):
    t, d = x2.shape
    grid_spec = pltpu.PrefetchScalarGridSpec(
        num_scalar_prefetch=1,
        grid=(t // tm,),
        in_specs=[pl.BlockSpec(memory_space=pl.ANY),
                  pl.BlockSpec((tm, d), lambda i, pos: (i, 0)),
                  pl.BlockSpec((tm, LANES), lambda i, pos: (i, 0)),
                  pl.BlockSpec((1, d), lambda i, pos: (0, 0))],
        out_specs=pl.BlockSpec((tm, d), lambda i, pos: (i, 0)),
        scratch_shapes=[pltpu.VMEM((2, tm, d), F32), pltpu.SemaphoreType.DMA(())],
    )
    return pl.pallas_call(
        functools.partial(_combine_body, final=final),
        grid_spec=grid_spec,
        out_shape=jax.ShapeDtypeStruct((t, d), F32),
        compiler_params=_cparams(("arbitrary",)),
        name="moe_combine",
    )(pos_flat, yb, x2, route, g_final)


def _route_plan(route, t, tm):
    e_flat = route[:, 0:2].astype(I32).reshape(-1)
    n_assign = e_flat.shape[0]
    onehot = (e_flat[:, None] == jnp.arange(N_EXPERTS, dtype=I32)[None, :]).astype(I32)
    csum = jnp.cumsum(onehot, axis=0)
    rank = jnp.sum(onehot * (csum - 1), axis=1)
    counts = csum[-1]
    padded = ((counts + tm - 1) // tm) * tm
    pend = jnp.cumsum(padded)
    pstart = pend - padded
    dest = (pstart[e_flat] + rank).astype(I32)
    n_rows = n_assign + N_EXPERTS * tm
    row_tok = jnp.zeros((n_rows,), I32).at[dest].set(jnp.arange(n_assign, dtype=I32) // 2)
    blk_start = jnp.arange(n_rows // tm, dtype=I32) * tm
    blk_expert = jnp.minimum(jnp.searchsorted(pend, blk_start, side='right'), N_EXPERTS - 1).astype(I32)
    return row_tok, blk_expert, dest


def _rope_tables(seq):
    half = HEAD_DIM // 2
    inv = ROPE_THETA ** (-jnp.arange(0, HEAD_DIM, 2, dtype=F32) / HEAD_DIM)
    ang = jnp.arange(seq, dtype=F32)[:, None] * inv[None, :]
    cos, sin = jnp.cos(ang), jnp.sin(ang)
    reps = LANES // HEAD_DIM
    cos_t = jnp.concatenate([cos, cos] * reps, axis=1)
    sin_t = jnp.concatenate([-sin, sin] * reps, axis=1)
    del half
    return cos_t, sin_t


def _pick_tile(n, pref):
    while n % pref:
        pref //= 2
    return pref


def kernel(x, norm_mix, w_in, na_rpb, conv_w, conv_b, dt_bias, a_log, d_skip, ssm_norm, w_out, norm_ffn,
           ffn_w_gate, ffn_w_up, ffn_w_down, router_w, exp_w_gate, exp_w_up, exp_w_down, norm_final):
    bsz, seq, d = x.shape
    t = bsz * seq
    depth = w_in.shape[0]
    assert w_in.shape[2] == MAIN_COLS + 2 * SSM_HEADS and seq % (GRID_W * NA_RB) == 0
    x2 = x.reshape(t, d).astype(F32)

    tm_proj = _pick_tile(seq, 1024)
    tm_out = _pick_tile(t, 512)
    tm_ffn = _pick_tile(t, 512)
    tf = 512
    tm_moe = _pick_tile(t, 512)
    tm_comb = _pick_tile(t, 256)
    conv_rows = _pick_tile(seq, 1024)

    cos_t, sin_t = _rope_tables(seq)
    w_main = w_in[:, :, :MAIN_COLS].astype(BF16)
    w_dt = jnp.pad(w_in[:, :, MAIN_COLS:], ((0, 0), (0, 0), (0, LANES - 2 * SSM_HEADS))).astype(BF16)
    w_out_b = w_out.astype(BF16)
    wg_b, wu_b, wd_b = ffn_w_gate.astype(BF16), ffn_w_up.astype(BF16), ffn_w_down.astype(BF16)
    eg_b, eu_b, ed_b = exp_w_gate.astype(BF16), exp_w_up.astype(BF16), exp_w_down.astype(BF16)
    e_fwd, e_bwd = _expand_matrix(False), _expand_matrix(True)
    pad_row = lambda v: jnp.pad(v.reshape(1, -1).astype(F32), ((0, 0), (0, LANES - v.size)))

    for layer in range(depth):
        proj, dt_raw = _inproj(x2, norm_mix[layer].reshape(1, d), w_main[layer], w_dt[layer],
                               cos_t, sin_t, seq, tm_proj)
        y_na = _na_attention(proj, _na_bias_tables(na_rpb[layer], seq // GRID_W), bsz, seq)
        y_dil = _dilated_attention(proj, bsz, seq)
        xc = _conv_silu(proj, conv_w[layer].astype(F32), conv_b[layer].reshape(1, -1).astype(F32), seq, conv_rows)
        bias_row = pad_row(dt_bias[layer])
        a_row = pad_row(-jnp.exp(a_log[layer].astype(F32)))
        y_fwd = _ssd_scan(xc, dt_raw, bias_row, a_row, e_fwd, bsz, seq, False)
        dsk_row = jnp.repeat(d_skip[layer].astype(F32), HEAD_DIM).reshape(1, SSM_W)
        y_ssd = _ssd_scan(xc, dt_raw, bias_row, a_row, e_bwd, bsz, seq, True,
                          extra=(y_fwd, proj, dsk_row, ssm_norm[layer].reshape(1, SSM_W).astype(F32)))
        j = layer // 2
        g_ffn = norm_ffn[layer].reshape(1, d).astype(F32)
        if layer % 2 == 0:
            x2, hn = _outproj(y_na, y_dil, y_ssd, x2, w_out_b[layer], g_ffn, None, tm_out, BF16)
            x2 = _ffn(hn, x2, wg_b[j], wu_b[j], wd_b[j], tm_ffn, tf)
        else:
            wr = jnp.pad(router_w[j].astype(F32), ((0, 0), (0, LANES - N_EXPERTS)))
            x2, h32, route = _outproj(y_na, y_dil, y_ssd, x2, w_out_b[layer], g_ffn, wr, tm_out, F32)
            row_tok, blk_expert, dest = _route_plan(route, t, tm_moe)
            yb = _moe_ffn(row_tok, blk_expert, h32, eg_b[j], eu_b[j], ed_b[j], tm_moe, tf)
            final = layer == depth - 1
            x2 = _combine(dest, yb, x2, route, norm_final.reshape(1, d).astype(F32), tm_comb, final)
    if depth % 2 == 1:
        raise NotImplementedError("final norm is fused into the expert combine of the last (odd) layer")
    return x2.reshape(bsz, seq, d).astype(x.dtype)
```

```python
import functools

import numpy as np
import jax
import jax.numpy as jnp
from jax import lax
from jax.experimental import pallas as pl
from jax.experimental.pallas import tpu as pltpu

F32 = jnp.float32
BF16 = jnp.bfloat16
I32 = jnp.int32

GRID_W = 64
HEAD_DIM = 64
NA_KH = 8
NA_KW = 16
N_HEADS = 8
ATT_W = N_HEADS * HEAD_DIM
DIL_BRANCHES = ((128, 1), (512, 4), (2048, 16))
DIL_QBLK = 128
DIL_HALF = 64
SSM_HEADS = 16
SSM_GROUPS = 2
SSM_STATE = 128
SSM_CONV = 5
SSM_CHUNK = 128
SSM_W = SSM_HEADS * HEAD_DIM
SSM_BC = SSM_GROUPS * SSM_STATE
SSM_CONV_CH = SSM_W + 2 * SSM_BC
N_EXPERTS = 8
ROPE_THETA = 10000.0
NORM_EPS = 1e-6
NEG = -1e30

LANES = 128
VMEM_LIMIT = 56 * 1024 * 1024
PROJ_TN = 512
MAIN_COLS = 11 * PROJ_TN
NA_RB = 4


def _cparams(sem):
    return pltpu.CompilerParams(dimension_semantics=sem, vmem_limit_bytes=VMEM_LIMIT)


def _split3(x):
    hi = x.astype(BF16)
    r = x - hi.astype(F32)
    mid = r.astype(BF16)
    lo = (r - mid.astype(F32)).astype(BF16)
    return hi, mid, lo


def _dot01_l(m01, x):
    hi, mid, lo = _split3(x)
    d = lambda a: jnp.dot(m01, a, preferred_element_type=F32)
    return d(hi) + d(mid) + d(lo)


def _dot01_r(x, m01):
    hi, mid, lo = _split3(x)
    d = lambda a: jnp.dot(a, m01, preferred_element_type=F32)
    return d(hi) + d(mid) + d(lo)


def _silu(x):
    return x / (1.0 + jnp.exp(-x))


def _inproj_body(x_ref, g_ref, w_ref, wdt_ref, cos_ref, sin_ref, o_ref, odt_ref, xn_ref):
    j = pl.program_id(1)

    @pl.when(j == 0)
    def _():
        x = x_ref[...]
        ms = jnp.mean(x * x, axis=-1, keepdims=True)
        xn = (x * lax.rsqrt(ms + NORM_EPS) * g_ref[...]).astype(BF16)
        xn_ref[...] = xn
        odt_ref[...] = jnp.dot(xn, wdt_ref[...], preferred_element_type=F32)

    acc = jnp.dot(xn_ref[...], w_ref[...], preferred_element_type=F32)
    acc = acc * jnp.where((j == 0) | (j == 3), HEAD_DIM ** -0.5, 1.0).astype(F32)
    is_rope = (j == 3) | (j == 4)

    @pl.when(is_rope)
    def _():
        cos = cos_ref[...]
        sin = sin_ref[...]
        lane = lax.broadcasted_iota(I32, cos.shape, 1)
        first = (lane % HEAD_DIM) < (HEAD_DIM // 2)
        for c in range(PROJ_TN // LANES):
            xs = acc[:, c * LANES:(c + 1) * LANES]
            partner = jnp.where(first, pltpu.roll(xs, LANES - HEAD_DIM // 2, 1),
                                pltpu.roll(xs, HEAD_DIM // 2, 1))
            o_ref[:, c * LANES:(c + 1) * LANES] = (xs * cos + partner * sin).astype(BF16)

    @pl.when(jnp.logical_not(is_rope))
    def _():
        o_ref[...] = acc.astype(BF16)


def _inproj(x2, g, w_main, w_dt, cos_t, sin_t, seq, tm):
    t, d = x2.shape
    n_s = seq // tm
    return pl.pallas_call(
        _inproj_body,
        grid=(t // tm, MAIN_COLS // PROJ_TN),
        in_specs=[
            pl.BlockSpec((tm, d), lambda i, j: (i, 0)),
            pl.BlockSpec((1, d), lambda i, j: (0, 0)),
            pl.BlockSpec((d, PROJ_TN), lambda i, j: (0, j)),
            pl.BlockSpec((d, LANES), lambda i, j: (0, 0)),
            pl.BlockSpec((tm, LANES), lambda i, j: (i % n_s, 0)),
            pl.BlockSpec((tm, LANES), lambda i, j: (i % n_s, 0)),
        ],
        out_specs=[
            pl.BlockSpec((tm, PROJ_TN), lambda i, j: (i, j)),
            pl.BlockSpec((tm, LANES), lambda i, j: (i, 0)),
        ],
        out_shape=[jax.ShapeDtypeStruct((t, MAIN_COLS), BF16),
                   jax.ShapeDtypeStruct((t, LANES), F32)],
        scratch_shapes=[pltpu.VMEM((tm, d), BF16)],
        compiler_params=_cparams(("arbitrary", "arbitrary")),
        name="inproj",
    )(x2, g, w_main, w_dt, cos_t, sin_t)


def _na_bias_tables(rpb, rows):
    nb = rows // NA_RB
    c = np.arange(GRID_W)
    cs = np.clip(c - NA_KW // 2, 0, GRID_W - NA_KW)
    ck = np.arange(GRID_W)
    col_ok = (ck[None, :] >= cs[:, None]) & (ck[None, :] < cs[:, None] + NA_KW)
    dc = np.clip(ck[None, :] - c[:, None] + NA_KW - 1, 0, 2 * NA_KW - 2)
    onehot = (dc.reshape(-1)[None, :] == np.arange(2 * NA_KW - 1)[:, None]).astype(np.float32)
    tabs = []
    for i in (0, 1, nb - 1):
        r = NA_RB * i + np.arange(NA_RB)
        rs = np.clip(r - NA_KH // 2, 0, rows - NA_KH)
        absrow = NA_RB * (i - 1) + np.arange(3 * NA_RB)
        row_ok = (absrow[None, :] >= rs[:, None]) & (absrow[None, :] < rs[:, None] + NA_KH)
        dr = np.clip(absrow[None, :] - r[:, None] + NA_KH - 1, 0, 2 * NA_KH - 2)
        by_row = rpb.astype(F32)[:, :, dr, :]
        full = jnp.einsum('lhqkd,dx->lhqkx', by_row, jnp.asarray(onehot), precision=lax.Precision.HIGHEST)
        full = full.reshape(full.shape[:4] + (GRID_W, GRID_W))
        ok = row_ok[:, :, None, None] & col_ok[None, None, :, :]
        full = jnp.where(ok[None, None], full, NEG)
        full = jnp.transpose(full, (0, 1, 2, 4, 3, 5))
        tabs.append(full.reshape(full.shape[:2] + (NA_RB * GRID_W, 3 * NA_RB * GRID_W)))
    return jnp.stack(tabs, axis=1)


def _na_body(q_ref, kp_ref, kc_ref, kn_ref, vp_ref, vc_ref, vn_ref, b_ref, o_ref):
    for h in range(N_HEADS):
        sl = slice(h * HEAD_DIM, (h + 1) * HEAD_DIM)
        q = q_ref[:, sl]
        k = jnp.concatenate([kp_ref[:, sl], kc_ref[:, sl], kn_ref[:, sl]], axis=0)
        v = jnp.concatenate([vp_ref[:, sl], vc_ref[:, sl], vn_ref[:, sl]], axis=0)
        s = lax.dot_general(q, k, (((1,), (1,)), ((), ())), preferred_element_type=F32)
        s = s + b_ref[h]
        m = jnp.max(s, axis=-1, keepdims=True)
        p = jnp.exp(s - m)
        l = jnp.sum(p, axis=-1, keepdims=True)
        o = jnp.dot(p.astype(BF16), v, preferred_element_type=F32) / l
        o_ref[:, sl] = o.astype(BF16)


def _na_attention(proj, bias_tabs, layer, bsz, seq):
    t = bsz * seq
    tq = NA_RB * GRID_W
    nb = seq // tq
    assert nb >= 3 and seq // GRID_W >= NA_KH
    sel = lambda i: jnp.where(i == 0, 0, jnp.where(i == nb - 1, 2, 1))
    prev = lambda b, i: b * nb + jnp.maximum(i - 1, 0)
    cur = lambda b, i: b * nb + i
    nxt = lambda b, i: b * nb + jnp.minimum(i + 1, nb - 1)
    blk = lambda rowf, col: pl.BlockSpec((tq, ATT_W), lambda b, i: (rowf(b, i), col))
    return pl.pallas_call(
        _na_body,
        grid=(bsz, nb),
        in_specs=[blk(cur, 0), blk(prev, 1), blk(cur, 1), blk(nxt, 1),
                  blk(prev, 2), blk(cur, 2), blk(nxt, 2),
                  pl.BlockSpec((None, None, N_HEADS, tq, 3 * tq), lambda b, i: (layer, sel(i), 0, 0, 0))],
        out_specs=pl.BlockSpec((tq, ATT_W), lambda b, i: (cur(b, i), 0)),
        out_shape=jax.ShapeDtypeStruct((t, ATT_W), BF16),
        compiler_params=_cparams(("arbitrary", "arbitrary")),
        name="na_attn",
    )(proj, proj, proj, proj, proj, proj, proj, bias_tabs)


DIL_SUPER = 2048


def _dil_split_body(x_ref, *rest):
    o_refs, scr = rest[:-1], rest[-1]
    for c in range(ATT_W // LANES):
        scr[c] = x_ref[:, c * LANES:(c + 1) * LANES].astype(F32)
    for o_ref in o_refs:
        dil = o_ref.shape[2]
        n = o_ref.shape[3]
        for rho in range(dil):
            for c in range(ATT_W // LANES):
                o_ref[0, 0, rho, :, c * LANES:(c + 1) * LANES] = (
                    scr[c, pl.ds(rho, n, stride=dil), :].astype(BF16))


def _dil_split(proj, bsz, seq, dils):
    t = bsz * seq
    nsb = seq // DIL_SUPER
    out_specs = [pl.BlockSpec((1, 1, dil, DIL_SUPER // dil, ATT_W),
                              lambda i, c: (c, i // nsb, 0, i % nsb, 0)) for dil in dils]
    out_shape = [jax.ShapeDtypeStruct((3, bsz, dil, seq // dil, ATT_W), BF16) for dil in dils]
    return pl.pallas_call(
        _dil_split_body,
        grid=(t // DIL_SUPER, 3),
        in_specs=[pl.BlockSpec((DIL_SUPER, ATT_W), lambda i, c: (i, 3 + c))],
        out_specs=out_specs, out_shape=out_shape,
        scratch_shapes=[pltpu.VMEM((ATT_W // LANES, DIL_SUPER, LANES), F32)],
        compiler_params=_cparams(("arbitrary", "arbitrary")),
        name="dil_split",
    )(proj)


def _dil_body(q_ref, kp_ref, kc_ref, kn_ref, vp_ref, vc_ref, vn_ref, o_ref, l_ref, *, n_sub):
    u = pl.program_id(2)
    nk = 2 * DIL_QBLK
    a = lax.broadcasted_iota(I32, (DIL_QBLK, nk), 0)
    kk = lax.broadcasted_iota(I32, (DIL_QBLK, nk), 1)
    uk = u * DIL_QBLK - DIL_HALF + kk
    ok = (kk >= a) & (kk <= a + 2 * DIL_HALF) & (uk >= 0) & (uk < n_sub)
    lane = lax.broadcasted_iota(I32, (DIL_QBLK, LANES), 1)
    lse_tile = jnp.zeros((DIL_QBLK, LANES), F32)
    for h in range(N_HEADS):
        sl = slice(h * HEAD_DIM, (h + 1) * HEAD_DIM)
        q = q_ref[:, sl]
        k = jnp.concatenate([kp_ref[DIL_HALF:, sl], kc_ref[:, sl], kn_ref[:DIL_HALF, sl]], axis=0)
        v = jnp.concatenate([vp_ref[DIL_HALF:, sl], vc_ref[:, sl], vn_ref[:DIL_HALF, sl]], axis=0)
        s = lax.dot_general(q, k, (((1,), (1,)), ((), ())), preferred_element_type=F32)
        s = jnp.where(ok, s, NEG)
        m = jnp.max(s, axis=-1, keepdims=True)
        p = jnp.exp(s - m)
        l = jnp.sum(p, axis=-1, keepdims=True)
        o = jnp.dot(p.astype(BF16), v, preferred_element_type=F32) / l
        o_ref[:, sl] = o.astype(BF16)
        lse_tile = jnp.where(lane == h, m + jnp.log(l), lse_tile)
    l_ref[...] = lse_tile


def _dil_branch(src, bsz, seq, dil):
    n_sub = seq // dil
    nu = n_sub // DIL_QBLK
    prev = lambda u: jnp.maximum(u - 1, 0)
    cur = lambda u: u
    nxt = lambda u: jnp.minimum(u + 1, nu - 1)
    if dil == 1:
        blk = lambda uf, c: pl.BlockSpec((DIL_QBLK, ATT_W), lambda b, r, u: (b * nu + uf(u), 3 + c))
    else:
        blk = lambda uf, c: pl.BlockSpec((None, None, None, DIL_QBLK, ATT_W),
                                         lambda b, r, u: (c, b, r, uf(u), 0))
    out_blk = lambda w: pl.BlockSpec((None, None, DIL_QBLK, w), lambda b, r, u: (b, r, u, 0))
    return pl.pallas_call(
        functools.partial(_dil_body, n_sub=n_sub),
        grid=(bsz, dil, nu),
        in_specs=[blk(cur, 0), blk(prev, 1), blk(cur, 1), blk(nxt, 1), blk(prev, 2), blk(cur, 2), blk(nxt, 2)],
        out_specs=[out_blk(ATT_W), out_blk(LANES)],
        out_shape=[jax.ShapeDtypeStruct((bsz, dil, n_sub, ATT_W), BF16),
                   jax.ShapeDtypeStruct((bsz, dil, n_sub, LANES), F32)],
        compiler_params=_cparams(("arbitrary", "arbitrary", "arbitrary")),
        name=f"dil_attn_d{dil}",
    )(*([src] * 7))


def _dil_merge_body(*refs, dils):
    nbr = len(dils)
    o_refs, l_refs = refs[:nbr], refs[nbr:2 * nbr]
    e_ref, out_ref = refs[2 * nbr], refs[2 * nbr + 1]
    scr = refs[2 * nbr + 2:]
    outs, lses = [], []
    si = 0
    for n, dil in enumerate(dils):
        if dil == 1:
            outs.append(o_refs[n][0, 0].astype(F32))
            lses.append(l_refs[n][0, 0])
            continue
        so, sl = scr[si], scr[si + 1]
        si += 2
        rows = DIL_SUPER // dil
        for rho in range(dil):
            for c in range(ATT_W // LANES):
                so[c, pl.ds(rho, rows, stride=dil), :] = (
                    o_refs[n][0, rho, :, c * LANES:(c + 1) * LANES].astype(F32))
            sl[pl.ds(rho, rows, stride=dil), :] = l_refs[n][0, rho]
        outs.append(jnp.concatenate([so[c] for c in range(ATT_W // LANES)], axis=1))
        lses.append(sl[...])
    m = functools.reduce(jnp.maximum, lses)
    es = [jnp.exp(l - m) for l in lses]
    tot = functools.reduce(lambda x, y: x + y, es)
    acc = None
    for o, e in zip(outs, es):
        w = jnp.dot((e / tot).astype(BF16), e_ref[...], preferred_element_type=F32)
        acc = w * o if acc is None else acc + w * o
    out_ref[...] = acc.astype(BF16)


def _dil_merge(outs, lses, bsz, seq, dils):
    t = bsz * seq
    nsb = seq // DIL_SUPER
    spec = lambda dil, w: pl.BlockSpec((1, dil, DIL_SUPER // dil, w), lambda i: (i // nsb, 0, i % nsb, 0))
    e8 = np.zeros((LANES, ATT_W), np.float32)
    for h in range(N_HEADS):
        e8[h, h * HEAD_DIM:(h + 1) * HEAD_DIM] = 1.0
    scratch = []
    for dil in dils:
        if dil != 1:
            scratch += [pltpu.VMEM((ATT_W // LANES, DIL_SUPER, LANES), F32), pltpu.VMEM((DIL_SUPER, LANES), F32)]
    return pl.pallas_call(
        functools.partial(_dil_merge_body, dils=dils),
        grid=(t // DIL_SUPER,),
        in_specs=[spec(dil, ATT_W) for dil in dils] + [spec(dil, LANES) for dil in dils]
                 + [pl.BlockSpec((LANES, ATT_W), lambda i: (0, 0))],
        out_specs=pl.BlockSpec((DIL_SUPER, ATT_W), lambda i: (i, 0)),
        out_shape=jax.ShapeDtypeStruct((t, ATT_W), BF16),
        scratch_shapes=scratch,
        compiler_params=_cparams(("arbitrary",)),
        name="dil_merge",
    )(*outs, *lses, jnp.asarray(e8, BF16))


def _dilated_attention(proj, bsz, seq):
    dils = tuple(dil for _, dil in DIL_BRANCHES)
    for win, dil in DIL_BRANCHES:
        assert win // 2 // dil == DIL_HALF and DIL_SUPER % (dil * DIL_QBLK) == 0
    assert seq % DIL_SUPER == 0 and dils[0] == 1
    split = _dil_split(proj, bsz, seq, dils[1:])
    outs, lses = [], []
    for n, dil in enumerate(dils):
        o, l = _dil_branch(proj if dil == 1 else split[n - 1], bsz, seq, dil)
        outs.append(o)
        lses.append(l)
    return _dil_merge(outs, lses, bsz, seq, dils)


CONV_HALO = 16


def _conv_body(xp_ref, xc_ref, xn_ref, w_ref, b_ref, o_ref, *, blocks_per_seq):
    i = pl.program_id(0)
    r = xc_ref.shape[0]
    pos = i % blocks_per_seq
    keep_p = jnp.where(pos == 0, 0.0, 1.0).astype(F32)
    keep_n = jnp.where(pos == blocks_per_seq - 1, 0.0, 1.0).astype(F32)
    ext = jnp.concatenate([xp_ref[...].astype(F32) * keep_p, xc_ref[...].astype(F32),
                           xn_ref[...].astype(F32) * keep_n], axis=0)
    w = w_ref[...]
    half = SSM_CONV // 2
    y = b_ref[...] + jnp.zeros((r, w.shape[1]), F32)
    for k in range(SSM_CONV):
        st = CONV_HALO + k - half
        y = y + ext[st:st + r, :] * w[k:k + 1, :]
    o_ref[...] = _silu(y).astype(BF16)


def _conv_silu(proj, conv_w, conv_b, seq, rblk):
    t = proj.shape[0]
    ncol = SSM_CONV_CH // PROJ_TN
    col0 = (MAIN_COLS - SSM_CONV_CH) // PROJ_TN
    rh = rblk // CONV_HALO
    nh = t // CONV_HALO
    return pl.pallas_call(
        functools.partial(_conv_body, blocks_per_seq=seq // rblk),
        grid=(t // rblk, ncol),
        in_specs=[
            pl.BlockSpec((CONV_HALO, PROJ_TN), lambda i, c: (jnp.maximum(i * rh - 1, 0), col0 + c)),
            pl.BlockSpec((rblk, PROJ_TN), lambda i, c: (i, col0 + c)),
            pl.BlockSpec((CONV_HALO, PROJ_TN), lambda i, c: (jnp.minimum((i + 1) * rh, nh - 1), col0 + c)),
            pl.BlockSpec((SSM_CONV, PROJ_TN), lambda i, c: (0, c)),
            pl.BlockSpec((1, PROJ_TN), lambda i, c: (0, c)),
        ],
        out_specs=pl.BlockSpec((rblk, PROJ_TN), lambda i, c: (i, c)),
        out_shape=jax.ShapeDtypeStruct((t, SSM_CONV_CH), BF16),
        compiler_params=_cparams(("arbitrary", "arbitrary")),
        name="ssd_conv",
    )(proj, proj, proj, conv_w, conv_b)


def _ssd_body(*refs, reverse, final):
    if final:
        (x_ref, b_ref, c_ref, dt_ref, bias_ref, a_ref, e_ref,
         yf_ref, z_ref, dsk_ref, ng_ref, o_ref, state_ref, y_ref) = refs
    else:
        x_ref, b_ref, c_ref, dt_ref, bias_ref, a_ref, e_ref, o_ref, state_ref, y_ref = refs
    q = SSM_CHUNK
    gw = SSM_W // SSM_GROUPS
    hpg = SSM_HEADS // SSM_GROUPS
    off = SSM_HEADS if reverse else 0
    end = 0 if reverse else q - 1

    @pl.when(pl.program_id(1) == 0)
    def _():
        state_ref[...] = jnp.zeros_like(state_ref)

    xr = dt_ref[...] + bias_ref[...]
    dt = jnp.maximum(xr, 0.0) + jnp.log1p(jnp.exp(-jnp.abs(xr)))
    dta = dt * a_ref[...]
    row = lax.broadcasted_iota(I32, (q, q), 0)
    col = lax.broadcasted_iota(I32, (q, q), 1)
    tri = (col >= row) if reverse else (row >= col)
    acs = _dot01_l(jnp.where(tri, 1.0, 0.0).astype(BF16), dta)
    acs_t = acs.T
    wide = _dot01_r(jnp.concatenate([dt, acs], axis=0), e_ref[...])
    dt_x = wide[:q]
    acs_x = wide[q:]
    e_acs = jnp.exp(acs_x)
    xs = x_ref[...].astype(F32)
    xdt = xs * dt_x
    xds = (xdt * jnp.exp(acs_x[end:end + 1, :] - acs_x)).astype(BF16)
    xdt = xdt.astype(BF16)

    for g in range(SSM_GROUPS):
        gs = slice(g * gw, (g + 1) * gw)
        bg = b_ref[:, g * SSM_STATE:(g + 1) * SSM_STATE]
        cg = c_ref[:, g * SSM_STATE:(g + 1) * SSM_STATE]
        cb = lax.dot_general(cg, bg, (((1,), (1,)), ((), ())), preferred_element_type=F32)
        for r in range(hpg):
            hh = g * hpg + r
            hs = slice(hh * HEAD_DIM, (hh + 1) * HEAD_DIM)
            diff = acs[:, off + hh:off + hh + 1] - acs_t[off + hh:off + hh + 1, :]
            decay = jnp.exp(jnp.where(tri, diff, -jnp.inf))
            y_ref[:, hs] = jnp.dot((cb * decay).astype(BF16), xdt[:, hs], preferred_element_type=F32)
        prev = state_ref[g]
        y_off = jnp.dot(cg, prev.astype(BF16), preferred_element_type=F32) * e_acs[:, gs]
        y_ref[:, gs] = y_ref[:, gs] + y_off
        bt = bg.astype(F32).T.astype(BF16)
        st = jnp.dot(bt, xds[:, gs], preferred_element_type=F32)
        state_ref[g] = e_acs[end:end + 1, gs] * prev + st

    if final:
        y = yf_ref[...] + y_ref[...] + dsk_ref[...] * xs
        y = y * _silu(z_ref[...].astype(F32))
        for g in range(SSM_GROUPS):
            gs = slice(g * gw, (g + 1) * gw)
            yg = y[:, gs]
            yg = yg * lax.rsqrt(jnp.mean(yg * yg, axis=-1, keepdims=True) + NORM_EPS)
            o_ref[:, gs] = (yg * ng_ref[:, gs]).astype(o_ref.dtype)
    else:
        o_ref[...] = y_ref[...]


def _ssd_scan(xc, dt_raw, bias_row, a_row, e_mat, bsz, seq, reverse, extra=None):
    t = bsz * seq
    q = SSM_CHUNK
    nc = seq // q
    final = extra is not None
    rowf = (lambda b, c: b * nc + (nc - 1 - c)) if reverse else (lambda b, c: b * nc + c)
    const = lambda w: pl.BlockSpec((1, w), lambda b, c: (0, 0))
    in_specs = [
        pl.BlockSpec((q, SSM_W), lambda b, c: (rowf(b, c), 0)),
        pl.BlockSpec((q, SSM_BC), lambda b, c: (rowf(b, c), SSM_W // SSM_BC)),
        pl.BlockSpec((q, SSM_BC), lambda b, c: (rowf(b, c), SSM_W // SSM_BC + 1)),
        pl.BlockSpec((q, LANES), lambda b, c: (rowf(b, c), 0)),
        const(LANES), const(LANES),
        pl.BlockSpec((LANES, SSM_W), lambda b, c: (0, 0)),
    ]
    args = [xc, xc, xc, dt_raw, bias_row, a_row, e_mat]
    if final:
        y_fwd, proj, dsk_row, ng_row = extra
        zcol = (3 * ATT_W * 2) // SSM_W
        in_specs += [pl.BlockSpec((q, SSM_W), lambda b, c: (rowf(b, c), 0)),
                     pl.BlockSpec((q, SSM_W), lambda b, c: (rowf(b, c), zcol)),
                     const(SSM_W), const(SSM_W)]
        args += [y_fwd, proj, dsk_row, ng_row]
    return pl.pallas_call(
        functools.partial(_ssd_body, reverse=reverse, final=final),
        grid=(bsz, nc),
        in_specs=in_specs,
        out_specs=pl.BlockSpec((q, SSM_W), lambda b, c: (rowf(b, c), 0)),
        out_shape=jax.ShapeDtypeStruct((t, SSM_W), BF16 if final else F32),
        scratch_shapes=[pltpu.VMEM((SSM_GROUPS, SSM_STATE, SSM_W // SSM_GROUPS), F32),
                        pltpu.VMEM((q, SSM_W), F32)],
        compiler_params=_cparams(("arbitrary", "arbitrary")),
        name="ssd_bwd" if reverse else "ssd_fwd",
    )(*args)


def _expand_matrix(reverse):
    e = np.zeros((LANES, SSM_W), np.float32)
    off = SSM_HEADS if reverse else 0
    for h in range(SSM_HEADS):
        e[off + h, h * HEAD_DIM:(h + 1) * HEAD_DIM] = 1.0
    return jnp.asarray(e, BF16)


def _outproj_body(*refs, router):
    if router:
        yna_ref, ydil_ref, yssd_ref, x_ref, w_ref, g_ref, wr_ref, xo_ref, hn_ref, rt_ref = refs
    else:
        yna_ref, ydil_ref, yssd_ref, x_ref, w_ref, g_ref, xo_ref, hn_ref = refs
    acc = x_ref[...]
    acc = acc + jnp.dot(yna_ref[...], w_ref[0:ATT_W, :], preferred_element_type=F32)
    acc = acc + jnp.dot(ydil_ref[...], w_ref[ATT_W:2 * ATT_W, :], preferred_element_type=F32)
    acc = acc + jnp.dot(yssd_ref[...], w_ref[2 * ATT_W:, :], preferred_element_type=F32)
    xo_ref[...] = acc
    ms = jnp.mean(acc * acc, axis=-1, keepdims=True)
    h = acc * lax.rsqrt(ms + NORM_EPS) * g_ref[...]
    hn_ref[...] = h.astype(hn_ref.dtype)
    if router:
        h_hi = h.astype(BF16)
        h_lo = (h - h_hi.astype(F32)).astype(BF16)
        wr = wr_ref[...]
        w_hi = wr.astype(BF16)
        w_lo = (wr - w_hi.astype(F32)).astype(BF16)
        d = lambda a, b: jnp.dot(a, b, preferred_element_type=F32)
        logits = d(h_hi, w_hi) + d(h_hi, w_lo) + d(h_lo, w_hi)
        lane = lax.broadcasted_iota(I32, logits.shape, 1)
        logits = jnp.where(lane < N_EXPERTS, logits, -jnp.inf)
        m1 = jnp.max(logits, axis=-1, keepdims=True)
        i1 = jnp.min(jnp.where(logits == m1, lane, LANES), axis=-1, keepdims=True)
        rest = jnp.where(lane == i1, -jnp.inf, logits)
        m2 = jnp.max(rest, axis=-1, keepdims=True)
        i2 = jnp.min(jnp.where(rest == m2, lane, LANES), axis=-1, keepdims=True)
        e = jnp.exp(m2 - m1)
        g1 = 1.0 / (1.0 + e)
        g2 = e / (1.0 + e)
        rt = jnp.where(lane == 0, i1.astype(F32),
                       jnp.where(lane == 1, i2.astype(F32),
                                 jnp.where(lane == 2, g1, jnp.where(lane == 3, g2, 0.0))))
        rt_ref[...] = rt


def _outproj(y_na, y_dil, y_ssd, x2, w_out, g, w_router, tm, hn_dtype):
    t, d = x2.shape
    router = w_router is not None
    row = lambda w: pl.BlockSpec((tm, w), lambda i: (i, 0))
    in_specs = [row(ATT_W), row(ATT_W), row(SSM_W), row(d),
                pl.BlockSpec((d, d), lambda i: (0, 0)),
                pl.BlockSpec((1, d), lambda i: (0, 0))]
    args = [y_na, y_dil, y_ssd, x2, w_out, g]
    out_specs = [row(d), row(d)]
    out_shape = [jax.ShapeDtypeStruct((t, d), F32), jax.ShapeDtypeStruct((t, d), hn_dtype)]
    if router:
        in_specs.append(pl.BlockSpec((d, LANES), lambda i: (0, 0)))
        args.append(w_router)
        out_specs.append(row(LANES))
        out_shape.append(jax.ShapeDtypeStruct((t, LANES), F32))
    return pl.pallas_call(
        functools.partial(_outproj_body, router=router),
        grid=(t // tm,),
        in_specs=in_specs, out_specs=out_specs, out_shape=out_shape,
        compiler_params=_cparams(("arbitrary",)),
        name="outproj_router" if router else "outproj",
    )(*args)


def _ffn_body(hn_ref, x_ref, wg_ref, wu_ref, wd_ref, o_ref, acc_ref):
    f = pl.program_id(1)

    @pl.when(f == 0)
    def _():
        acc_ref[...] = x_ref[...]

    h = hn_ref[...]
    g = jnp.dot(h, wg_ref[...], preferred_element_type=F32)
    u = jnp.dot(h, wu_ref[...], preferred_element_type=F32)
    a = (_silu(g) * u).astype(BF16)
    acc_ref[...] += jnp.dot(a, wd_ref[...], preferred_element_type=F32)

    @pl.when(f == pl.num_programs(1) - 1)
    def _():
        o_ref[...] = acc_ref[...]


def _ffn(hn, x2, wg, wu, wd, tm, tf):
    t, d = x2.shape
    fdim = wg.shape[1]
    return pl.pallas_call(
        _ffn_body,
        grid=(t // tm, fdim // tf),
        in_specs=[pl.BlockSpec((tm, d), lambda i, f: (i, 0)),
                  pl.BlockSpec((tm, d), lambda i, f: (i, 0)),
                  pl.BlockSpec((d, tf), lambda i, f: (0, f)),
                  pl.BlockSpec((d, tf), lambda i, f: (0, f)),
                  pl.BlockSpec((tf, d), lambda i, f: (f, 0))],
        out_specs=pl.BlockSpec((tm, d), lambda i, f: (i, 0)),
        out_shape=jax.ShapeDtypeStruct((t, d), F32),
        scratch_shapes=[pltpu.VMEM((tm, d), F32)],
        compiler_params=_cparams(("arbitrary", "arbitrary")),
        name="ffn",
    )(hn, x2, wg, wu, wd)


def _moe_row_copy(h_hbm, xg_ref, sem, tok, slot, r):
    return pltpu.make_async_copy(h_hbm.at[pl.ds(tok, 1), :], xg_ref.at[slot, pl.ds(r, 1), :], sem.at[slot])


def _moe_ffn_body(tok_ref, be_ref, nu_ref, h_hbm, wg_ref, wu_ref, wd_ref, o_ref, xg_ref, xb_ref, acc_ref, sem):
    i = pl.program_id(0)
    f = pl.program_id(1)
    tm = xb_ref.shape[0]
    n_used = nu_ref[0]
    active = i < n_used
    slot = i % 2

    def start_gather(blk, dst_slot):
        def issue(r, carry):
            _moe_row_copy(h_hbm, xg_ref, sem, tok_ref[blk * tm + r], dst_slot, r).start()
            return carry
        lax.fori_loop(0, tm, issue, 0)

    @pl.when((f == 0) & (i == 0))
    def _():
        start_gather(0, 0)

    @pl.when((f == 0) & active)
    def _():
        def drain(r, carry):
            _moe_row_copy(h_hbm, xg_ref, sem, 0, slot, r).wait()
            return carry
        lax.fori_loop(0, tm, drain, 0)
        xb_ref[...] = xg_ref[slot].astype(BF16)
        acc_ref[...] = jnp.zeros_like(acc_ref)

    @pl.when((f == 1) & (i + 1 < n_used))
    def _():
        start_gather(i + 1, 1 - slot)

    @pl.when(active)
    def _():
        h = xb_ref[...]
        g = jnp.dot(h, wg_ref[0], preferred_element_type=F32)
        u = jnp.dot(h, wu_ref[0], preferred_element_type=F32)
        a = (_silu(g) * u).astype(BF16)
        acc_ref[...] += jnp.dot(a, wd_ref[0], preferred_element_type=F32)

    @pl.when(active & (f == pl.num_programs(1) - 1))
    def _():
        o_ref[...] = acc_ref[...]

    @pl.when(jnp.logical_not(active) & (f == 0))
    def _():
        o_ref[...] = jnp.zeros_like(o_ref)


def _moe_ffn(row_tok, blk_expert, n_used, h32, wg, wu, wd, tm, tf):
    n_rows = row_tok.shape[0]
    d = h32.shape[1]
    fdim = wg.shape[2]
    nf = fdim // tf
    assert nf >= 2
    fidx = lambda i, f, nu: jnp.where(i < nu[0], f, nf - 1)
    grid_spec = pltpu.PrefetchScalarGridSpec(
        num_scalar_prefetch=3,
        grid=(n_rows // tm, nf),
        in_specs=[pl.BlockSpec(memory_space=pl.ANY),
                  pl.BlockSpec((1, d, tf), lambda i, f, tok, be, nu: (be[i], 0, fidx(i, f, nu))),
                  pl.BlockSpec((1, d, tf), lambda i, f, tok, be, nu: (be[i], 0, fidx(i, f, nu))),
                  pl.BlockSpec((1, tf, d), lambda i, f, tok, be, nu: (be[i], fidx(i, f, nu), 0))],
        out_specs=pl.BlockSpec((tm, d), lambda i, f, tok, be, nu: (i, 0)),
        scratch_shapes=[pltpu.VMEM((2, tm, d), F32), pltpu.VMEM((tm, d), BF16),
                        pltpu.VMEM((tm, d), F32), pltpu.SemaphoreType.DMA((2,))],
    )
    return pl.pallas_call(
        _moe_ffn_body,
        grid_spec=grid_spec,
        out_shape=jax.ShapeDtypeStruct((n_rows, d), F32),
        compiler_params=_cparams(("arbitrary", "arbitrary")),
        name="moe_ffn",
    )(row_tok, blk_expert, n_used, h32, wg, wu, wd)


def _combine_row_copy(y_hbm, buf_ref, sem, src, slot, k, r):
    return pltpu.make_async_copy(y_hbm.at[pl.ds(src, 1), :], buf_ref.at[slot, k, pl.ds(r, 1), :], sem.at[slot])


def _combine_body(pos_ref, y_hbm, x_ref, rt_ref, gf_ref, o_ref, buf_ref, sem, *, final):
    i = pl.program_id(0)
    tm = x_ref.shape[0]
    slot = i % 2

    def start_gather(step, dst_slot):
        def issue(r, carry):
            for k in range(2):
                _combine_row_copy(y_hbm, buf_ref, sem, pos_ref[2 * (step * tm + r) + k], dst_slot, k, r).start()
            return carry
        lax.fori_loop(0, tm, issue, 0)

    @pl.when(i == 0)
    def _():
        start_gather(0, 0)

    @pl.when(i + 1 < pl.num_programs(0))
    def _():
        start_gather(i + 1, 1 - slot)

    def drain(r, carry):
        for k in range(2):
            _combine_row_copy(y_hbm, buf_ref, sem, 0, slot, k, r).wait()
        return carry
    lax.fori_loop(0, tm, drain, 0)

    rt = rt_ref[...]
    out = x_ref[...] + rt[:, 2:3] * buf_ref[slot, 0] + rt[:, 3:4] * buf_ref[slot, 1]
    if final:
        ms = jnp.mean(out * out, axis=-1, keepdims=True)
        out = out * lax.rsqrt(ms + NORM_EPS) * gf_ref[...]
    o_ref[...] = out


def _combine(pos_flat, yb, x2, route, g_final, tm, final):
    t, d = x2.shape
    grid_spec = pltpu.PrefetchScalarGridSpec(
        num_scalar_prefetch=1,
        grid=(t // tm,),
        in_specs=[pl.BlockSpec(memory_space=pl.ANY),
                  pl.BlockSpec((tm, d), lambda i, pos: (i, 0)),
                  pl.BlockSpec((tm, LANES), lambda i, pos: (i, 0)),
                  pl.BlockSpec((1, d), lambda i, pos: (0, 0))],
        out_specs=pl.BlockSpec((tm, d), lambda i, pos: (i, 0)),
        scratch_shapes=[pltpu.VMEM((2, 2, tm, d), F32), pltpu.SemaphoreType.DMA((2,))],
    )
    return pl.pallas_call(
        functools.partial(_combine_body, final=final),
        grid_spec=grid_spec,
        out_shape=jax.ShapeDtypeStruct((t, d), F32),
        compiler_params=_cparams(("arbitrary",)),
        name="moe_combine",
    )(pos_flat, yb, x2, route, g_final)


def _route_plan(route, t, tm):
    e_flat = route[:, 0:2].astype(I32).reshape(-1)
    n_assign = e_flat.shape[0]
    onehot = (e_flat[:, None] == jnp.arange(N_EXPERTS, dtype=I32)[None, :]).astype(I32)
    csum = jnp.cumsum(onehot, axis=0)
    rank = jnp.sum(onehot * (csum - 1), axis=1)
    counts = csum[-1]
    padded = ((counts + tm - 1) // tm) * tm
    pend = jnp.cumsum(padded)
    pstart = pend - padded
    dest = (pstart[e_flat] + rank).astype(I32)
    n_rows = n_assign + N_EXPERTS * tm
    row_tok = jnp.zeros((n_rows,), I32).at[dest].set(jnp.arange(n_assign, dtype=I32) // 2)
    blk_start = jnp.arange(n_rows // tm, dtype=I32) * tm
    blk_expert = jnp.minimum(jnp.searchsorted(pend, blk_start, side='right'), N_EXPERTS - 1).astype(I32)
    n_used = (pend[-1:] // tm).astype(I32)
    return row_tok, blk_expert, n_used, dest


def _rope_tables(seq):
    half = HEAD_DIM // 2
    inv = ROPE_THETA ** (-jnp.arange(0, HEAD_DIM, 2, dtype=F32) / HEAD_DIM)
    ang = jnp.arange(seq, dtype=F32)[:, None] * inv[None, :]
    cos, sin = jnp.cos(ang), jnp.sin(ang)
    reps = LANES // HEAD_DIM
    cos_t = jnp.concatenate([cos, cos] * reps, axis=1)
    sin_t = jnp.concatenate([-sin, sin] * reps, axis=1)
    del half
    return cos_t, sin_t


def _pick_tile(n, pref):
    while n % pref:
        pref //= 2
    return pref


def kernel(x, norm_mix, w_in, na_rpb, conv_w, conv_b, dt_bias, a_log, d_skip, ssm_norm, w_out, norm_ffn,
           ffn_w_gate, ffn_w_up, ffn_w_down, router_w, exp_w_gate, exp_w_up, exp_w_down, norm_final):
    bsz, seq, d = x.shape
    t = bsz * seq
    depth = w_in.shape[0]
    assert w_in.shape[2] == MAIN_COLS + 2 * SSM_HEADS and seq % (GRID_W * NA_RB) == 0
    x2 = x.reshape(t, d).astype(F32)

    tm_proj = _pick_tile(seq, 1024)
    tm_out = _pick_tile(t, 512)
    tm_ffn = _pick_tile(t, 512)
    tf = 512
    tm_moe = _pick_tile(t, 512)
    tm_comb = _pick_tile(t, 256)
    conv_rows = _pick_tile(seq, 1024)

    cos_t, sin_t = _rope_tables(seq)
    w_main = w_in.astype(BF16)
    w_dt = jnp.pad(w_in[:, :, MAIN_COLS:], ((0, 0), (0, 0), (0, LANES - 2 * SSM_HEADS))).astype(BF16)
    w_out_b = w_out.astype(BF16)
    wg_b, wu_b, wd_b = ffn_w_gate.astype(BF16), ffn_w_up.astype(BF16), ffn_w_down.astype(BF16)
    eg_b, eu_b, ed_b = exp_w_gate.astype(BF16), exp_w_up.astype(BF16), exp_w_down.astype(BF16)
    e_fwd, e_bwd = _expand_matrix(False), _expand_matrix(True)
    na_tabs = _na_bias_tables(na_rpb, seq // GRID_W)
    pad_row = lambda v: jnp.pad(v.reshape(1, -1).astype(F32), ((0, 0), (0, LANES - v.size)))

    for layer in range(depth):
        proj, dt_raw = _inproj(x2, norm_mix[layer].reshape(1, d), w_main[layer], w_dt[layer],
                               cos_t, sin_t, seq, tm_proj)
        y_na = _na_attention(proj, na_tabs, layer, bsz, seq)
        y_dil = _dilated_attention(proj, bsz, seq)
        xc = _conv_silu(proj, conv_w[layer].astype(F32), conv_b[layer].reshape(1, -1).astype(F32), seq, conv_rows)
        bias_row = pad_row(dt_bias[layer])
        a_row = pad_row(-jnp.exp(a_log[layer].astype(F32)))
        y_fwd = _ssd_scan(xc, dt_raw, bias_row, a_row, e_fwd, bsz, seq, False)
        dsk_row = jnp.repeat(d_skip[layer].astype(F32), HEAD_DIM).reshape(1, SSM_W)
        y_ssd = _ssd_scan(xc, dt_raw, bias_row, a_row, e_bwd, bsz, seq, True,
                          extra=(y_fwd, proj, dsk_row, ssm_norm[layer].reshape(1, SSM_W).astype(F32)))
        j = layer // 2
        g_ffn = norm_ffn[layer].reshape(1, d).astype(F32)
        if layer % 2 == 0:
            x2, hn = _outproj(y_na, y_dil, y_ssd, x2, w_out_b[layer], g_ffn, None, tm_out, BF16)
            x2 = _ffn(hn, x2, wg_b[j], wu_b[j], wd_b[j], tm_ffn, tf)
        else:
            wr = jnp.pad(router_w[j].astype(F32), ((0, 0), (0, LANES - N_EXPERTS)))
            x2, h32, route = _outproj(y_na, y_dil, y_ssd, x2, w_out_b[layer], g_ffn, wr, tm_out, F32)
            row_tok, blk_expert, n_used, dest = _route_plan(route, t, tm_moe)
            yb = _moe_ffn(row_tok, blk_expert, n_used, h32, eg_b[j], eu_b[j], ed_b[j], tm_moe, tf)
            final = layer == depth - 1
            x2 = _combine(dest, yb, x2, route, norm_final.reshape(1, d).astype(F32), tm_comb, final)
    if depth % 2 == 1:
        raise NotImplementedError("final norm is fused into the expert combine of the last (odd) layer")
    return x2.reshape(bsz, seq, d).astype(x.dtype)
```

```python
import functools

import numpy as np
import jax
import jax.numpy as jnp
from jax import lax
from jax.experimental import pallas as pl
from jax.experimental.pallas import tpu as pltpu

F32 = jnp.float32
BF16 = jnp.bfloat16
I32 = jnp.int32

GRID_W = 64
HEAD_DIM = 64
NA_KH = 8
NA_KW = 16
N_HEADS = 8
ATT_W = N_HEADS * HEAD_DIM
DIL_BRANCHES = ((128, 1), (512, 4), (2048, 16))
DIL_QBLK = 128
DIL_HALF = 64
SSM_HEADS = 16
SSM_GROUPS = 2
SSM_STATE = 128
SSM_CONV = 5
SSM_CHUNK = 128
SSM_W = SSM_HEADS * HEAD_DIM
SSM_BC = SSM_GROUPS * SSM_STATE
SSM_CONV_CH = SSM_W + 2 * SSM_BC
N_EXPERTS = 8
ROPE_THETA = 10000.0
NORM_EPS = 1e-6
NEG = -1e30

LANES = 128
VMEM_LIMIT = 56 * 1024 * 1024
PROJ_TN = 512
MAIN_COLS = 11 * PROJ_TN
NA_RB = 4
DMA_UNROLL = 8


def _cparams(sem):
    return pltpu.CompilerParams(dimension_semantics=sem, vmem_limit_bytes=VMEM_LIMIT)


def _split3(x):
    hi = x.astype(BF16)
    r = x - hi.astype(F32)
    mid = r.astype(BF16)
    lo = (r - mid.astype(F32)).astype(BF16)
    return hi, mid, lo


def _dot01_l(m01, x):
    hi, mid, lo = _split3(x)
    d = lambda a: jnp.dot(m01, a, preferred_element_type=F32)
    return d(hi) + d(mid) + d(lo)


def _dot01_r(x, m01):
    hi, mid, lo = _split3(x)
    d = lambda a: jnp.dot(a, m01, preferred_element_type=F32)
    return d(hi) + d(mid) + d(lo)


def _silu(x):
    return x / (1.0 + jnp.exp(-x))


def _inproj_body(x_ref, g_ref, w_ref, wdt_ref, cos_ref, sin_ref, o_ref, odt_ref, xn_ref):
    j = pl.program_id(1)

    @pl.when(j == 0)
    def _():
        x = x_ref[...]
        ms = jnp.mean(x * x, axis=-1, keepdims=True)
        xn = (x * lax.rsqrt(ms + NORM_EPS) * g_ref[...]).astype(BF16)
        xn_ref[...] = xn
        odt_ref[...] = jnp.dot(xn, wdt_ref[...], preferred_element_type=F32)

    acc = jnp.dot(xn_ref[...], w_ref[...], preferred_element_type=F32)
    acc = acc * jnp.where((j == 0) | (j == 3), HEAD_DIM ** -0.5, 1.0).astype(F32)
    is_rope = (j == 3) | (j == 4)

    @pl.when(is_rope)
    def _():
        cos = cos_ref[...]
        sin = sin_ref[...]
        lane = lax.broadcasted_iota(I32, cos.shape, 1)
        first = (lane % HEAD_DIM) < (HEAD_DIM // 2)
        for c in range(PROJ_TN // LANES):
            xs = acc[:, c * LANES:(c + 1) * LANES]
            partner = jnp.where(first, pltpu.roll(xs, LANES - HEAD_DIM // 2, 1),
                                pltpu.roll(xs, HEAD_DIM // 2, 1))
            o_ref[:, c * LANES:(c + 1) * LANES] = (xs * cos + partner * sin).astype(BF16)

    @pl.when(jnp.logical_not(is_rope))
    def _():
        o_ref[...] = acc.astype(BF16)


def _inproj(x2, g, w_main, w_dt, layer, cos_t, sin_t, seq, tm):
    t, d = x2.shape
    n_s = seq // tm
    return pl.pallas_call(
        _inproj_body,
        grid=(t // tm, MAIN_COLS // PROJ_TN),
        in_specs=[
            pl.BlockSpec((tm, d), lambda i, j: (i, 0)),
            pl.BlockSpec((1, d), lambda i, j: (0, 0)),
            pl.BlockSpec((None, d, PROJ_TN), lambda i, j: (layer, 0, j)),
            pl.BlockSpec((None, d, LANES), lambda i, j: (layer, 0, 0)),
            pl.BlockSpec((tm, LANES), lambda i, j: (i % n_s, 0)),
            pl.BlockSpec((tm, LANES), lambda i, j: (i % n_s, 0)),
        ],
        out_specs=[
            pl.BlockSpec((tm, PROJ_TN), lambda i, j: (i, j)),
            pl.BlockSpec((tm, LANES), lambda i, j: (i, 0)),
        ],
        out_shape=[jax.ShapeDtypeStruct((t, MAIN_COLS), BF16),
                   jax.ShapeDtypeStruct((t, LANES), F32)],
        scratch_shapes=[pltpu.VMEM((tm, d), BF16)],
        compiler_params=_cparams(("arbitrary", "arbitrary")),
        name="inproj",
    )(x2, g, w_main, w_dt, cos_t, sin_t)


def _na_bias_tables(rpb, rows):
    nb = rows // NA_RB
    c = np.arange(GRID_W)
    cs = np.clip(c - NA_KW // 2, 0, GRID_W - NA_KW)
    ck = np.arange(GRID_W)
    col_ok = (ck[None, :] >= cs[:, None]) & (ck[None, :] < cs[:, None] + NA_KW)
    dc = np.clip(ck[None, :] - c[:, None] + NA_KW - 1, 0, 2 * NA_KW - 2)
    onehot = (dc.reshape(-1)[None, :] == np.arange(2 * NA_KW - 1)[:, None]).astype(np.float32)
    tabs = []
    for i in (0, 1, nb - 1):
        r = NA_RB * i + np.arange(NA_RB)
        rs = np.clip(r - NA_KH // 2, 0, rows - NA_KH)
        absrow = NA_RB * (i - 1) + np.arange(3 * NA_RB)
        row_ok = (absrow[None, :] >= rs[:, None]) & (absrow[None, :] < rs[:, None] + NA_KH)
        dr = np.clip(absrow[None, :] - r[:, None] + NA_KH - 1, 0, 2 * NA_KH - 2)
        by_row = rpb.astype(F32)[:, :, dr, :]
        full = jnp.einsum('lhqkd,dx->lhqkx', by_row, jnp.asarray(onehot), precision=lax.Precision.HIGHEST)
        full = full.reshape(full.shape[:4] + (GRID_W, GRID_W))
        ok = row_ok[:, :, None, None] & col_ok[None, None, :, :]
        full = jnp.where(ok[None, None], full, NEG)
        full = jnp.transpose(full, (0, 1, 2, 4, 3, 5))
        tabs.append(full.reshape(full.shape[:2] + (NA_RB * GRID_W, 3 * NA_RB * GRID_W)))
    return jnp.stack(tabs, axis=1)


def _na_body(q_ref, kp_ref, kc_ref, kn_ref, vp_ref, vc_ref, vn_ref, b_ref, o_ref):
    for h in range(N_HEADS):
        sl = slice(h * HEAD_DIM, (h + 1) * HEAD_DIM)
        q = q_ref[:, sl]
        k = jnp.concatenate([kp_ref[:, sl], kc_ref[:, sl], kn_ref[:, sl]], axis=0)
        v = jnp.concatenate([vp_ref[:, sl], vc_ref[:, sl], vn_ref[:, sl]], axis=0)
        s = lax.dot_general(q, k, (((1,), (1,)), ((), ())), preferred_element_type=F32)
        s = s + b_ref[h]
        m = jnp.max(s, axis=-1, keepdims=True)
        p = jnp.exp(s - m)
        l = jnp.sum(p, axis=-1, keepdims=True)
        o = jnp.dot(p.astype(BF16), v, preferred_element_type=F32) / l
        o_ref[:, sl] = o.astype(BF16)


def _na_attention(proj, bias_tabs, layer, bsz, seq):
    t = bsz * seq
    tq = NA_RB * GRID_W
    nb = seq // tq
    assert nb >= 3 and seq // GRID_W >= NA_KH
    sel = lambda i: jnp.where(i == 0, 0, jnp.where(i == nb - 1, 2, 1))
    prev = lambda b, i: b * nb + jnp.maximum(i - 1, 0)
    cur = lambda b, i: b * nb + i
    nxt = lambda b, i: b * nb + jnp.minimum(i + 1, nb - 1)
    blk = lambda rowf, col: pl.BlockSpec((tq, ATT_W), lambda b, i: (rowf(b, i), col))
    return pl.pallas_call(
        _na_body,
        grid=(bsz, nb),
        in_specs=[blk(cur, 0), blk(prev, 1), blk(cur, 1), blk(nxt, 1),
                  blk(prev, 2), blk(cur, 2), blk(nxt, 2),
                  pl.BlockSpec((None, None, N_HEADS, tq, 3 * tq), lambda b, i: (layer, sel(i), 0, 0, 0))],
        out_specs=pl.BlockSpec((tq, ATT_W), lambda b, i: (cur(b, i), 0)),
        out_shape=jax.ShapeDtypeStruct((t, ATT_W), BF16),
        compiler_params=_cparams(("arbitrary", "arbitrary")),
        name="na_attn",
    )(proj, proj, proj, proj, proj, proj, proj, bias_tabs)


DIL_SUPER = 2048


def _dil_split_body(x_ref, *rest):
    o_refs, scr = rest[:-1], rest[-1]
    for c in range(ATT_W // LANES):
        scr[c] = x_ref[:, c * LANES:(c + 1) * LANES].astype(F32)
    for o_ref in o_refs:
        dil = o_ref.shape[2]
        n = o_ref.shape[3]
        for rho in range(dil):
            for c in range(ATT_W // LANES):
                o_ref[0, 0, rho, :, c * LANES:(c + 1) * LANES] = (
                    scr[c, pl.ds(rho, n, stride=dil), :].astype(BF16))


def _dil_split(proj, bsz, seq, dils):
    t = bsz * seq
    nsb = seq // DIL_SUPER
    out_specs = [pl.BlockSpec((1, 1, dil, DIL_SUPER // dil, ATT_W),
                              lambda i, c: (c, i // nsb, 0, i % nsb, 0)) for dil in dils]
    out_shape = [jax.ShapeDtypeStruct((3, bsz, dil, seq // dil, ATT_W), BF16) for dil in dils]
    return pl.pallas_call(
        _dil_split_body,
        grid=(t // DIL_SUPER, 3),
        in_specs=[pl.BlockSpec((DIL_SUPER, ATT_W), lambda i, c: (i, 3 + c))],
        out_specs=out_specs, out_shape=out_shape,
        scratch_shapes=[pltpu.VMEM((ATT_W // LANES, DIL_SUPER, LANES), F32)],
        compiler_params=_cparams(("arbitrary", "arbitrary")),
        name="dil_split",
    )(proj)


def _dil_body(q_ref, kp_ref, kc_ref, kn_ref, vp_ref, vc_ref, vn_ref, o_ref, l_ref, *, n_sub):
    u = pl.program_id(2)
    nk = 2 * DIL_QBLK
    a = lax.broadcasted_iota(I32, (DIL_QBLK, nk), 0)
    kk = lax.broadcasted_iota(I32, (DIL_QBLK, nk), 1)
    uk = u * DIL_QBLK - DIL_HALF + kk
    ok = (kk >= a) & (kk <= a + 2 * DIL_HALF) & (uk >= 0) & (uk < n_sub)
    lane = lax.broadcasted_iota(I32, (DIL_QBLK, LANES), 1)
    lse_tile = jnp.zeros((DIL_QBLK, LANES), F32)
    for h in range(N_HEADS):
        sl = slice(h * HEAD_DIM, (h + 1) * HEAD_DIM)
        q = q_ref[:, sl]
        k = jnp.concatenate([kp_ref[DIL_HALF:, sl], kc_ref[:, sl], kn_ref[:DIL_HALF, sl]], axis=0)
        v = jnp.concatenate([vp_ref[DIL_HALF:, sl], vc_ref[:, sl], vn_ref[:DIL_HALF, sl]], axis=0)
        s = lax.dot_general(q, k, (((1,), (1,)), ((), ())), preferred_element_type=F32)
        s = jnp.where(ok, s, NEG)
        m = jnp.max(s, axis=-1, keepdims=True)
        p = jnp.exp(s - m)
        l = jnp.sum(p, axis=-1, keepdims=True)
        o = jnp.dot(p.astype(BF16), v, preferred_element_type=F32) / l
        o_ref[:, sl] = o.astype(BF16)
        lse_tile = jnp.where(lane == h, m + jnp.log(l), lse_tile)
    l_ref[...] = lse_tile


def _dil_branch(src, bsz, seq, dil):
    n_sub = seq // dil
    nu = n_sub // DIL_QBLK
    prev = lambda u: jnp.maximum(u - 1, 0)
    cur = lambda u: u
    nxt = lambda u: jnp.minimum(u + 1, nu - 1)
    if dil == 1:
        blk = lambda uf, c: pl.BlockSpec((DIL_QBLK, ATT_W), lambda b, r, u: (b * nu + uf(u), 3 + c))
    else:
        blk = lambda uf, c: pl.BlockSpec((None, None, None, DIL_QBLK, ATT_W),
                                         lambda b, r, u: (c, b, r, uf(u), 0))
    out_blk = lambda w: pl.BlockSpec((None, None, DIL_QBLK, w), lambda b, r, u: (b, r, u, 0))
    return pl.pallas_call(
        functools.partial(_dil_body, n_sub=n_sub),
        grid=(bsz, dil, nu),
        in_specs=[blk(cur, 0), blk(prev, 1), blk(cur, 1), blk(nxt, 1), blk(prev, 2), blk(cur, 2), blk(nxt, 2)],
        out_specs=[out_blk(ATT_W), out_blk(LANES)],
        out_shape=[jax.ShapeDtypeStruct((bsz, dil, n_sub, ATT_W), BF16),
                   jax.ShapeDtypeStruct((bsz, dil, n_sub, LANES), F32)],
        compiler_params=_cparams(("arbitrary", "arbitrary", "arbitrary")),
        name=f"dil_attn_d{dil}",
    )(*([src] * 7))


def _dil_merge_body(*refs, dils):
    nbr = len(dils)
    o_refs, l_refs = refs[:nbr], refs[nbr:2 * nbr]
    e_ref, out_ref = refs[2 * nbr], refs[2 * nbr + 1]
    scr = refs[2 * nbr + 2:]
    outs, lses = [], []
    si = 0
    for n, dil in enumerate(dils):
        if dil == 1:
            outs.append(o_refs[n][0, 0].astype(F32))
            lses.append(l_refs[n][0, 0])
            continue
        so, sl = scr[si], scr[si + 1]
        si += 2
        rows = DIL_SUPER // dil
        for rho in range(dil):
            for c in range(ATT_W // LANES):
                so[c, pl.ds(rho, rows, stride=dil), :] = (
                    o_refs[n][0, rho, :, c * LANES:(c + 1) * LANES].astype(F32))
            sl[pl.ds(rho, rows, stride=dil), :] = l_refs[n][0, rho]
        outs.append(jnp.concatenate([so[c] for c in range(ATT_W // LANES)], axis=1))
        lses.append(sl[...])
    m = functools.reduce(jnp.maximum, lses)
    es = [jnp.exp(l - m) for l in lses]
    tot = functools.reduce(lambda x, y: x + y, es)
    acc = None
    for o, e in zip(outs, es):
        w = jnp.dot((e / tot).astype(BF16), e_ref[...], preferred_element_type=F32)
        acc = w * o if acc is None else acc + w * o
    out_ref[...] = acc.astype(BF16)


def _dil_merge(outs, lses, bsz, seq, dils):
    t = bsz * seq
    nsb = seq // DIL_SUPER
    spec = lambda dil, w: pl.BlockSpec((1, dil, DIL_SUPER // dil, w), lambda i: (i // nsb, 0, i % nsb, 0))
    e8 = np.zeros((LANES, ATT_W), np.float32)
    for h in range(N_HEADS):
        e8[h, h * HEAD_DIM:(h + 1) * HEAD_DIM] = 1.0
    scratch = []
    for dil in dils:
        if dil != 1:
            scratch += [pltpu.VMEM((ATT_W // LANES, DIL_SUPER, LANES), F32), pltpu.VMEM((DIL_SUPER, LANES), F32)]
    return pl.pallas_call(
        functools.partial(_dil_merge_body, dils=dils),
        grid=(t // DIL_SUPER,),
        in_specs=[spec(dil, ATT_W) for dil in dils] + [spec(dil, LANES) for dil in dils]
                 + [pl.BlockSpec((LANES, ATT_W), lambda i: (0, 0))],
        out_specs=pl.BlockSpec((DIL_SUPER, ATT_W), lambda i: (i, 0)),
        out_shape=jax.ShapeDtypeStruct((t, ATT_W), BF16),
        scratch_shapes=scratch,
        compiler_params=_cparams(("arbitrary",)),
        name="dil_merge",
    )(*outs, *lses, jnp.asarray(e8, BF16))


def _dilated_attention(proj, bsz, seq):
    dils = tuple(dil for _, dil in DIL_BRANCHES)
    for win, dil in DIL_BRANCHES:
        assert win // 2 // dil == DIL_HALF and DIL_SUPER % (dil * DIL_QBLK) == 0
    assert seq % DIL_SUPER == 0 and dils[0] == 1
    split = _dil_split(proj, bsz, seq, dils[1:])
    outs, lses = [], []
    for n, dil in enumerate(dils):
        o, l = _dil_branch(proj if dil == 1 else split[n - 1], bsz, seq, dil)
        outs.append(o)
        lses.append(l)
    return _dil_merge(outs, lses, bsz, seq, dils)


CONV_HALO = 16


def _conv_body(xp_ref, xc_ref, xn_ref, w_ref, b_ref, o_ref, *, blocks_per_seq):
    i = pl.program_id(0)
    r = xc_ref.shape[0]
    pos = i % blocks_per_seq
    keep_p = jnp.where(pos == 0, 0.0, 1.0).astype(F32)
    keep_n = jnp.where(pos == blocks_per_seq - 1, 0.0, 1.0).astype(F32)
    ext = jnp.concatenate([xp_ref[...].astype(F32) * keep_p, xc_ref[...].astype(F32),
                           xn_ref[...].astype(F32) * keep_n], axis=0)
    w = w_ref[...]
    half = SSM_CONV // 2
    y = b_ref[...] + jnp.zeros((r, w.shape[1]), F32)
    for k in range(SSM_CONV):
        st = CONV_HALO + k - half
        y = y + ext[st:st + r, :] * w[k:k + 1, :]
    o_ref[...] = _silu(y).astype(BF16)


def _conv_silu(proj, conv_w, conv_b, seq, rblk):
    t = proj.shape[0]
    ncol = SSM_CONV_CH // PROJ_TN
    col0 = (MAIN_COLS - SSM_CONV_CH) // PROJ_TN
    rh = rblk // CONV_HALO
    nh = t // CONV_HALO
    return pl.pallas_call(
        functools.partial(_conv_body, blocks_per_seq=seq // rblk),
        grid=(t // rblk, ncol),
        in_specs=[
            pl.BlockSpec((CONV_HALO, PROJ_TN), lambda i, c: (jnp.maximum(i * rh - 1, 0), col0 + c)),
            pl.BlockSpec((rblk, PROJ_TN), lambda i, c: (i, col0 + c)),
            pl.BlockSpec((CONV_HALO, PROJ_TN), lambda i, c: (jnp.minimum((i + 1) * rh, nh - 1), col0 + c)),
            pl.BlockSpec((SSM_CONV, PROJ_TN), lambda i, c: (0, c)),
            pl.BlockSpec((1, PROJ_TN), lambda i, c: (0, c)),
        ],
        out_specs=pl.BlockSpec((rblk, PROJ_TN), lambda i, c: (i, c)),
        out_shape=jax.ShapeDtypeStruct((t, SSM_CONV_CH), BF16),
        compiler_params=_cparams(("arbitrary", "arbitrary")),
        name="ssd_conv",
    )(proj, proj, proj, conv_w, conv_b)


def _ssd_body(*refs, reverse, final):
    if final:
        (x_ref, b_ref, c_ref, dt_ref, bias_ref, a_ref, e_ref,
         yf_ref, z_ref, dsk_ref, ng_ref, o_ref, state_ref, y_ref) = refs
    else:
        x_ref, b_ref, c_ref, dt_ref, bias_ref, a_ref, e_ref, o_ref, state_ref, y_ref = refs
    q = SSM_CHUNK
    gw = SSM_W // SSM_GROUPS
    hpg = SSM_HEADS // SSM_GROUPS
    off = SSM_HEADS if reverse else 0
    end = 0 if reverse else q - 1

    @pl.when(pl.program_id(1) == 0)
    def _():
        state_ref[...] = jnp.zeros_like(state_ref)

    xr = dt_ref[...] + bias_ref[...]
    dt = jnp.maximum(xr, 0.0) + jnp.log1p(jnp.exp(-jnp.abs(xr)))
    dta = dt * a_ref[...]
    row = lax.broadcasted_iota(I32, (q, q), 0)
    col = lax.broadcasted_iota(I32, (q, q), 1)
    tri = (col >= row) if reverse else (row >= col)
    acs = _dot01_l(jnp.where(tri, 1.0, 0.0).astype(BF16), dta)
    acs_t = acs.T
    wide = _dot01_r(jnp.concatenate([dt, acs], axis=0), e_ref[...])
    dt_x = wide[:q]
    acs_x = wide[q:]
    e_acs = jnp.exp(acs_x)
    xs = x_ref[...].astype(F32)
    xdt = xs * dt_x
    xds = (xdt * jnp.exp(acs_x[end:end + 1, :] - acs_x)).astype(BF16)
    xdt = xdt.astype(BF16)

    for g in range(SSM_GROUPS):
        gs = slice(g * gw, (g + 1) * gw)
        bg = b_ref[:, g * SSM_STATE:(g + 1) * SSM_STATE]
        cg = c_ref[:, g * SSM_STATE:(g + 1) * SSM_STATE]
        cb = lax.dot_general(cg, bg, (((1,), (1,)), ((), ())), preferred_element_type=F32)
        for r in range(hpg):
            hh = g * hpg + r
            hs = slice(hh * HEAD_DIM, (hh + 1) * HEAD_DIM)
            diff = acs[:, off + hh:off + hh + 1] - acs_t[off + hh:off + hh + 1, :]
            decay = jnp.exp(jnp.where(tri, diff, -jnp.inf))
            y_ref[:, hs] = jnp.dot((cb * decay).astype(BF16), xdt[:, hs], preferred_element_type=F32)
        prev = state_ref[g]
        y_off = jnp.dot(cg, prev.astype(BF16), preferred_element_type=F32) * e_acs[:, gs]
        y_ref[:, gs] = y_ref[:, gs] + y_off
        bt = bg.astype(F32).T.astype(BF16)
        st = jnp.dot(bt, xds[:, gs], preferred_element_type=F32)
        state_ref[g] = e_acs[end:end + 1, gs] * prev + st

    if final:
        y = yf_ref[...] + y_ref[...] + dsk_ref[...] * xs
        y = y * _silu(z_ref[...].astype(F32))
        for g in range(SSM_GROUPS):
            gs = slice(g * gw, (g + 1) * gw)
            yg = y[:, gs]
            yg = yg * lax.rsqrt(jnp.mean(yg * yg, axis=-1, keepdims=True) + NORM_EPS)
            o_ref[:, gs] = (yg * ng_ref[:, gs]).astype(o_ref.dtype)
    else:
        o_ref[...] = y_ref[...]


def _ssd_scan(xc, dt_raw, bias_row, a_row, e_mat, bsz, seq, reverse, extra=None):
    t = bsz * seq
    q = SSM_CHUNK
    nc = seq // q
    final = extra is not None
    rowf = (lambda b, c: b * nc + (nc - 1 - c)) if reverse else (lambda b, c: b * nc + c)
    const = lambda w: pl.BlockSpec((1, w), lambda b, c: (0, 0))
    in_specs = [
        pl.BlockSpec((q, SSM_W), lambda b, c: (rowf(b, c), 0)),
        pl.BlockSpec((q, SSM_BC), lambda b, c: (rowf(b, c), SSM_W // SSM_BC)),
        pl.BlockSpec((q, SSM_BC), lambda b, c: (rowf(b, c), SSM_W // SSM_BC + 1)),
        pl.BlockSpec((q, LANES), lambda b, c: (rowf(b, c), 0)),
        const(LANES), const(LANES),
        pl.BlockSpec((LANES, SSM_W), lambda b, c: (0, 0)),
    ]
    args = [xc, xc, xc, dt_raw, bias_row, a_row, e_mat]
    if final:
        y_fwd, proj, dsk_row, ng_row = extra
        zcol = (3 * ATT_W * 2) // SSM_W
        in_specs += [pl.BlockSpec((q, SSM_W), lambda b, c: (rowf(b, c), 0)),
                     pl.BlockSpec((q, SSM_W), lambda b, c: (rowf(b, c), zcol)),
                     const(SSM_W), const(SSM_W)]
        args += [y_fwd, proj, dsk_row, ng_row]
    return pl.pallas_call(
        functools.partial(_ssd_body, reverse=reverse, final=final),
        grid=(bsz, nc),
        in_specs=in_specs,
        out_specs=pl.BlockSpec((q, SSM_W), lambda b, c: (rowf(b, c), 0)),
        out_shape=jax.ShapeDtypeStruct((t, SSM_W), BF16 if final else F32),
        scratch_shapes=[pltpu.VMEM((SSM_GROUPS, SSM_STATE, SSM_W // SSM_GROUPS), F32),
                        pltpu.VMEM((q, SSM_W), F32)],
        compiler_params=_cparams(("arbitrary", "arbitrary")),
        name="ssd_bwd" if reverse else "ssd_fwd",
    )(*args)


def _expand_matrix(reverse):
    e = np.zeros((LANES, SSM_W), np.float32)
    off = SSM_HEADS if reverse else 0
    for h in range(SSM_HEADS):
        e[off + h, h * HEAD_DIM:(h + 1) * HEAD_DIM] = 1.0
    return jnp.asarray(e, BF16)


def _outproj_body(*refs, router):
    if router:
        yna_ref, ydil_ref, yssd_ref, x_ref, w_ref, g_ref, wr_ref, xo_ref, hn_ref, rt_ref = refs
    else:
        yna_ref, ydil_ref, yssd_ref, x_ref, w_ref, g_ref, xo_ref, hn_ref = refs
    acc = x_ref[...]
    acc = acc + jnp.dot(yna_ref[...], w_ref[0:ATT_W, :], preferred_element_type=F32)
    acc = acc + jnp.dot(ydil_ref[...], w_ref[ATT_W:2 * ATT_W, :], preferred_element_type=F32)
    acc = acc + jnp.dot(yssd_ref[...], w_ref[2 * ATT_W:, :], preferred_element_type=F32)
    xo_ref[...] = acc
    ms = jnp.mean(acc * acc, axis=-1, keepdims=True)
    h = acc * lax.rsqrt(ms + NORM_EPS) * g_ref[...]
    hn_ref[...] = h.astype(hn_ref.dtype)
    if router:
        h_hi = h.astype(BF16)
        h_lo = (h - h_hi.astype(F32)).astype(BF16)
        wr = wr_ref[...]
        w_hi = wr.astype(BF16)
        w_lo = (wr - w_hi.astype(F32)).astype(BF16)
        d = lambda a, b: jnp.dot(a, b, preferred_element_type=F32)
        logits = d(h_hi, w_hi) + d(h_hi, w_lo) + d(h_lo, w_hi)
        lane = lax.broadcasted_iota(I32, logits.shape, 1)
        logits = jnp.where(lane < N_EXPERTS, logits, -jnp.inf)
        m1 = jnp.max(logits, axis=-1, keepdims=True)
        i1 = jnp.min(jnp.where(logits == m1, lane, LANES), axis=-1, keepdims=True)
        rest = jnp.where(lane == i1, -jnp.inf, logits)
        m2 = jnp.max(rest, axis=-1, keepdims=True)
        i2 = jnp.min(jnp.where(rest == m2, lane, LANES), axis=-1, keepdims=True)
        e = jnp.exp(m2 - m1)
        g1 = 1.0 / (1.0 + e)
        g2 = e / (1.0 + e)
        rt = jnp.where(lane == 0, i1.astype(F32),
                       jnp.where(lane == 1, i2.astype(F32),
                                 jnp.where(lane == 2, g1, jnp.where(lane == 3, g2, 0.0))))
        rt_ref[...] = rt


def _outproj(y_na, y_dil, y_ssd, x2, w_out, layer, g, w_router, tm, hn_dtype):
    t, d = x2.shape
    router = w_router is not None
    row = lambda w: pl.BlockSpec((tm, w), lambda i: (i, 0))
    in_specs = [row(ATT_W), row(ATT_W), row(SSM_W), row(d),
                pl.BlockSpec((None, d, d), lambda i: (layer, 0, 0)),
                pl.BlockSpec((1, d), lambda i: (0, 0))]
    args = [y_na, y_dil, y_ssd, x2, w_out, g]
    out_specs = [row(d), row(d)]
    out_shape = [jax.ShapeDtypeStruct((t, d), F32), jax.ShapeDtypeStruct((t, d), hn_dtype)]
    if router:
        in_specs.append(pl.BlockSpec((d, LANES), lambda i: (0, 0)))
        args.append(w_router)
        out_specs.append(row(LANES))
        out_shape.append(jax.ShapeDtypeStruct((t, LANES), F32))
    return pl.pallas_call(
        functools.partial(_outproj_body, router=router),
        grid=(t // tm,),
        in_specs=in_specs, out_specs=out_specs, out_shape=out_shape,
        compiler_params=_cparams(("arbitrary",)),
        name="outproj_router" if router else "outproj",
    )(*args)


def _ffn_body(hn_ref, x_ref, wg_ref, wu_ref, wd_ref, o_ref, acc_ref):
    f = pl.program_id(1)

    @pl.when(f == 0)
    def _():
        acc_ref[...] = x_ref[...]

    h = hn_ref[...]
    g = jnp.dot(h, wg_ref[...], preferred_element_type=F32)
    u = jnp.dot(h, wu_ref[...], preferred_element_type=F32)
    a = (_silu(g) * u).astype(BF16)
    acc_ref[...] += jnp.dot(a, wd_ref[...], preferred_element_type=F32)

    @pl.when(f == pl.num_programs(1) - 1)
    def _():
        o_ref[...] = acc_ref[...]


def _ffn(hn, x2, wg, wu, wd, j, tm, tf):
    t, d = x2.shape
    fdim = wg.shape[2]
    return pl.pallas_call(
        _ffn_body,
        grid=(t // tm, fdim // tf),
        in_specs=[pl.BlockSpec((tm, d), lambda i, f: (i, 0)),
                  pl.BlockSpec((tm, d), lambda i, f: (i, 0)),
                  pl.BlockSpec((None, d, tf), lambda i, f: (j, 0, f)),
                  pl.BlockSpec((None, d, tf), lambda i, f: (j, 0, f)),
                  pl.BlockSpec((None, tf, d), lambda i, f: (j, f, 0))],
        out_specs=pl.BlockSpec((tm, d), lambda i, f: (i, 0)),
        out_shape=jax.ShapeDtypeStruct((t, d), F32),
        scratch_shapes=[pltpu.VMEM((tm, d), F32)],
        compiler_params=_cparams(("arbitrary", "arbitrary")),
        name="ffn",
    )(hn, x2, wg, wu, wd)


def _moe_row_copy(h_hbm, xg_ref, sem, tok, slot, r):
    return pltpu.make_async_copy(h_hbm.at[pl.ds(tok, 1), :], xg_ref.at[slot, pl.ds(r, 1), :], sem.at[slot])


def _moe_ffn_body(tok_ref, be_ref, nu_ref, h_hbm, wg_ref, wu_ref, wd_ref, o_ref, xg_ref, xb_ref, acc_ref, sem):
    i = pl.program_id(0)
    f = pl.program_id(1)
    tm = xb_ref.shape[0]
    n_used = nu_ref[0]
    active = i < n_used
    slot = i % 2

    def start_gather(blk, dst_slot):
        def issue(r8, carry):
            for k in range(DMA_UNROLL):
                r = r8 * DMA_UNROLL + k
                _moe_row_copy(h_hbm, xg_ref, sem, tok_ref[blk * tm + r], dst_slot, r).start()
            return carry
        lax.fori_loop(0, tm // DMA_UNROLL, issue, 0)

    @pl.when((f == 0) & (i == 0))
    def _():
        start_gather(0, 0)

    @pl.when((f == 0) & active)
    def _():
        pltpu.make_async_copy(h_hbm.at[pl.ds(0, tm), :], xg_ref.at[slot], sem.at[slot]).wait()
        xb_ref[...] = xg_ref[slot].astype(BF16)
        acc_ref[...] = jnp.zeros_like(acc_ref)

    @pl.when((f == 1) & (i + 1 < n_used))
    def _():
        start_gather(i + 1, 1 - slot)

    @pl.when(active)
    def _():
        h = xb_ref[...]
        g = jnp.dot(h, wg_ref[...], preferred_element_type=F32)
        u = jnp.dot(h, wu_ref[...], preferred_element_type=F32)
        a = (_silu(g) * u).astype(BF16)
        acc_ref[...] += jnp.dot(a, wd_ref[...], preferred_element_type=F32)

    @pl.when(active & (f == pl.num_programs(1) - 1))
    def _():
        o_ref[...] = acc_ref[...]

    @pl.when(jnp.logical_not(active) & (f == 0))
    def _():
        o_ref[...] = jnp.zeros_like(o_ref)


def _moe_ffn(row_tok, blk_expert, n_used, h32, wg, wu, wd, j, tm, tf):
    n_rows = row_tok.shape[0]
    d = h32.shape[1]
    fdim = wg.shape[3]
    nf = fdim // tf
    assert nf >= 2 and tm % DMA_UNROLL == 0
    fidx = lambda i, f, nu: jnp.where(i < nu[0], f, nf - 1)
    grid_spec = pltpu.PrefetchScalarGridSpec(
        num_scalar_prefetch=3,
        grid=(n_rows // tm, nf),
        in_specs=[pl.BlockSpec(memory_space=pl.ANY),
                  pl.BlockSpec((None, None, d, tf), lambda i, f, tok, be, nu: (j, be[i], 0, fidx(i, f, nu))),
                  pl.BlockSpec((None, None, d, tf), lambda i, f, tok, be, nu: (j, be[i], 0, fidx(i, f, nu))),
                  pl.BlockSpec((None, None, tf, d), lambda i, f, tok, be, nu: (j, be[i], fidx(i, f, nu), 0))],
        out_specs=pl.BlockSpec((tm, d), lambda i, f, tok, be, nu: (i, 0)),
        scratch_shapes=[pltpu.VMEM((2, tm, d), F32), pltpu.VMEM((tm, d), BF16),
                        pltpu.VMEM((tm, d), F32), pltpu.SemaphoreType.DMA((2,))],
    )
    return pl.pallas_call(
        _moe_ffn_body,
        grid_spec=grid_spec,
        out_shape=jax.ShapeDtypeStruct((n_rows, d), F32),
        compiler_params=_cparams(("arbitrary", "arbitrary")),
        name="moe_ffn",
    )(row_tok, blk_expert, n_used, h32, wg, wu, wd)


def _combine_row_copy(y_hbm, buf_ref, sem, src, slot, k, r):
    return pltpu.make_async_copy(y_hbm.at[pl.ds(src, 1), :], buf_ref.at[slot, k, pl.ds(r, 1), :], sem.at[slot])


def _combine_body(pos_ref, y_hbm, x_ref, rt_ref, gf_ref, o_ref, buf_ref, sem, *, final):
    i = pl.program_id(0)
    tm = x_ref.shape[0]
    slot = i % 2

    def start_gather(step, dst_slot):
        def issue(r4, carry):
            for u in range(DMA_UNROLL // 2):
                r = r4 * (DMA_UNROLL // 2) + u
                for k in range(2):
                    _combine_row_copy(y_hbm, buf_ref, sem, pos_ref[2 * (step * tm + r) + k], dst_slot, k, r).start()
            return carry
        lax.fori_loop(0, tm // (DMA_UNROLL // 2), issue, 0)

    @pl.when(i == 0)
    def _():
        start_gather(0, 0)

    @pl.when(i + 1 < pl.num_programs(0))
    def _():
        start_gather(i + 1, 1 - slot)

    for k in range(2):
        pltpu.make_async_copy(y_hbm.at[pl.ds(0, tm), :], buf_ref.at[slot, k], sem.at[slot]).wait()

    rt = rt_ref[...]
    out = x_ref[...] + rt[:, 2:3] * buf_ref[slot, 0] + rt[:, 3:4] * buf_ref[slot, 1]
    if final:
        ms = jnp.mean(out * out, axis=-1, keepdims=True)
        out = out * lax.rsqrt(ms + NORM_EPS) * gf_ref[...]
    o_ref[...] = out


def _combine(pos_flat, yb, x2, route, g_final, tm, final):
    t, d = x2.shape
    grid_spec = pltpu.PrefetchScalarGridSpec(
        num_scalar_prefetch=1,
        grid=(t // tm,),
        in_specs=[pl.BlockSpec(memory_space=pl.ANY),
                  pl.BlockSpec((tm, d), lambda i, pos: (i, 0)),
                  pl.BlockSpec((tm, LANES), lambda i, pos: (i, 0)),
                  pl.BlockSpec((1, d), lambda i, pos: (0, 0))],
        out_specs=pl.BlockSpec((tm, d), lambda i, pos: (i, 0)),
        scratch_shapes=[pltpu.VMEM((2, 2, tm, d), F32), pltpu.SemaphoreType.DMA((2,))],
    )
    return pl.pallas_call(
        functools.partial(_combine_body, final=final),
        grid_spec=grid_spec,
        out_shape=jax.ShapeDtypeStruct((t, d), F32),
        compiler_params=_cparams(("arbitrary",)),
        name="moe_combine",
    )(pos_flat, yb, x2, route, g_final)


def _route_plan(route, t, tm):
    e_flat = route[:, 0:2].astype(I32).reshape(-1)
    n_assign = e_flat.shape[0]
    onehot = (e_flat[:, None] == jnp.arange(N_EXPERTS, dtype=I32)[None, :]).astype(I32)
    csum = jnp.cumsum(onehot, axis=0)
    rank = jnp.sum(onehot * (csum - 1), axis=1)
    counts = csum[-1]
    padded = ((counts + tm - 1) // tm) * tm
    pend = jnp.cumsum(padded)
    pstart = pend - padded
    dest = (pstart[e_flat] + rank).astype(I32)
    n_rows = n_assign + N_EXPERTS * tm
    row_tok = jnp.zeros((n_rows,), I32).at[dest].set(jnp.arange(n_assign, dtype=I32) // 2)
    blk_start = jnp.arange(n_rows // tm, dtype=I32) * tm
    blk_expert = jnp.minimum(jnp.searchsorted(pend, blk_start, side='right'), N_EXPERTS - 1).astype(I32)
    n_used = (pend[-1:] // tm).astype(I32)
    return row_tok, blk_expert, n_used, dest


def _rope_tables(seq):
    half = HEAD_DIM // 2
    inv = ROPE_THETA ** (-jnp.arange(0, HEAD_DIM, 2, dtype=F32) / HEAD_DIM)
    ang = jnp.arange(seq, dtype=F32)[:, None] * inv[None, :]
    cos, sin = jnp.cos(ang), jnp.sin(ang)
    reps = LANES // HEAD_DIM
    cos_t = jnp.concatenate([cos, cos] * reps, axis=1)
    sin_t = jnp.concatenate([-sin, sin] * reps, axis=1)
    del half
    return cos_t, sin_t


def _pick_tile(n, pref):
    while n % pref:
        pref //= 2
    return pref


def kernel(x, norm_mix, w_in, na_rpb, conv_w, conv_b, dt_bias, a_log, d_skip, ssm_norm, w_out, norm_ffn,
           ffn_w_gate, ffn_w_up, ffn_w_down, router_w, exp_w_gate, exp_w_up, exp_w_down, norm_final):
    bsz, seq, d = x.shape
    t = bsz * seq
    depth = w_in.shape[0]
    assert w_in.shape[2] == MAIN_COLS + 2 * SSM_HEADS and seq % (GRID_W * NA_RB) == 0
    x2 = x.reshape(t, d).astype(F32)

    tm_proj = _pick_tile(seq, 1024)
    tm_out = _pick_tile(t, 512)
    tm_ffn = _pick_tile(t, 512)
    tf = 512
    tm_moe = _pick_tile(t, 512)
    tm_comb = _pick_tile(t, 256)
    conv_rows = _pick_tile(seq, 1024)

    cos_t, sin_t = _rope_tables(seq)
    w_main = w_in.astype(BF16)
    w_dt = jnp.pad(w_in[:, :, MAIN_COLS:], ((0, 0), (0, 0), (0, LANES - 2 * SSM_HEADS))).astype(BF16)
    w_out_b = w_out.astype(BF16)
    wg_b, wu_b, wd_b = ffn_w_gate.astype(BF16), ffn_w_up.astype(BF16), ffn_w_down.astype(BF16)
    eg_b, eu_b, ed_b = exp_w_gate.astype(BF16), exp_w_up.astype(BF16), exp_w_down.astype(BF16)
    e_fwd, e_bwd = _expand_matrix(False), _expand_matrix(True)
    na_tabs = _na_bias_tables(na_rpb, seq // GRID_W)
    pad_row = lambda v: jnp.pad(v.reshape(1, -1).astype(F32), ((0, 0), (0, LANES - v.size)))

    for layer in range(depth):
        proj, dt_raw = _inproj(x2, norm_mix[layer].reshape(1, d), w_main, w_dt, layer,
                               cos_t, sin_t, seq, tm_proj)
        y_na = _na_attention(proj, na_tabs, layer, bsz, seq)
        y_dil = _dilated_attention(proj, bsz, seq)
        xc = _conv_silu(proj, conv_w[layer].astype(F32), conv_b[layer].reshape(1, -1).astype(F32), seq, conv_rows)
        bias_row = pad_row(dt_bias[layer])
        a_row = pad_row(-jnp.exp(a_log[layer].astype(F32)))
        y_fwd = _ssd_scan(xc, dt_raw, bias_row, a_row, e_fwd, bsz, seq, False)
        dsk_row = jnp.repeat(d_skip[layer].astype(F32), HEAD_DIM).reshape(1, SSM_W)
        y_ssd = _ssd_scan(xc, dt_raw, bias_row, a_row, e_bwd, bsz, seq, True,
                          extra=(y_fwd, proj, dsk_row, ssm_norm[layer].reshape(1, SSM_W).astype(F32)))
        j = layer // 2
        g_ffn = norm_ffn[layer].reshape(1, d).astype(F32)
        if layer % 2 == 0:
            x2, hn = _outproj(y_na, y_dil, y_ssd, x2, w_out_b, layer, g_ffn, None, tm_out, BF16)
            x2 = _ffn(hn, x2, wg_b, wu_b, wd_b, j, tm_ffn, tf)
        else:
            wr = jnp.pad(router_w[j].astype(F32), ((0, 0), (0, LANES - N_EXPERTS)))
            x2, h32, route = _outproj(y_na, y_dil, y_ssd, x2, w_out_b, layer, g_ffn, wr, tm_out, F32)
            row_tok, blk_expert, n_used, dest = _route_plan(route, t, tm_moe)
            yb = _moe_ffn(row_tok, blk_expert, n_used, h32, eg_b, eu_b, ed_b, j, tm_moe, tf)
            final = layer == depth - 1
            x2 = _combine(dest, yb, x2, route, norm_final.reshape(1, d).astype(F32), tm_comb, final)
    if depth % 2 == 1:
        raise NotImplementedError("final norm is fused into the expert combine of the last (odd) layer")
    return x2.reshape(bsz, seq, d).astype(x.dtype)
```

```python
import functools

import numpy as np
import jax
import jax.numpy as jnp
from jax import lax
from jax.experimental import pallas as pl
from jax.experimental.pallas import tpu as pltpu

F32 = jnp.float32
BF16 = jnp.bfloat16
I32 = jnp.int32

GRID_W = 64
HEAD_DIM = 64
NA_KH = 8
NA_KW = 16
N_HEADS = 8
ATT_W = N_HEADS * HEAD_DIM
DIL_BRANCHES = ((128, 1), (512, 4), (2048, 16))
DIL_QBLK = 128
DIL_HALF = 64
SSM_HEADS = 16
SSM_GROUPS = 2
SSM_STATE = 128
SSM_CONV = 5
SSM_CHUNK = 128
SSM_W = SSM_HEADS * HEAD_DIM
SSM_BC = SSM_GROUPS * SSM_STATE
SSM_CONV_CH = SSM_W + 2 * SSM_BC
N_EXPERTS = 8
ROPE_THETA = 10000.0
NORM_EPS = 1e-6
NEG = -1e30

LANES = 128
VMEM_LIMIT = 56 * 1024 * 1024
PROJ_TN = 512
MAIN_COLS = 11 * PROJ_TN
NA_RB = 4
MOE_ISSUE_STEPS = 8
SSM_STEP_CHUNKS = 4
DMA_UNROLL = 8


def _cparams(sem):
    return pltpu.CompilerParams(dimension_semantics=sem, vmem_limit_bytes=VMEM_LIMIT)


def _split3(x):
    hi = x.astype(BF16)
    r = x - hi.astype(F32)
    mid = r.astype(BF16)
    lo = (r - mid.astype(F32)).astype(BF16)
    return hi, mid, lo


def _dot01_l(m01, x):
    hi, mid, lo = _split3(x)
    d = lambda a: jnp.dot(m01, a, preferred_element_type=F32)
    return d(hi) + d(mid) + d(lo)


def _dot01_r(x, m01):
    hi, mid, lo = _split3(x)
    d = lambda a: jnp.dot(a, m01, preferred_element_type=F32)
    return d(hi) + d(mid) + d(lo)


def _silu(x):
    return x / (1.0 + jnp.exp(-x))


def _inproj_body(x_ref, g_ref, w_ref, wdt_ref, cos_ref, sin_ref, o_ref, odt_ref, xn_ref):
    j = pl.program_id(1)

    @pl.when(j == 0)
    def _():
        x = x_ref[...]
        ms = jnp.mean(x * x, axis=-1, keepdims=True)
        xn = (x * lax.rsqrt(ms + NORM_EPS) * g_ref[...]).astype(BF16)
        xn_ref[...] = xn
        odt_ref[...] = jnp.dot(xn, wdt_ref[...], preferred_element_type=F32)

    acc = jnp.dot(xn_ref[...], w_ref[...], preferred_element_type=F32)
    acc = acc * jnp.where((j == 0) | (j == 3), HEAD_DIM ** -0.5, 1.0).astype(F32)
    is_rope = (j == 3) | (j == 4)

    @pl.when(is_rope)
    def _():
        cos = cos_ref[...]
        sin = sin_ref[...]
        lane = lax.broadcasted_iota(I32, cos.shape, 1)
        first = (lane % HEAD_DIM) < (HEAD_DIM // 2)
        for c in range(PROJ_TN // LANES):
            xs = acc[:, c * LANES:(c + 1) * LANES]
            partner = jnp.where(first, pltpu.roll(xs, LANES - HEAD_DIM // 2, 1),
                                pltpu.roll(xs, HEAD_DIM // 2, 1))
            o_ref[:, c * LANES:(c + 1) * LANES] = (xs * cos + partner * sin).astype(BF16)

    @pl.when(jnp.logical_not(is_rope))
    def _():
        o_ref[...] = acc.astype(BF16)


def _inproj(x2, g, w_main, w_dt, layer, cos_t, sin_t, seq, tm):
    t, d = x2.shape
    n_s = seq // tm
    return pl.pallas_call(
        _inproj_body,
        grid=(t // tm, MAIN_COLS // PROJ_TN),
        in_specs=[
            pl.BlockSpec((tm, d), lambda i, j: (i, 0)),
            pl.BlockSpec((1, d), lambda i, j: (0, 0)),
            pl.BlockSpec((None, d, PROJ_TN), lambda i, j: (layer, 0, j)),
            pl.BlockSpec((None, d, LANES), lambda i, j: (layer, 0, 0)),
            pl.BlockSpec((tm, LANES), lambda i, j: (i % n_s, 0)),
            pl.BlockSpec((tm, LANES), lambda i, j: (i % n_s, 0)),
        ],
        out_specs=[
            pl.BlockSpec((tm, PROJ_TN), lambda i, j: (i, j)),
            pl.BlockSpec((tm, LANES), lambda i, j: (i, 0)),
        ],
        out_shape=[jax.ShapeDtypeStruct((t, MAIN_COLS), BF16),
                   jax.ShapeDtypeStruct((t, LANES), F32)],
        scratch_shapes=[pltpu.VMEM((tm, d), BF16)],
        compiler_params=_cparams(("arbitrary", "arbitrary")),
        name="inproj",
    )(x2, g, w_main, w_dt, cos_t, sin_t)


def _na_bias_tables(rpb, rows):
    nb = rows // NA_RB
    c = np.arange(GRID_W)
    cs = np.clip(c - NA_KW // 2, 0, GRID_W - NA_KW)
    ck = np.arange(GRID_W)
    col_ok = (ck[None, :] >= cs[:, None]) & (ck[None, :] < cs[:, None] + NA_KW)
    dc = np.clip(ck[None, :] - c[:, None] + NA_KW - 1, 0, 2 * NA_KW - 2)
    onehot = (dc.reshape(-1)[None, :] == np.arange(2 * NA_KW - 1)[:, None]).astype(np.float32)
    tabs = []
    for i in (0, 1, nb - 1):
        r = NA_RB * i + np.arange(NA_RB)
        rs = np.clip(r - NA_KH // 2, 0, rows - NA_KH)
        absrow = NA_RB * (i - 1) + np.arange(3 * NA_RB)
        row_ok = (absrow[None, :] >= rs[:, None]) & (absrow[None, :] < rs[:, None] + NA_KH)
        dr = np.clip(absrow[None, :] - r[:, None] + NA_KH - 1, 0, 2 * NA_KH - 2)
        by_row = rpb.astype(F32)[:, :, dr, :]
        full = jnp.einsum('lhqkd,dx->lhqkx', by_row, jnp.asarray(onehot), precision=lax.Precision.HIGHEST)
        full = full.reshape(full.shape[:4] + (GRID_W, GRID_W))
        ok = row_ok[:, :, None, None] & col_ok[None, None, :, :]
        full = jnp.where(ok[None, None], full, NEG)
        full = jnp.transpose(full, (0, 1, 2, 4, 3, 5))
        tabs.append(full.reshape(full.shape[:2] + (NA_RB * GRID_W, 3 * NA_RB * GRID_W)))
    return jnp.stack(tabs, axis=1)


def _na_body(q_ref, kp_ref, kc_ref, kn_ref, vp_ref, vc_ref, vn_ref, b_ref, o_ref):
    for h in range(N_HEADS):
        sl = slice(h * HEAD_DIM, (h + 1) * HEAD_DIM)
        q = q_ref[:, sl]
        k = jnp.concatenate([kp_ref[:, sl], kc_ref[:, sl], kn_ref[:, sl]], axis=0)
        v = jnp.concatenate([vp_ref[:, sl], vc_ref[:, sl], vn_ref[:, sl]], axis=0)
        s = lax.dot_general(q, k, (((1,), (1,)), ((), ())), preferred_element_type=F32)
        s = s + b_ref[h]
        m = jnp.max(s, axis=-1, keepdims=True)
        p = jnp.exp(s - m)
        l = jnp.sum(p, axis=-1, keepdims=True)
        o = jnp.dot(p.astype(BF16), v, preferred_element_type=F32) / l
        o_ref[:, sl] = o.astype(BF16)


def _na_attention(proj, bias_tabs, layer, bsz, seq):
    t = bsz * seq
    tq = NA_RB * GRID_W
    nb = seq // tq
    assert nb >= 3 and seq // GRID_W >= NA_KH
    sel = lambda i: jnp.where(i == 0, 0, jnp.where(i == nb - 1, 2, 1))
    prev = lambda b, i: b * nb + jnp.maximum(i - 1, 0)
    cur = lambda b, i: b * nb + i
    nxt = lambda b, i: b * nb + jnp.minimum(i + 1, nb - 1)
    blk = lambda rowf, col: pl.BlockSpec((tq, ATT_W), lambda b, i: (rowf(b, i), col))
    return pl.pallas_call(
        _na_body,
        grid=(bsz, nb),
        in_specs=[blk(cur, 0), blk(prev, 1), blk(cur, 1), blk(nxt, 1),
                  blk(prev, 2), blk(cur, 2), blk(nxt, 2),
                  pl.BlockSpec((None, None, N_HEADS, tq, 3 * tq), lambda b, i: (layer, sel(i), 0, 0, 0))],
        out_specs=pl.BlockSpec((tq, ATT_W), lambda b, i: (cur(b, i), 0)),
        out_shape=jax.ShapeDtypeStruct((t, ATT_W), BF16),
        compiler_params=_cparams(("arbitrary", "arbitrary")),
        name="na_attn",
    )(proj, proj, proj, proj, proj, proj, proj, bias_tabs)


DIL_SUPER = 2048
DIL_STEP_Q = 1024


def _dil_split_body(x_ref, *rest):
    o_refs, scr = rest[:-1], rest[-1]
    for c in range(ATT_W // LANES):
        scr[c] = x_ref[:, c * LANES:(c + 1) * LANES].astype(F32)
    for o_ref in o_refs:
        dil = o_ref.shape[2]
        n = o_ref.shape[3]
        for rho in range(dil):
            for c in range(ATT_W // LANES):
                o_ref[0, 0, rho, :, c * LANES:(c + 1) * LANES] = (
                    scr[c, pl.ds(rho, n, stride=dil), :].astype(BF16))


def _dil_split(proj, bsz, seq, dils):
    t = bsz * seq
    nsb = seq // DIL_SUPER
    out_specs = [pl.BlockSpec((1, 1, dil, DIL_SUPER // dil, ATT_W),
                              lambda i, c: (c, i // nsb, 0, i % nsb, 0)) for dil in dils]
    out_shape = [jax.ShapeDtypeStruct((3, bsz, dil, seq // dil, ATT_W), BF16) for dil in dils]
    return pl.pallas_call(
        _dil_split_body,
        grid=(t // DIL_SUPER, 3),
        in_specs=[pl.BlockSpec((DIL_SUPER, ATT_W), lambda i, c: (i, 3 + c))],
        out_specs=out_specs, out_shape=out_shape,
        scratch_shapes=[pltpu.VMEM((ATT_W // LANES, DIL_SUPER, LANES), F32)],
        compiler_params=_cparams(("arbitrary", "arbitrary")),
        name="dil_split",
    )(proj)


def _dil_body(q_ref, kp_ref, kc_ref, kn_ref, vp_ref, vc_ref, vn_ref, o_ref, l_ref, *, n_sub):
    u = pl.program_id(2)
    qb = q_ref.shape[0]
    nsb = qb // DIL_QBLK
    nk = 2 * DIL_QBLK
    a = lax.broadcasted_iota(I32, (DIL_QBLK, nk), 0)
    kk = lax.broadcasted_iota(I32, (DIL_QBLK, nk), 1)
    band = (kk >= a) & (kk <= a + 2 * DIL_HALF)
    oks = []
    for j in range(nsb):
        uk = u * qb + j * DIL_QBLK - DIL_HALF + kk
        oks.append(band & (uk >= 0) & (uk < n_sub))
    ok = jnp.concatenate(oks, axis=0) if nsb > 1 else oks[0]
    lane = lax.broadcasted_iota(I32, (qb, LANES), 1)
    lse_tile = jnp.zeros((qb, LANES), F32)
    dn = (((1,), (1,)), ((), ()))
    for h in range(N_HEADS):
        sl = slice(h * HEAD_DIM, (h + 1) * HEAD_DIM)
        k = jnp.concatenate([kp_ref[:, sl], kc_ref[:, sl], kn_ref[:, sl]], axis=0)
        v = jnp.concatenate([vp_ref[:, sl], vc_ref[:, sl], vn_ref[:, sl]], axis=0)
        parts = [lax.dot_general(q_ref[j * DIL_QBLK:(j + 1) * DIL_QBLK, sl], k[j * DIL_QBLK:j * DIL_QBLK + nk],
                                 dn, preferred_element_type=F32) for j in range(nsb)]
        s = jnp.concatenate(parts, axis=0) if nsb > 1 else parts[0]
        s = jnp.where(ok, s, NEG)
        m = jnp.max(s, axis=-1, keepdims=True)
        p = jnp.exp(s - m)
        l = jnp.sum(p, axis=-1, keepdims=True)
        pb = p.astype(BF16)
        outs = [jnp.dot(pb[j * DIL_QBLK:(j + 1) * DIL_QBLK], v[j * DIL_QBLK:j * DIL_QBLK + nk],
                        preferred_element_type=F32) for j in range(nsb)]
        o = jnp.concatenate(outs, axis=0) if nsb > 1 else outs[0]
        o_ref[:, sl] = (o / l).astype(BF16)
        lse_tile = jnp.where(lane == h, m + jnp.log(l), lse_tile)
    l_ref[...] = lse_tile


def _dil_branch(src, bsz, seq, dil):
    n_sub = seq // dil
    qb = min(DIL_STEP_Q, n_sub)
    nu = n_sub // qb
    hpb = qb // DIL_HALF
    nhalo = n_sub // DIL_HALF
    prev = lambda u: jnp.maximum(u * hpb - 1, 0)
    nxt = lambda u: jnp.minimum((u + 1) * hpb, nhalo - 1)
    if dil == 1:
        cur = lambda c: pl.BlockSpec((qb, ATT_W), lambda b, r, u: (b * nu + u, 3 + c))
        halo = lambda hf, c: pl.BlockSpec((DIL_HALF, ATT_W), lambda b, r, u: (b * nhalo + hf(u), 3 + c))
    else:
        cur = lambda c: pl.BlockSpec((None, None, None, qb, ATT_W), lambda b, r, u: (c, b, r, u, 0))
        halo = lambda hf, c: pl.BlockSpec((None, None, None, DIL_HALF, ATT_W),
                                          lambda b, r, u: (c, b, r, hf(u), 0))
    out_blk = lambda w: pl.BlockSpec((None, None, qb, w), lambda b, r, u: (b, r, u, 0))
    return pl.pallas_call(
        functools.partial(_dil_body, n_sub=n_sub),
        grid=(bsz, dil, nu),
        in_specs=[cur(0), halo(prev, 1), cur(1), halo(nxt, 1), halo(prev, 2), cur(2), halo(nxt, 2)],
        out_specs=[out_blk(ATT_W), out_blk(LANES)],
        out_shape=[jax.ShapeDtypeStruct((bsz, dil, n_sub, ATT_W), BF16),
                   jax.ShapeDtypeStruct((bsz, dil, n_sub, LANES), F32)],
        compiler_params=_cparams(("arbitrary", "arbitrary", "arbitrary")),
        name=f"dil_attn_d{dil}",
    )(*([src] * 7))


def _dil_merge_body(*refs, dils):
    nbr = len(dils)
    o_refs, l_refs = refs[:nbr], refs[nbr:2 * nbr]
    e_ref, out_ref = refs[2 * nbr], refs[2 * nbr + 1]
    scr = refs[2 * nbr + 2:]
    outs, lses = [], []
    si = 0
    for n, dil in enumerate(dils):
        if dil == 1:
            outs.append(o_refs[n][0, 0].astype(F32))
            lses.append(l_refs[n][0, 0])
            continue
        so, sl = scr[si], scr[si + 1]
        si += 2
        rows = DIL_SUPER // dil
        for rho in range(dil):
            for c in range(ATT_W // LANES):
                so[c, pl.ds(rho, rows, stride=dil), :] = (
                    o_refs[n][0, rho, :, c * LANES:(c + 1) * LANES].astype(F32))
            sl[pl.ds(rho, rows, stride=dil), :] = l_refs[n][0, rho]
        outs.append(jnp.concatenate([so[c] for c in range(ATT_W // LANES)], axis=1))
        lses.append(sl[...])
    m = functools.reduce(jnp.maximum, lses)
    es = [jnp.exp(l - m) for l in lses]
    tot = functools.reduce(lambda x, y: x + y, es)
    acc = None
    for o, e in zip(outs, es):
        w = jnp.dot((e / tot).astype(BF16), e_ref[...], preferred_element_type=F32)
        acc = w * o if acc is None else acc + w * o
    out_ref[...] = acc.astype(BF16)


def _dil_merge(outs, lses, bsz, seq, dils):
    t = bsz * seq
    nsb = seq // DIL_SUPER
    spec = lambda dil, w: pl.BlockSpec((1, dil, DIL_SUPER // dil, w), lambda i: (i // nsb, 0, i % nsb, 0))
    e8 = np.zeros((LANES, ATT_W), np.float32)
    for h in range(N_HEADS):
        e8[h, h * HEAD_DIM:(h + 1) * HEAD_DIM] = 1.0
    scratch = []
    for dil in dils:
        if dil != 1:
            scratch += [pltpu.VMEM((ATT_W // LANES, DIL_SUPER, LANES), F32), pltpu.VMEM((DIL_SUPER, LANES), F32)]
    return pl.pallas_call(
        functools.partial(_dil_merge_body, dils=dils),
        grid=(t // DIL_SUPER,),
        in_specs=[spec(dil, ATT_W) for dil in dils] + [spec(dil, LANES) for dil in dils]
                 + [pl.BlockSpec((LANES, ATT_W), lambda i: (0, 0))],
        out_specs=pl.BlockSpec((DIL_SUPER, ATT_W), lambda i: (i, 0)),
        out_shape=jax.ShapeDtypeStruct((t, ATT_W), BF16),
        scratch_shapes=scratch,
        compiler_params=_cparams(("arbitrary",)),
        name="dil_merge",
    )(*outs, *lses, jnp.asarray(e8, BF16))


def _dilated_attention(proj, bsz, seq):
    dils = tuple(dil for _, dil in DIL_BRANCHES)
    for win, dil in DIL_BRANCHES:
        assert win // 2 // dil == DIL_HALF and DIL_SUPER % (dil * DIL_QBLK) == 0
    assert seq % DIL_SUPER == 0 and dils[0] == 1
    split = _dil_split(proj, bsz, seq, dils[1:])
    outs, lses = [], []
    for n, dil in enumerate(dils):
        o, l = _dil_branch(proj if dil == 1 else split[n - 1], bsz, seq, dil)
        outs.append(o)
        lses.append(l)
    return _dil_merge(outs, lses, bsz, seq, dils)


CONV_HALO = 16


def _conv_body(xp_ref, xc_ref, xn_ref, w_ref, b_ref, o_ref, *, blocks_per_seq):
    i = pl.program_id(0)
    r = xc_ref.shape[0]
    pos = i % blocks_per_seq
    keep_p = jnp.where(pos == 0, 0.0, 1.0).astype(F32)
    keep_n = jnp.where(pos == blocks_per_seq - 1, 0.0, 1.0).astype(F32)
    ext = jnp.concatenate([xp_ref[...].astype(F32) * keep_p, xc_ref[...].astype(F32),
                           xn_ref[...].astype(F32) * keep_n], axis=0)
    w = w_ref[...]
    half = SSM_CONV // 2
    y = b_ref[...] + jnp.zeros((r, w.shape[1]), F32)
    for k in range(SSM_CONV):
        st = CONV_HALO + k - half
        y = y + ext[st:st + r, :] * w[k:k + 1, :]
    o_ref[...] = _silu(y).astype(BF16)


def _conv_silu(proj, conv_w, conv_b, seq, rblk):
    t = proj.shape[0]
    ncol = SSM_CONV_CH // PROJ_TN
    col0 = (MAIN_COLS - SSM_CONV_CH) // PROJ_TN
    rh = rblk // CONV_HALO
    nh = t // CONV_HALO
    return pl.pallas_call(
        functools.partial(_conv_body, blocks_per_seq=seq // rblk),
        grid=(t // rblk, ncol),
        in_specs=[
            pl.BlockSpec((CONV_HALO, PROJ_TN), lambda i, c: (jnp.maximum(i * rh - 1, 0), col0 + c)),
            pl.BlockSpec((rblk, PROJ_TN), lambda i, c: (i, col0 + c)),
            pl.BlockSpec((CONV_HALO, PROJ_TN), lambda i, c: (jnp.minimum((i + 1) * rh, nh - 1), col0 + c)),
            pl.BlockSpec((SSM_CONV, PROJ_TN), lambda i, c: (0, c)),
            pl.BlockSpec((1, PROJ_TN), lambda i, c: (0, c)),
        ],
        out_specs=pl.BlockSpec((rblk, PROJ_TN), lambda i, c: (i, c)),
        out_shape=jax.ShapeDtypeStruct((t, SSM_CONV_CH), BF16),
        compiler_params=_cparams(("arbitrary", "arbitrary")),
        name="ssd_conv",
    )(proj, proj, proj, conv_w, conv_b)


def _ssd_body(*refs, reverse, final):
    if final:
        (x_ref, b_ref, c_ref, dt_ref, bias_ref, a_ref, e_ref,
         yf_ref, z_ref, dsk_ref, ng_ref, o_ref, state_ref, y_ref) = refs
    else:
        x_ref, b_ref, c_ref, dt_ref, bias_ref, a_ref, e_ref, o_ref, state_ref, y_ref = refs
    q = SSM_CHUNK
    gw = SSM_W // SSM_GROUPS
    hpg = SSM_HEADS // SSM_GROUPS
    off = SSM_HEADS if reverse else 0
    end = 0 if reverse else q - 1

    @pl.when(pl.program_id(1) == 0)
    def _():
        state_ref[...] = jnp.zeros_like(state_ref)

    row = lax.broadcasted_iota(I32, (q, q), 0)
    col = lax.broadcasted_iota(I32, (q, q), 1)
    tri = (col >= row) if reverse else (row >= col)
    tri01 = jnp.where(tri, 1.0, 0.0).astype(BF16)
    lane_lo = lax.broadcasted_iota(I32, (q, 2 * HEAD_DIM), 1) < HEAD_DIM
    n_chunks = x_ref.shape[0] // q
    for k in (range(n_chunks - 1, -1, -1) if reverse else range(n_chunks)):
        rows = slice(k * q, (k + 1) * q)
        xr = dt_ref[rows, :] + bias_ref[...]
        dt = jnp.maximum(xr, 0.0) + jnp.log1p(jnp.exp(-jnp.abs(xr)))
        dta = dt * a_ref[...]
        acs = _dot01_l(tri01, dta)
        acs_t = acs.T
        wide = _dot01_r(jnp.concatenate([dt, acs], axis=0), e_ref[...])
        dt_x = wide[:q]
        acs_x = wide[q:]
        e_acs = jnp.exp(acs_x)
        xs = x_ref[rows, :].astype(F32)
        xdt = xs * dt_x
        xds = (xdt * jnp.exp(acs_x[end:end + 1, :] - acs_x)).astype(BF16)
        xdt = xdt.astype(BF16)

        for g in range(SSM_GROUPS):
            gs = slice(g * gw, (g + 1) * gw)
            bg = b_ref[rows, g * SSM_STATE:(g + 1) * SSM_STATE]
            cg = c_ref[rows, g * SSM_STATE:(g + 1) * SSM_STATE]
            cb = lax.dot_general(cg, bg, (((1,), (1,)), ((), ())), preferred_element_type=F32)
            for r in range(0, hpg, 2):
                hh = g * hpg + r
                ms = []
                for h2 in (hh, hh + 1):
                    diff = acs[:, off + h2:off + h2 + 1] - acs_t[off + h2:off + h2 + 1, :]
                    ms.append((cb * jnp.exp(jnp.where(tri, diff, -jnp.inf))).astype(BF16))
                ps = slice(hh * HEAD_DIM, (hh + 2) * HEAD_DIM)
                xp = xdt[:, ps]
                zero = jnp.zeros_like(xp)
                rhs = jnp.concatenate([jnp.where(lane_lo, xp, zero), jnp.where(lane_lo, zero, xp)], axis=0)
                y_ref[rows, ps] = jnp.dot(jnp.concatenate(ms, axis=1), rhs, preferred_element_type=F32)
            prev = state_ref[g]
            y_off = jnp.dot(cg, prev.astype(BF16), preferred_element_type=F32) * e_acs[:, gs]
            y_ref[rows, gs] = y_ref[rows, gs] + y_off
            bt = bg.astype(F32).T.astype(BF16)
            st = jnp.dot(bt, xds[:, gs], preferred_element_type=F32)
            state_ref[g] = e_acs[end:end + 1, gs] * prev + st

        if final:
            y = yf_ref[rows, :] + y_ref[rows, :] + dsk_ref[...] * xs
            y = y * _silu(z_ref[rows, :].astype(F32))
            for g in range(SSM_GROUPS):
                gs = slice(g * gw, (g + 1) * gw)
                yg = y[:, gs]
                yg = yg * lax.rsqrt(jnp.mean(yg * yg, axis=-1, keepdims=True) + NORM_EPS)
                o_ref[rows, gs] = (yg * ng_ref[:, gs]).astype(o_ref.dtype)
        else:
            o_ref[rows, :] = y_ref[rows, :]


def _ssd_scan(xc, dt_raw, bias_row, a_row, e_mat, bsz, seq, reverse, extra=None):
    t = bsz * seq
    q = SSM_CHUNK * SSM_STEP_CHUNKS
    assert seq % q == 0
    nc = seq // q
    final = extra is not None
    rowf =(lambda b, c: b * nc + (nc - 1 - c)) if reverse else (lambda b, c: b * nc + c)
    const = lambda w: pl.BlockSpec((1, w), lambda b, c: (0, 0))
    in_specs = [
        pl.BlockSpec((q, SSM_W), lambda b, c: (rowf(b, c), 0)),
        pl.BlockSpec((q, SSM_BC), lambda b, c: (rowf(b, c), SSM_W // SSM_BC)),
        pl.BlockSpec((q, SSM_BC), lambda b, c: (rowf(b, c), SSM_W // SSM_BC + 1)),
        pl.BlockSpec((q, LANES), lambda b, c: (rowf(b, c), 0)),
        const(LANES), const(LANES),
        pl.BlockSpec((LANES, SSM_W), lambda b, c: (0, 0)),
    ]
    args = [xc, xc, xc, dt_raw, bias_row, a_row, e_mat]
    if final:
        y_fwd, proj, dsk_row, ng_row = extra
        zcol = (3 * ATT_W * 2) // SSM_W
        in_specs += [pl.BlockSpec((q, SSM_W), lambda b, c: (rowf(b, c), 0)),
                     pl.BlockSpec((q, SSM_W), lambda b, c: (rowf(b, c), zcol)),
                     const(SSM_W), const(SSM_W)]
        args += [y_fwd, proj, dsk_row, ng_row]
    return pl.pallas_call(
        functools.partial(_ssd_body, reverse=reverse, final=final),
        grid=(bsz, nc),
        in_specs=in_specs,
        out_specs=pl.BlockSpec((q, SSM_W), lambda b, c: (rowf(b, c), 0)),
        out_shape=jax.ShapeDtypeStruct((t, SSM_W), BF16 if final else F32),
        scratch_shapes=[pltpu.VMEM((SSM_GROUPS, SSM_STATE, SSM_W // SSM_GROUPS), F32),
                        pltpu.VMEM((q, SSM_W), F32)],
        compiler_params=_cparams(("arbitrary", "arbitrary")),
        name="ssd_bwd" if reverse else "ssd_fwd",
    )(*args)


def _expand_matrix(reverse):
    e = np.zeros((LANES, SSM_W), np.float32)
    off = SSM_HEADS if reverse else 0
    for h in range(SSM_HEADS):
        e[off + h, h * HEAD_DIM:(h + 1) * HEAD_DIM] = 1.0
    return jnp.asarray(e, BF16)


def _outproj_body(*refs, router):
    if router:
        yna_ref, ydil_ref, yssd_ref, x_ref, w_ref, g_ref, wr_ref, xo_ref, hn_ref, rt_ref = refs
    else:
        yna_ref, ydil_ref, yssd_ref, x_ref, w_ref, g_ref, xo_ref, hn_ref = refs
    acc = x_ref[...]
    acc = acc + jnp.dot(yna_ref[...], w_ref[0:ATT_W, :], preferred_element_type=F32)
    acc = acc + jnp.dot(ydil_ref[...], w_ref[ATT_W:2 * ATT_W, :], preferred_element_type=F32)
    acc = acc + jnp.dot(yssd_ref[...], w_ref[2 * ATT_W:, :], preferred_element_type=F32)
    xo_ref[...] = acc
    ms = jnp.mean(acc * acc, axis=-1, keepdims=True)
    h = acc * lax.rsqrt(ms + NORM_EPS) * g_ref[...]
    hn_ref[...] = h.astype(hn_ref.dtype)
    if router:
        h_hi = h.astype(BF16)
        h_lo = (h - h_hi.astype(F32)).astype(BF16)
        wr = wr_ref[...]
        w_hi = wr.astype(BF16)
        w_lo = (wr - w_hi.astype(F32)).astype(BF16)
        d = lambda a, b: jnp.dot(a, b, preferred_element_type=F32)
        logits = d(h_hi, w_hi) + d(h_hi, w_lo) + d(h_lo, w_hi)
        lane = lax.broadcasted_iota(I32, logits.shape, 1)
        logits = jnp.where(lane < N_EXPERTS, logits, -jnp.inf)
        m1 = jnp.max(logits, axis=-1, keepdims=True)
        i1 = jnp.min(jnp.where(logits == m1, lane, LANES), axis=-1, keepdims=True)
        rest = jnp.where(lane == i1, -jnp.inf, logits)
        m2 = jnp.max(rest, axis=-1, keepdims=True)
        i2 = jnp.min(jnp.where(rest == m2, lane, LANES), axis=-1, keepdims=True)
        e = jnp.exp(m2 - m1)
        g1 = 1.0 / (1.0 + e)
        g2 = e / (1.0 + e)
        rt = jnp.where(lane == 0, i1.astype(F32),
                       jnp.where(lane == 1, i2.astype(F32),
                                 jnp.where(lane == 2, g1, jnp.where(lane == 3, g2, 0.0))))
        rt_ref[...] = rt


def _outproj(y_na, y_dil, y_ssd, x2, w_out, layer, g, w_router, tm, hn_dtype):
    t, d = x2.shape
    router = w_router is not None
    row = lambda w: pl.BlockSpec((tm, w), lambda i: (i, 0))
    in_specs = [row(ATT_W), row(ATT_W), row(SSM_W), row(d),
                pl.BlockSpec((None, d, d), lambda i: (layer, 0, 0)),
                pl.BlockSpec((1, d), lambda i: (0, 0))]
    args = [y_na, y_dil, y_ssd, x2, w_out, g]
    out_specs = [row(d), row(d)]
    out_shape = [jax.ShapeDtypeStruct((t, d), F32), jax.ShapeDtypeStruct((t, d), hn_dtype)]
    if router:
        in_specs.append(pl.BlockSpec((d, LANES), lambda i: (0, 0)))
        args.append(w_router)
        out_specs.append(row(LANES))
        out_shape.append(jax.ShapeDtypeStruct((t, LANES), F32))
    return pl.pallas_call(
        functools.partial(_outproj_body, router=router),
        grid=(t // tm,),
        in_specs=in_specs, out_specs=out_specs, out_shape=out_shape,
        compiler_params=_cparams(("arbitrary",)),
        name="outproj_router" if router else "outproj",
    )(*args)


def _ffn_body(hn_ref, x_ref, wg_ref, wu_ref, wd_ref, o_ref, acc_ref):
    f = pl.program_id(1)

    @pl.when(f == 0)
    def _():
        acc_ref[...] = x_ref[...]

    h = hn_ref[...]
    g = jnp.dot(h, wg_ref[...], preferred_element_type=F32)
    u = jnp.dot(h, wu_ref[...], preferred_element_type=F32)
    a = (_silu(g) * u).astype(BF16)
    acc_ref[...] += jnp.dot(a, wd_ref[...], preferred_element_type=F32)

    @pl.when(f == pl.num_programs(1) - 1)
    def _():
        o_ref[...] = acc_ref[...]


def _ffn(hn, x2, wg, wu, wd, j, tm, tf):
    t, d = x2.shape
    fdim = wg.shape[2]
    return pl.pallas_call(
        _ffn_body,
        grid=(t // tm, fdim // tf),
        in_specs=[pl.BlockSpec((tm, d), lambda i, f: (i, 0)),
                  pl.BlockSpec((tm, d), lambda i, f: (i, 0)),
                  pl.BlockSpec((None, d, tf), lambda i, f: (j, 0, f)),
                  pl.BlockSpec((None, d, tf), lambda i, f: (j, 0, f)),
                  pl.BlockSpec((None, tf, d), lambda i, f: (j, f, 0))],
        out_specs=pl.BlockSpec((tm, d), lambda i, f: (i, 0)),
        out_shape=jax.ShapeDtypeStruct((t, d), F32),
        scratch_shapes=[pltpu.VMEM((tm, d), F32)],
        compiler_params=_cparams(("arbitrary", "arbitrary")),
        name="ffn",
    )(hn, x2, wg, wu, wd)


def _moe_row_copy(h_hbm, xg_ref, sem, tok, slot, r):
    return pltpu.make_async_copy(h_hbm.at[pl.ds(tok, 1), :], xg_ref.at[slot, pl.ds(r, 1), :], sem.at[slot])


def _moe_ffn_body(tok_ref, be_ref, nu_ref, h_hbm, wg_ref, wu_ref, wd_ref, o_ref, xg_ref, xb_ref, acc_ref, sem):
    i = pl.program_id(0)
    f = pl.program_id(1)
    tm = xb_ref.shape[0]
    n_used = nu_ref[0]
    active = i < n_used
    slot = i % 2

    def start_gather(blk, dst_slot):
        def issue(r8, carry):
            for k in range(DMA_UNROLL):
                r = r8 * DMA_UNROLL + k
                _moe_row_copy(h_hbm, xg_ref, sem, tok_ref[blk * tm + r], dst_slot, r).start()
            return carry
        lax.fori_loop(0, tm // DMA_UNROLL, issue, 0)

    @pl.when((f == 0) & (i == 0))
    def _():
        start_gather(0, 0)

    @pl.when((f == 0) & active)
    def _():
        pltpu.make_async_copy(h_hbm.at[pl.ds(0, tm), :], xg_ref.at[slot], sem.at[slot]).wait()
        xb_ref[...] = xg_ref[slot].astype(BF16)
        acc_ref[...] = jnp.zeros_like(acc_ref)

    def compute():
        h = xb_ref[...]
        g = jnp.dot(h, wg_ref[...], preferred_element_type=F32)
        u = jnp.dot(h, wu_ref[...], preferred_element_type=F32)
        a = (_silu(g) * u).astype(BF16)
        acc_ref[...] += jnp.dot(a, wd_ref[...], preferred_element_type=F32)

    share = tm // MOE_ISSUE_STEPS
    fetch = active & (f >= 1) & (f <= MOE_ISSUE_STEPS) & (i + 1 < n_used)

    @pl.when(fetch)
    def _():
        compute()
        base = (f - 1) * share
        for k in range(share):
            r = base + k
            _moe_row_copy(h_hbm, xg_ref, sem, tok_ref[(i + 1) * tm + r], 1 - slot, r).start()

    @pl.when(active & jnp.logical_not(fetch))
    def _():
        compute()

    @pl.when(active & (f == pl.num_programs(1) - 1))
    def _():
        o_ref[...] = acc_ref[...]

    @pl.when(jnp.logical_not(active) & (f == 0))
    def _():
        o_ref[...] = jnp.zeros_like(o_ref)


def _moe_ffn(row_tok, blk_expert, n_used, h32, wg, wu, wd, j, tm, tf):
    n_rows = row_tok.shape[0]
    d = h32.shape[1]
    fdim = wg.shape[3]
    nf = fdim // tf
    assert nf > MOE_ISSUE_STEPS and tm % MOE_ISSUE_STEPS == 0 and tm % DMA_UNROLL == 0
    fidx = lambda i, f, nu: jnp.where(i < nu[0], f, nf - 1)
    grid_spec = pltpu.PrefetchScalarGridSpec(
        num_scalar_prefetch=3,
        grid=(n_rows // tm, nf),
        in_specs=[pl.BlockSpec(memory_space=pl.ANY),
                  pl.BlockSpec((None, None, d, tf), lambda i, f, tok, be, nu: (j, be[i], 0, fidx(i, f, nu))),
                  pl.BlockSpec((None, None, d, tf), lambda i, f, tok, be, nu: (j, be[i], 0, fidx(i, f, nu))),
                  pl.BlockSpec((None, None, tf, d), lambda i, f, tok, be, nu: (j, be[i], fidx(i, f, nu), 0))],
        out_specs=pl.BlockSpec((tm, d), lambda i, f, tok, be, nu: (i, 0)),
        scratch_shapes=[pltpu.VMEM((2, tm, d), F32), pltpu.VMEM((tm, d), BF16),
                        pltpu.VMEM((tm, d), F32), pltpu.SemaphoreType.DMA((2,))],
    )
    return pl.pallas_call(
        _moe_ffn_body,
        grid_spec=grid_spec,
        out_shape=jax.ShapeDtypeStruct((n_rows, d), F32),
        compiler_params=_cparams(("arbitrary", "arbitrary")),
        name="moe_ffn",
    )(row_tok, blk_expert, n_used, h32, wg, wu, wd)


def _combine_row_copy(y_hbm, buf_ref, sem, src, slot, k, r):
    return pltpu.make_async_copy(y_hbm.at[pl.ds(src, 1), :], buf_ref.at[slot, k, pl.ds(r, 1), :], sem.at[slot])


def _combine_body(pos_ref, y_hbm, x_ref, rt_ref, gf_ref, o_ref, buf_ref, sem, *, final):
    i = pl.program_id(0)
    tm = x_ref.shape[0]
    slot = i % 2

    def start_gather(step, dst_slot):
        def issue(r4, carry):
            for u in range(DMA_UNROLL // 2):
                r = r4 * (DMA_UNROLL // 2) + u
                for k in range(2):
                    _combine_row_copy(y_hbm, buf_ref, sem, pos_ref[2 * (step * tm + r) + k], dst_slot, k, r).start()
            return carry
        lax.fori_loop(0, tm // (DMA_UNROLL // 2), issue, 0)

    @pl.when(i == 0)
    def _():
        start_gather(0, 0)

    @pl.when(i + 1 < pl.num_programs(0))
    def _():
        start_gather(i + 1, 1 - slot)

    for k in range(2):
        pltpu.make_async_copy(y_hbm.at[pl.ds(0, tm), :], buf_ref.at[slot, k], sem.at[slot]).wait()

    rt = rt_ref[...]
    out = x_ref[...] + rt[:, 2:3] * buf_ref[slot, 0] + rt[:, 3:4] * buf_ref[slot, 1]
    if final:
        ms = jnp.mean(out * out, axis=-1, keepdims=True)
        out = out * lax.rsqrt(ms + NORM_EPS) * gf_ref[...]
    o_ref[...] = out


def _combine(pos_flat, yb, x2, route, g_final, tm, final):
    t, d = x2.shape
    grid_spec = pltpu.PrefetchScalarGridSpec(
        num_scalar_prefetch=1,
        grid=(t // tm,),
        in_specs=[pl.BlockSpec(memory_space=pl.ANY),
                  pl.BlockSpec((tm, d), lambda i, pos: (i, 0)),
                  pl.BlockSpec((tm, LANES), lambda i, pos: (i, 0)),
                  pl.BlockSpec((1, d), lambda i, pos: (0, 0))],
        out_specs=pl.BlockSpec((tm, d), lambda i, pos: (i, 0)),
        scratch_shapes=[pltpu.VMEM((2, 2, tm, d), F32), pltpu.SemaphoreType.DMA((2,))],
    )
    return pl.pallas_call(
        functools.partial(_combine_body, final=final),
        grid_spec=grid_spec,
        out_shape=jax.ShapeDtypeStruct((t, d), F32),
        compiler_params=_cparams(("arbitrary",)),
        name="moe_combine",
    )(pos_flat, yb, x2, route, g_final)


def _route_plan(route, t, tm):
    e_flat = route[:, 0:2].astype(I32).reshape(-1)
    n_assign = e_flat.shape[0]
    onehot = (e_flat[:, None] == jnp.arange(N_EXPERTS, dtype=I32)[None, :]).astype(I32)
    csum = jnp.cumsum(onehot, axis=0)
    rank = jnp.sum(onehot * (csum - 1), axis=1)
    counts = csum[-1]
    padded = ((counts + tm - 1) // tm) * tm
    pend = jnp.cumsum(padded)
    pstart = pend - padded
    dest = (pstart[e_flat] + rank).astype(I32)
    n_rows = n_assign + N_EXPERTS * tm
    row_tok = jnp.zeros((n_rows,), I32).at[dest].set(jnp.arange(n_assign, dtype=I32) // 2)
    blk_start = jnp.arange(n_rows // tm, dtype=I32) * tm
    blk_expert = jnp.minimum(jnp.searchsorted(pend, blk_start, side='right'), N_EXPERTS - 1).astype(I32)
    n_used = (pend[-1:] // tm).astype(I32)
    return row_tok, blk_expert, n_used, dest


def _rope_tables(seq):
    half = HEAD_DIM // 2
    inv = ROPE_THETA ** (-jnp.arange(0, HEAD_DIM, 2, dtype=F32) / HEAD_DIM)
    ang = jnp.arange(seq, dtype=F32)[:, None] * inv[None, :]
    cos, sin = jnp.cos(ang), jnp.sin(ang)
    reps = LANES // HEAD_DIM
    cos_t = jnp.concatenate([cos, cos] * reps, axis=1)
    sin_t = jnp.concatenate([-sin, sin] * reps, axis=1)
    del half
    return cos_t, sin_t


def _pick_tile(n, pref):
    while n % pref:
        pref //= 2
    return pref


def kernel(x, norm_mix, w_in, na_rpb, conv_w, conv_b, dt_bias, a_log, d_skip, ssm_norm, w_out, norm_ffn,
           ffn_w_gate, ffn_w_up, ffn_w_down, router_w, exp_w_gate, exp_w_up, exp_w_down, norm_final):
    bsz, seq, d = x.shape
    t = bsz * seq
    depth = w_in.shape[0]
    assert w_in.shape[2] == MAIN_COLS + 2 * SSM_HEADS and seq % (GRID_W * NA_RB) == 0
    x2 = x.reshape(t, d).astype(F32)

    tm_proj = _pick_tile(seq, 1024)
    tm_out = _pick_tile(t, 512)
    tm_ffn = _pick_tile(t, 512)
    tf = 512
    tm_moe = _pick_tile(t, 512)
    tm_comb = _pick_tile(t, 256)
    conv_rows = _pick_tile(seq, 1024)

    cos_t, sin_t = _rope_tables(seq)
    w_main = w_in.astype(BF16)
    w_dt = jnp.pad(w_in[:, :, MAIN_COLS:], ((0, 0), (0, 0), (0, LANES - 2 * SSM_HEADS))).astype(BF16)
    w_out_b = w_out.astype(BF16)
    wg_b, wu_b, wd_b = ffn_w_gate.astype(BF16), ffn_w_up.astype(BF16), ffn_w_down.astype(BF16)
    eg_b, eu_b, ed_b = exp_w_gate.astype(BF16), exp_w_up.astype(BF16), exp_w_down.astype(BF16)
    e_fwd, e_bwd = _expand_matrix(False), _expand_matrix(True)
    na_tabs = _na_bias_tables(na_rpb, seq // GRID_W)
    pad_row = lambda v: jnp.pad(v.reshape(1, -1).astype(F32), ((0, 0), (0, LANES - v.size)))

    for layer in range(depth):
        proj, dt_raw = _inproj(x2, norm_mix[layer].reshape(1, d), w_main, w_dt, layer,
                               cos_t, sin_t, seq, tm_proj)
        y_na = _na_attention(proj, na_tabs, layer, bsz, seq)
        y_dil = _dilated_attention(proj, bsz, seq)
        xc = _conv_silu(proj, conv_w[layer].astype(F32), conv_b[layer].reshape(1, -1).astype(F32), seq, conv_rows)
        bias_row = pad_row(dt_bias[layer])
        a_row = pad_row(-jnp.exp(a_log[layer].astype(F32)))
        y_fwd = _ssd_scan(xc, dt_raw, bias_row, a_row, e_fwd, bsz, seq, False)
        dsk_row = jnp.repeat(d_skip[layer].astype(F32), HEAD_DIM).reshape(1, SSM_W)
        y_ssd = _ssd_scan(xc, dt_raw, bias_row, a_row, e_bwd, bsz, seq, True,
                          extra=(y_fwd, proj, dsk_row, ssm_norm[layer].reshape(1, SSM_W).astype(F32)))
        j = layer // 2
        g_ffn = norm_ffn[layer].reshape(1, d).astype(F32)
        if layer % 2 == 0:
            x2, hn = _outproj(y_na, y_dil, y_ssd, x2, w_out_b, layer, g_ffn, None, tm_out, BF16)
            x2 = _ffn(hn, x2, wg_b, wu_b, wd_b, j, tm_ffn, tf)
        else:
            wr = jnp.pad(router_w[j].astype(F32), ((0, 0), (0, LANES - N_EXPERTS)))
            x2, h32, route = _outproj(y_na, y_dil, y_ssd, x2, w_out_b, layer, g_ffn, wr, tm_out, F32)
            row_tok, blk_expert, n_used, dest = _route_plan(route, t, tm_moe)
            yb = _moe_ffn(row_tok, blk_expert, n_used, h32, eg_b, eu_b, ed_b, j, tm_moe, tf)
            final = layer == depth - 1
            x2 = _combine(dest, yb, x2, route, norm_final.reshape(1, d).astype(F32), tm_comb, final)
    if depth % 2 == 1:
        raise NotImplementedError("final norm is fused into the expert combine of the last (odd) layer")
    return x2.reshape(bsz, seq, d).astype(x.dtype)
```

```python
import functools

import numpy as np
import jax
import jax.numpy as jnp
from jax import lax
from jax.experimental import pallas as pl
from jax.experimental.pallas import tpu as pltpu

F32 = jnp.float32
BF16 = jnp.bfloat16
I32 = jnp.int32

GRID_W = 64
HEAD_DIM = 64
NA_KH = 8
NA_KW = 16
N_HEADS = 8
ATT_W = N_HEADS * HEAD_DIM
DIL_BRANCHES = ((128, 1), (512, 4), (2048, 16))
DIL_QBLK = 128
DIL_HALF = 64
SSM_HEADS = 16
SSM_GROUPS = 2
SSM_STATE = 128
SSM_CONV = 5
SSM_CHUNK = 128
SSM_W = SSM_HEADS * HEAD_DIM
SSM_BC = SSM_GROUPS * SSM_STATE
SSM_CONV_CH = SSM_W + 2 * SSM_BC
N_EXPERTS = 8
ROPE_THETA = 10000.0
NORM_EPS = 1e-6
NEG = -1e30

LANES = 128
VMEM_LIMIT = 56 * 1024 * 1024
PROJ_TN = 512
MAIN_COLS = 11 * PROJ_TN
NA_RB = 4
INPROJ_SPLIT = 4
OUTPROJ_SPLIT = 4
MOE_ISSUE_STEPS = 8
SSM_STEP_CHUNKS = 4
DMA_UNROLL = 8


def _cparams(sem):
    return pltpu.CompilerParams(dimension_semantics=sem, vmem_limit_bytes=VMEM_LIMIT)


def _split3(x):
    hi = x.astype(BF16)
    r = x - hi.astype(F32)
    mid = r.astype(BF16)
    lo = (r - mid.astype(F32)).astype(BF16)
    return hi, mid, lo


def _dot01_l(m01, x):
    hi, mid, lo = _split3(x)
    d = lambda a: jnp.dot(m01, a, preferred_element_type=F32)
    return d(hi) + d(mid) + d(lo)


def _dot01_r(x, m01):
    hi, mid, lo = _split3(x)
    d = lambda a: jnp.dot(a, m01, preferred_element_type=F32)
    return d(hi) + d(mid) + d(lo)


def _silu(x):
    return x / (1.0 + jnp.exp(-x))


def _inproj_body(x_ref, g_ref, w_ref, wdt_ref, cos_ref, sin_ref, o_ref, odt_ref, xn_ref):
    j = pl.program_id(1)
    tm = x_ref.shape[0]
    parts = [slice(p * tm // INPROJ_SPLIT, (p + 1) * tm // INPROJ_SPLIT) for p in range(INPROJ_SPLIT)]

    @pl.when(j == 0)
    def _():
        for rows in parts:
            x = x_ref[rows, :]
            ms = jnp.mean(x * x, axis=-1, keepdims=True)
            xn = (x * lax.rsqrt(ms + NORM_EPS) * g_ref[...]).astype(BF16)
            xn_ref[rows, :] = xn
            odt_ref[rows, :] = jnp.dot(xn, wdt_ref[...], preferred_element_type=F32)
            acc = jnp.dot(xn, w_ref[...], preferred_element_type=F32)
            o_ref[rows, :] = (acc * HEAD_DIM ** -0.5).astype(BF16)

    @pl.when((j == 3) | (j == 4))
    def _():
        scale = jnp.where(j == 3, HEAD_DIM ** -0.5, 1.0).astype(F32)
        lane = lax.broadcasted_iota(I32, (tm // INPROJ_SPLIT, LANES), 1)
        first = (lane % HEAD_DIM) < (HEAD_DIM // 2)
        for rows in parts:
            acc = jnp.dot(xn_ref[rows, :], w_ref[...], preferred_element_type=F32) * scale
            cos = cos_ref[rows, :]
            sin = sin_ref[rows, :]
            for c in range(PROJ_TN // LANES):
                xs = acc[:, c * LANES:(c + 1) * LANES]
                partner = jnp.where(first, pltpu.roll(xs, LANES - HEAD_DIM // 2, 1),
                                    pltpu.roll(xs, HEAD_DIM // 2, 1))
                o_ref[rows, c * LANES:(c + 1) * LANES] = (xs * cos + partner * sin).astype(BF16)

    @pl.when((j != 0) & (j != 3) & (j != 4))
    def _():
        for rows in parts:
            o_ref[rows, :] = jnp.dot(xn_ref[rows, :], w_ref[...], preferred_element_type=F32).astype(BF16)


def _inproj(x2, g, w_main, w_dt, layer, cos_t, sin_t, seq, tm):
    t, d = x2.shape
    n_s = seq // tm
    return pl.pallas_call(
        _inproj_body,
        grid=(t // tm, MAIN_COLS // PROJ_TN),
        in_specs=[
            pl.BlockSpec((tm, d), lambda i, j: (i, 0)),
            pl.BlockSpec((1, d), lambda i, j: (0, 0)),
            pl.BlockSpec((None, d, PROJ_TN), lambda i, j: (layer, 0, j)),
            pl.BlockSpec((None, d, LANES), lambda i, j: (layer, 0, 0)),
            pl.BlockSpec((tm, LANES), lambda i, j: (i % n_s, 0)),
            pl.BlockSpec((tm, LANES), lambda i, j: (i % n_s, 0)),
        ],
        out_specs=[
            pl.BlockSpec((tm, PROJ_TN), lambda i, j: (i, j)),
            pl.BlockSpec((tm, LANES), lambda i, j: (i, 0)),
        ],
        out_shape=[jax.ShapeDtypeStruct((t, MAIN_COLS), BF16),
                   jax.ShapeDtypeStruct((t, LANES), F32)],
        scratch_shapes=[pltpu.VMEM((tm, d), BF16)],
        compiler_params=_cparams(("arbitrary", "arbitrary")),
        name="inproj",
    )(x2, g, w_main, w_dt, cos_t, sin_t)


def _na_bias_tables(rpb, rows):
    nb = rows // NA_RB
    c = np.arange(GRID_W)
    cs = np.clip(c - NA_KW // 2, 0, GRID_W - NA_KW)
    ck = np.arange(GRID_W)
    col_ok = (ck[None, :] >= cs[:, None]) & (ck[None, :] < cs[:, None] + NA_KW)
    dc = np.clip(ck[None, :] - c[:, None] + NA_KW - 1, 0, 2 * NA_KW - 2)
    onehot = (dc.reshape(-1)[None, :] == np.arange(2 * NA_KW - 1)[:, None]).astype(np.float32)
    tabs = []
    for i in (0, 1, nb - 1):
        r = NA_RB * i + np.arange(NA_RB)
        rs = np.clip(r - NA_KH // 2, 0, rows - NA_KH)
        absrow = NA_RB * (i - 1) + np.arange(3 * NA_RB)
        row_ok = (absrow[None, :] >= rs[:, None]) & (absrow[None, :] < rs[:, None] + NA_KH)
        dr = np.clip(absrow[None, :] - r[:, None] + NA_KH - 1, 0, 2 * NA_KH - 2)
        by_row = rpb.astype(F32)[:, :, dr, :]
        full = jnp.einsum('lhqkd,dx->lhqkx', by_row, jnp.asarray(onehot), precision=lax.Precision.HIGHEST)
        full = full.reshape(full.shape[:4] + (GRID_W, GRID_W))
        ok = row_ok[:, :, None, None] & col_ok[None, None, :, :]
        full = jnp.where(ok[None, None], full, NEG)
        full = jnp.transpose(full, (0, 1, 2, 4, 3, 5))
        tabs.append(full.reshape(full.shape[:2] + (NA_RB * GRID_W, 3 * NA_RB * GRID_W)))
    return jnp.stack(tabs, axis=1)


def _na_body(q_ref, kp_ref, kc_ref, kn_ref, vp_ref, vc_ref, vn_ref, b_ref, o_ref):
    for h in range(N_HEADS):
        sl = slice(h * HEAD_DIM, (h + 1) * HEAD_DIM)
        q = q_ref[:, sl]
        k = jnp.concatenate([kp_ref[:, sl], kc_ref[:, sl], kn_ref[:, sl]], axis=0)
        v = jnp.concatenate([vp_ref[:, sl], vc_ref[:, sl], vn_ref[:, sl]], axis=0)
        s = lax.dot_general(q, k, (((1,), (1,)), ((), ())), preferred_element_type=F32)
        s = s + b_ref[h]
        m = jnp.max(s, axis=-1, keepdims=True)
        p = jnp.exp(s - m)
        l = jnp.sum(p, axis=-1, keepdims=True)
        o = jnp.dot(p.astype(BF16), v, preferred_element_type=F32) / l
        o_ref[:, sl] = o.astype(BF16)


def _na_attention(proj, bias_tabs, layer, bsz, seq):
    t = bsz * seq
    tq = NA_RB * GRID_W
    nb = seq // tq
    assert nb >= 3 and seq // GRID_W >= NA_KH
    sel = lambda i: jnp.where(i == 0, 0, jnp.where(i == nb - 1, 2, 1))
    prev = lambda b, i: b * nb + jnp.maximum(i - 1, 0)
    cur = lambda b, i: b * nb + i
    nxt = lambda b, i: b * nb + jnp.minimum(i + 1, nb - 1)
    blk = lambda rowf, col: pl.BlockSpec((tq, ATT_W), lambda b, i: (rowf(b, i), col))
    return pl.pallas_call(
        _na_body,
        grid=(bsz, nb),
        in_specs=[blk(cur, 0), blk(prev, 1), blk(cur, 1), blk(nxt, 1),
                  blk(prev, 2), blk(cur, 2), blk(nxt, 2),
                  pl.BlockSpec((None, None, N_HEADS, tq, 3 * tq), lambda b, i: (layer, sel(i), 0, 0, 0))],
        out_specs=pl.BlockSpec((tq, ATT_W), lambda b, i: (cur(b, i), 0)),
        out_shape=jax.ShapeDtypeStruct((t, ATT_W), BF16),
        compiler_params=_cparams(("arbitrary", "arbitrary")),
        name="na_attn",
    )(proj, proj, proj, proj, proj, proj, proj, bias_tabs)


DIL_SUPER = 2048
DIL_STEP_Q = 1024


def _dil_split_body(x_ref, *rest):
    o_refs, scr = rest[:-1], rest[-1]
    for c in range(ATT_W // LANES):
        scr[c] = x_ref[:, c * LANES:(c + 1) * LANES].astype(F32)
    for o_ref in o_refs:
        dil = o_ref.shape[2]
        n = o_ref.shape[3]
        for rho in range(dil):
            for c in range(ATT_W // LANES):
                o_ref[0, 0, rho, :, c * LANES:(c + 1) * LANES] = (
                    scr[c, pl.ds(rho, n, stride=dil), :].astype(BF16))


def _dil_split(proj, bsz, seq, dils):
    t = bsz * seq
    nsb = seq // DIL_SUPER
    out_specs = [pl.BlockSpec((1, 1, dil, DIL_SUPER // dil, ATT_W),
                              lambda i, c: (c, i // nsb, 0, i % nsb, 0)) for dil in dils]
    out_shape = [jax.ShapeDtypeStruct((3, bsz, dil, seq // dil, ATT_W), BF16) for dil in dils]
    return pl.pallas_call(
        _dil_split_body,
        grid=(t // DIL_SUPER, 3),
        in_specs=[pl.BlockSpec((DIL_SUPER, ATT_W), lambda i, c: (i, 3 + c))],
        out_specs=out_specs, out_shape=out_shape,
        scratch_shapes=[pltpu.VMEM((ATT_W // LANES, DIL_SUPER, LANES), F32)],
        compiler_params=_cparams(("arbitrary", "arbitrary")),
        name="dil_split",
    )(proj)


def _dil_body(q_ref, kp_ref, kc_ref, kn_ref, vp_ref, vc_ref, vn_ref, o_ref, l_ref, *, n_sub):
    u = pl.program_id(2)
    qb = q_ref.shape[0]
    nsb = qb // DIL_QBLK
    nk = 2 * DIL_QBLK
    a = lax.broadcasted_iota(I32, (DIL_QBLK, nk), 0)
    kk = lax.broadcasted_iota(I32, (DIL_QBLK, nk), 1)
    band = (kk >= a) & (kk <= a + 2 * DIL_HALF)
    oks = []
    for j in range(nsb):
        uk = u * qb + j * DIL_QBLK - DIL_HALF + kk
        oks.append(band & (uk >= 0) & (uk < n_sub))
    ok = jnp.concatenate(oks, axis=0) if nsb > 1 else oks[0]
    lane = lax.broadcasted_iota(I32, (qb, LANES), 1)
    lse_tile = jnp.zeros((qb, LANES), F32)
    dn = (((1,), (1,)), ((), ()))
    for h in range(N_HEADS):
        sl = slice(h * HEAD_DIM, (h + 1) * HEAD_DIM)
        k = jnp.concatenate([kp_ref[:, sl], kc_ref[:, sl], kn_ref[:, sl]], axis=0)
        v = jnp.concatenate([vp_ref[:, sl], vc_ref[:, sl], vn_ref[:, sl]], axis=0)
        parts = [lax.dot_general(q_ref[j * DIL_QBLK:(j + 1) * DIL_QBLK, sl], k[j * DIL_QBLK:j * DIL_QBLK + nk],
                                 dn, preferred_element_type=F32) for j in range(nsb)]
        s = jnp.concatenate(parts, axis=0) if nsb > 1 else parts[0]
        s = jnp.where(ok, s, NEG)
        m = jnp.max(s, axis=-1, keepdims=True)
        p = jnp.exp(s - m)
        l = jnp.sum(p, axis=-1, keepdims=True)
        pb = p.astype(BF16)
        outs = [jnp.dot(pb[j * DIL_QBLK:(j + 1) * DIL_QBLK], v[j * DIL_QBLK:j * DIL_QBLK + nk],
                        preferred_element_type=F32) for j in range(nsb)]
        o = jnp.concatenate(outs, axis=0) if nsb > 1 else outs[0]
        o_ref[:, sl] = (o / l).astype(BF16)
        lse_tile = jnp.where(lane == h, m + jnp.log(l), lse_tile)
    l_ref[...] = lse_tile


def _dil_branch(src, bsz, seq, dil):
    n_sub = seq // dil
    qb = min(DIL_STEP_Q, n_sub)
    nu = n_sub // qb
    hpb = qb // DIL_HALF
    nhalo = n_sub // DIL_HALF
    prev = lambda u: jnp.maximum(u * hpb - 1, 0)
    nxt = lambda u: jnp.minimum((u + 1) * hpb, nhalo - 1)
    if dil == 1:
        cur = lambda c: pl.BlockSpec((qb, ATT_W), lambda b, r, u: (b * nu + u, 3 + c))
        halo = lambda hf, c: pl.BlockSpec((DIL_HALF, ATT_W), lambda b, r, u: (b * nhalo + hf(u), 3 + c))
    else:
        cur = lambda c: pl.BlockSpec((None, None, None, qb, ATT_W), lambda b, r, u: (c, b, r, u, 0))
        halo = lambda hf, c: pl.BlockSpec((None, None, None, DIL_HALF, ATT_W),
                                          lambda b, r, u: (c, b, r, hf(u), 0))
    out_blk = lambda w: pl.BlockSpec((None, None, qb, w), lambda b, r, u: (b, r, u, 0))
    return pl.pallas_call(
        functools.partial(_dil_body, n_sub=n_sub),
        grid=(bsz, dil, nu),
        in_specs=[cur(0), halo(prev, 1), cur(1), halo(nxt, 1), halo(prev, 2), cur(2), halo(nxt, 2)],
        out_specs=[out_blk(ATT_W), out_blk(LANES)],
        out_shape=[jax.ShapeDtypeStruct((bsz, dil, n_sub, ATT_W), BF16),
                   jax.ShapeDtypeStruct((bsz, dil, n_sub, LANES), F32)],
        compiler_params=_cparams(("arbitrary", "arbitrary", "arbitrary")),
        name=f"dil_attn_d{dil}",
    )(*([src] * 7))


def _dil_merge_body(*refs, dils):
    nbr = len(dils)
    o_refs, l_refs = refs[:nbr], refs[nbr:2 * nbr]
    e_ref, out_ref = refs[2 * nbr], refs[2 * nbr + 1]
    scr = refs[2 * nbr + 2:]
    outs, lses = [], []
    si = 0
    for n, dil in enumerate(dils):
        if dil == 1:
            outs.append(o_refs[n][0, 0].astype(F32))
            lses.append(l_refs[n][0, 0])
            continue
        so, sl = scr[si], scr[si + 1]
        si += 2
        rows = DIL_SUPER // dil
        for rho in range(dil):
            for c in range(ATT_W // LANES):
                so[c, pl.ds(rho, rows, stride=dil), :] = (
                    o_refs[n][0, rho, :, c * LANES:(c + 1) * LANES].astype(F32))
            sl[pl.ds(rho, rows, stride=dil), :] = l_refs[n][0, rho]
        outs.append(jnp.concatenate([so[c] for c in range(ATT_W // LANES)], axis=1))
        lses.append(sl[...])
    m = functools.reduce(jnp.maximum, lses)
    es = [jnp.exp(l - m) for l in lses]
    tot = functools.reduce(lambda x, y: x + y, es)
    acc = None
    for o, e in zip(outs, es):
        w = jnp.dot((e / tot).astype(BF16), e_ref[...], preferred_element_type=F32)
        acc = w * o if acc is None else acc + w * o
    out_ref[...] = acc.astype(BF16)


def _dil_merge(outs, lses, bsz, seq, dils):
    t = bsz * seq
    nsb = seq // DIL_SUPER
    spec = lambda dil, w: pl.BlockSpec((1, dil, DIL_SUPER // dil, w), lambda i: (i // nsb, 0, i % nsb, 0))
    e8 = np.zeros((LANES, ATT_W), np.float32)
    for h in range(N_HEADS):
        e8[h, h * HEAD_DIM:(h + 1) * HEAD_DIM] = 1.0
    scratch = []
    for dil in dils:
        if dil != 1:
            scratch += [pltpu.VMEM((ATT_W // LANES, DIL_SUPER, LANES), F32), pltpu.VMEM((DIL_SUPER, LANES), F32)]
    return pl.pallas_call(
        functools.partial(_dil_merge_body, dils=dils),
        grid=(t // DIL_SUPER,),
        in_specs=[spec(dil, ATT_W) for dil in dils] + [spec(dil, LANES) for dil in dils]
                 + [pl.BlockSpec((LANES, ATT_W), lambda i: (0, 0))],
        out_specs=pl.BlockSpec((DIL_SUPER, ATT_W), lambda i: (i, 0)),
        out_shape=jax.ShapeDtypeStruct((t, ATT_W), BF16),
        scratch_shapes=scratch,
        compiler_params=_cparams(("arbitrary",)),
        name="dil_merge",
    )(*outs, *lses, jnp.asarray(e8, BF16))


def _dilated_attention(proj, bsz, seq):
    dils = tuple(dil for _, dil in DIL_BRANCHES)
    for win, dil in DIL_BRANCHES:
        assert win // 2 // dil == DIL_HALF and DIL_SUPER % (dil * DIL_QBLK) == 0
    assert seq % DIL_SUPER == 0 and dils[0] == 1
    split = _dil_split(proj, bsz, seq, dils[1:])
    outs, lses = [], []
    for n, dil in enumerate(dils):
        o, l = _dil_branch(proj if dil == 1 else split[n - 1], bsz, seq, dil)
        outs.append(o)
        lses.append(l)
    return _dil_merge(outs, lses, bsz, seq, dils)


CONV_HALO = 16


def _conv_body(xp_ref, xc_ref, xn_ref, w_ref, b_ref, o_ref, *, blocks_per_seq):
    i = pl.program_id(0)
    r = xc_ref.shape[0]
    pos = i % blocks_per_seq
    keep_p = jnp.where(pos == 0, 0.0, 1.0).astype(F32)
    keep_n = jnp.where(pos == blocks_per_seq - 1, 0.0, 1.0).astype(F32)
    ext = jnp.concatenate([xp_ref[...].astype(F32) * keep_p, xc_ref[...].astype(F32),
                           xn_ref[...].astype(F32) * keep_n], axis=0)
    w = w_ref[...]
    half = SSM_CONV // 2
    y = b_ref[...] + jnp.zeros((r, w.shape[1]), F32)
    for k in range(SSM_CONV):
        st = CONV_HALO + k - half
        y = y + ext[st:st + r, :] * w[k:k + 1, :]
    o_ref[...] = _silu(y).astype(BF16)


def _conv_silu(proj, conv_w, conv_b, seq, rblk):
    t = proj.shape[0]
    ncol = SSM_CONV_CH // PROJ_TN
    col0 = (MAIN_COLS - SSM_CONV_CH) // PROJ_TN
    rh = rblk // CONV_HALO
    nh = t // CONV_HALO
    return pl.pallas_call(
        functools.partial(_conv_body, blocks_per_seq=seq // rblk),
        grid=(t // rblk, ncol),
        in_specs=[
            pl.BlockSpec((CONV_HALO, PROJ_TN), lambda i, c: (jnp.maximum(i * rh - 1, 0), col0 + c)),
            pl.BlockSpec((rblk, PROJ_TN), lambda i, c: (i, col0 + c)),
            pl.BlockSpec((CONV_HALO, PROJ_TN), lambda i, c: (jnp.minimum((i + 1) * rh, nh - 1), col0 + c)),
            pl.BlockSpec((SSM_CONV, PROJ_TN), lambda i, c: (0, c)),
            pl.BlockSpec((1, PROJ_TN), lambda i, c: (0, c)),
        ],
        out_specs=pl.BlockSpec((rblk, PROJ_TN), lambda i, c: (i, c)),
        out_shape=jax.ShapeDtypeStruct((t, SSM_CONV_CH), BF16),
        compiler_params=_cparams(("arbitrary", "arbitrary")),
        name="ssd_conv",
    )(proj, proj, proj, conv_w, conv_b)


def _ssd_body(*refs, reverse, final):
    if final:
        (x_ref, b_ref, c_ref, dt_ref, bias_ref, a_ref, e_ref,
         yf_ref, z_ref, dsk_ref, ng_ref, o_ref, state_ref, y_ref) = refs
    else:
        x_ref, b_ref, c_ref, dt_ref, bias_ref, a_ref, e_ref, o_ref, state_ref, y_ref = refs
    q = SSM_CHUNK
    gw = SSM_W // SSM_GROUPS
    hpg = SSM_HEADS // SSM_GROUPS
    off = SSM_HEADS if reverse else 0
    end = 0 if reverse else q - 1

    @pl.when(pl.program_id(1) == 0)
    def _():
        state_ref[...] = jnp.zeros_like(state_ref)

    row = lax.broadcasted_iota(I32, (q, q), 0)
    col = lax.broadcasted_iota(I32, (q, q), 1)
    tri = (col >= row) if reverse else (row >= col)
    tri01 = jnp.where(tri, 1.0, 0.0).astype(BF16)
    lane_lo = lax.broadcasted_iota(I32, (q, 2 * HEAD_DIM), 1) < HEAD_DIM
    n_chunks = x_ref.shape[0] // q
    for k in (range(n_chunks - 1, -1, -1) if reverse else range(n_chunks)):
        rows = slice(k * q, (k + 1) * q)
        xr = dt_ref[rows, :] + bias_ref[...]
        dt = jnp.maximum(xr, 0.0) + jnp.log(1.0 + jnp.exp(-jnp.abs(xr)))
        dta = dt * a_ref[...]
        acs = _dot01_l(tri01, dta)
        acs_t = acs.T
        acs_end = acs[end:end + 1, :]
        wide = jnp.dot(jnp.concatenate([dt, jnp.exp(acs), jnp.exp(acs_end - acs)], axis=0).astype(BF16),
                       e_ref[...], preferred_element_type=F32)
        dt_x, e_acs, ds_x = wide[:q], wide[q:2 * q], wide[2 * q:]
        cd_x = _dot01_r(jnp.broadcast_to(jnp.exp(acs_end), (8, LANES)), e_ref[...])[0:1, :]
        xs = x_ref[rows, :].astype(F32)
        xdt = xs * dt_x
        xds = (xdt * ds_x).astype(BF16)
        xdt = xdt.astype(BF16)

        for g in range(SSM_GROUPS):
            gs = slice(g * gw, (g + 1) * gw)
            bg = b_ref[rows, g * SSM_STATE:(g + 1) * SSM_STATE]
            cg = c_ref[rows, g * SSM_STATE:(g + 1) * SSM_STATE]
            cb = lax.dot_general(cg, bg, (((1,), (1,)), ((), ())), preferred_element_type=F32)
            for r in range(0, hpg, 2):
                hh = g * hpg + r
                ms = []
                for h2 in (hh, hh + 1):
                    diff = acs[:, off + h2:off + h2 + 1] - acs_t[off + h2:off + h2 + 1, :]
                    ms.append((cb * jnp.exp(jnp.where(tri, diff, -jnp.inf))).astype(BF16))
                ps = slice(hh * HEAD_DIM, (hh + 2) * HEAD_DIM)
                xp = xdt[:, ps]
                zero = jnp.zeros_like(xp)
                rhs = jnp.concatenate([jnp.where(lane_lo, xp, zero), jnp.where(lane_lo, zero, xp)], axis=0)
                y_ref[rows, ps] = jnp.dot(jnp.concatenate(ms, axis=1), rhs, preferred_element_type=F32)
            prev = state_ref[g]
            y_off = jnp.dot(cg, prev.astype(BF16), preferred_element_type=F32) * e_acs[:, gs]
            y_ref[rows, gs] = y_ref[rows, gs] + y_off
            bt = bg.astype(F32).T.astype(BF16)
            st = jnp.dot(bt, xds[:, gs], preferred_element_type=F32)
            state_ref[g] = cd_x[:, gs] * prev + st

        if final:
            y = yf_ref[rows, :] + y_ref[rows, :] + dsk_ref[...] * xs
            y = y * _silu(z_ref[rows, :].astype(F32))
            for g in range(SSM_GROUPS):
                gs = slice(g * gw, (g + 1) * gw)
                yg = y[:, gs]
                yg = yg * lax.rsqrt(jnp.mean(yg * yg, axis=-1, keepdims=True) + NORM_EPS)
                o_ref[rows, gs] = (yg * ng_ref[:, gs]).astype(o_ref.dtype)
        else:
            o_ref[rows, :] = y_ref[rows, :]


def _ssd_scan(xc, dt_raw, bias_row, a_row, e_mat, bsz, seq, reverse, extra=None):
    t = bsz * seq
    q = SSM_CHUNK * SSM_STEP_CHUNKS
    assert seq % q == 0
    nc = seq // q
    final = extra is not None
    rowf =(lambda b, c: b * nc + (nc - 1 - c)) if reverse else (lambda b, c: b * nc + c)
    const = lambda w: pl.BlockSpec((1, w), lambda b, c: (0, 0))
    in_specs = [
        pl.BlockSpec((q, SSM_W), lambda b, c: (rowf(b, c), 0)),
        pl.BlockSpec((q, SSM_BC), lambda b, c: (rowf(b, c), SSM_W // SSM_BC)),
        pl.BlockSpec((q, SSM_BC), lambda b, c: (rowf(b, c), SSM_W // SSM_BC + 1)),
        pl.BlockSpec((q, LANES), lambda b, c: (rowf(b, c), 0)),
        const(LANES), const(LANES),
        pl.BlockSpec((LANES, SSM_W), lambda b, c: (0, 0)),
    ]
    args = [xc, xc, xc, dt_raw, bias_row, a_row, e_mat]
    if final:
        y_fwd, proj, dsk_row, ng_row = extra
        zcol = (3 * ATT_W * 2) // SSM_W
        in_specs += [pl.BlockSpec((q, SSM_W), lambda b, c: (rowf(b, c), 0)),
                     pl.BlockSpec((q, SSM_W), lambda b, c: (rowf(b, c), zcol)),
                     const(SSM_W), const(SSM_W)]
        args += [y_fwd, proj, dsk_row, ng_row]
    return pl.pallas_call(
        functools.partial(_ssd_body, reverse=reverse, final=final),
        grid=(bsz, nc),
        in_specs=in_specs,
        out_specs=pl.BlockSpec((q, SSM_W), lambda b, c: (rowf(b, c), 0)),
        out_shape=jax.ShapeDtypeStruct((t, SSM_W), BF16 if final else F32),
        scratch_shapes=[pltpu.VMEM((SSM_GROUPS, SSM_STATE, SSM_W // SSM_GROUPS), F32),
                        pltpu.VMEM((q, SSM_W), F32)],
        compiler_params=_cparams(("arbitrary", "arbitrary")),
        name="ssd_bwd" if reverse else "ssd_fwd",
    )(*args)


def _expand_matrix(reverse):
    e = np.zeros((LANES, SSM_W), np.float32)
    off = SSM_HEADS if reverse else 0
    for h in range(SSM_HEADS):
        e[off + h, h * HEAD_DIM:(h + 1) * HEAD_DIM] = 1.0
    return jnp.asarray(e, BF16)


def _outproj_body(*refs, router):
    if router:
        yna_ref, ydil_ref, yssd_ref, x_ref, w_ref, g_ref, wr_ref, xo_ref, hn_ref, rt_ref = refs
    else:
        yna_ref, ydil_ref, yssd_ref, x_ref, w_ref, g_ref, xo_ref, hn_ref = refs
    tm = x_ref.shape[0]
    for part in range(OUTPROJ_SPLIT):
        rows = slice(part * tm // OUTPROJ_SPLIT, (part + 1) * tm // OUTPROJ_SPLIT)
        acc = x_ref[rows, :]
        acc = acc + jnp.dot(yna_ref[rows, :], w_ref[0:ATT_W, :], preferred_element_type=F32)
        acc = acc + jnp.dot(ydil_ref[rows, :], w_ref[ATT_W:2 * ATT_W, :], preferred_element_type=F32)
        acc = acc + jnp.dot(yssd_ref[rows, :], w_ref[2 * ATT_W:, :], preferred_element_type=F32)
        xo_ref[rows, :] = acc
        ms = jnp.mean(acc * acc, axis=-1, keepdims=True)
        h = acc * lax.rsqrt(ms + NORM_EPS) * g_ref[...]
        hn_ref[rows, :] = h.astype(hn_ref.dtype)
        if router:
            h_hi = h.astype(BF16)
            h_lo = (h - h_hi.astype(F32)).astype(BF16)
            wr = wr_ref[...]
            w_hi = wr.astype(BF16)
            w_lo = (wr - w_hi.astype(F32)).astype(BF16)
            d = lambda a, b: jnp.dot(a, b, preferred_element_type=F32)
            logits = d(h_hi, w_hi) + d(h_hi, w_lo) + d(h_lo, w_hi)
            lane = lax.broadcasted_iota(I32, logits.shape, 1)
            logits = jnp.where(lane < N_EXPERTS, logits, -jnp.inf)
            m1 = jnp.max(logits, axis=-1, keepdims=True)
            i1 = jnp.min(jnp.where(logits == m1, lane, LANES), axis=-1, keepdims=True)
            rest = jnp.where(lane == i1, -jnp.inf, logits)
            m2 = jnp.max(rest, axis=-1, keepdims=True)
            i2 = jnp.min(jnp.where(rest == m2, lane, LANES), axis=-1, keepdims=True)
            e = jnp.exp(m2 - m1)
            g1 = 1.0 / (1.0 + e)
            g2 = e / (1.0 + e)
            rt = jnp.where(lane == 0, i1.astype(F32),
                           jnp.where(lane == 1, i2.astype(F32),
                                     jnp.where(lane == 2, g1, jnp.where(lane == 3, g2, 0.0))))
            rt_ref[rows, :] = rt


def _outproj(y_na, y_dil, y_ssd, x2, w_out, layer, g, w_router, tm, hn_dtype):
    t, d = x2.shape
    router = w_router is not None
    row = lambda w: pl.BlockSpec((tm, w), lambda i: (i, 0))
    in_specs = [row(ATT_W), row(ATT_W), row(SSM_W), row(d),
                pl.BlockSpec((None, d, d), lambda i: (layer, 0, 0)),
                pl.BlockSpec((1, d), lambda i: (0, 0))]
    args = [y_na, y_dil, y_ssd, x2, w_out, g]
    out_specs = [row(d), row(d)]
    out_shape = [jax.ShapeDtypeStruct((t, d), F32), jax.ShapeDtypeStruct((t, d), hn_dtype)]
    if router:
        in_specs.append(pl.BlockSpec((d, LANES), lambda i: (0, 0)))
        args.append(w_router)
        out_specs.append(row(LANES))
        out_shape.append(jax.ShapeDtypeStruct((t, LANES), F32))
    return pl.pallas_call(
        functools.partial(_outproj_body, router=router),
        grid=(t // tm,),
        in_specs=in_specs, out_specs=out_specs, out_shape=out_shape,
        compiler_params=_cparams(("arbitrary",)),
        name="outproj_router" if router else "outproj",
    )(*args)


def _ffn_body(hn_ref, x_ref, wg_ref, wu_ref, wd_ref, o_ref, acc_ref):
    f = pl.program_id(1)

    @pl.when(f == 0)
    def _():
        acc_ref[...] = x_ref[...]

    h = hn_ref[...]
    g = jnp.dot(h, wg_ref[...], preferred_element_type=F32)
    u = jnp.dot(h, wu_ref[...], preferred_element_type=F32)
    a = (_silu(g) * u).astype(BF16)
    acc_ref[...] += jnp.dot(a, wd_ref[...], preferred_element_type=F32)

    @pl.when(f == pl.num_programs(1) - 1)
    def _():
        o_ref[...] = acc_ref[...]


def _ffn(hn, x2, wg, wu, wd, j, tm, tf):
    t, d = x2.shape
    fdim = wg.shape[2]
    return pl.pallas_call(
        _ffn_body,
        grid=(t // tm, fdim // tf),
        in_specs=[pl.BlockSpec((tm, d), lambda i, f: (i, 0)),
                  pl.BlockSpec((tm, d), lambda i, f: (i, 0)),
                  pl.BlockSpec((None, d, tf), lambda i, f: (j, 0, f)),
                  pl.BlockSpec((None, d, tf), lambda i, f: (j, 0, f)),
                  pl.BlockSpec((None, tf, d), lambda i, f: (j, f, 0))],
        out_specs=pl.BlockSpec((tm, d), lambda i, f: (i, 0)),
        out_shape=jax.ShapeDtypeStruct((t, d), F32),
        scratch_shapes=[pltpu.VMEM((tm, d), F32)],
        compiler_params=_cparams(("arbitrary", "arbitrary")),
        name="ffn",
    )(hn, x2, wg, wu, wd)


def _moe_row_copy(h_hbm, xg_ref, sem, tok, slot, r):
    return pltpu.make_async_copy(h_hbm.at[pl.ds(tok, 1), :], xg_ref.at[slot, pl.ds(r, 1), :], sem.at[slot])


def _moe_ffn_body(tok_ref, be_ref, nu_ref, h_hbm, wg_ref, wu_ref, wd_ref, o_ref, xg_ref, xb_ref, acc_ref, sem):
    i = pl.program_id(0)
    f = pl.program_id(1)
    tm = xb_ref.shape[0]
    n_used = nu_ref[0]
    active = i < n_used
    slot = i % 2

    def start_gather(blk, dst_slot):
        def issue(r8, carry):
            for k in range(DMA_UNROLL):
                r = r8 * DMA_UNROLL + k
                _moe_row_copy(h_hbm, xg_ref, sem, tok_ref[blk * tm + r], dst_slot, r).start()
            return carry
        lax.fori_loop(0, tm // DMA_UNROLL, issue, 0)

    @pl.when((f == 0) & (i == 0))
    def _():
        start_gather(0, 0)

    @pl.when((f == 0) & active)
    def _():
        pltpu.make_async_copy(h_hbm.at[pl.ds(0, tm), :], xg_ref.at[slot], sem.at[slot]).wait()
        xb_ref[...] = xg_ref[slot].astype(BF16)
        acc_ref[...] = jnp.zeros_like(acc_ref)

    def compute():
        h = xb_ref[...]
        g = jnp.dot(h, wg_ref[...], preferred_element_type=F32)
        u = jnp.dot(h, wu_ref[...], preferred_element_type=F32)
        a = (_silu(g) * u).astype(BF16)
        acc_ref[...] += jnp.dot(a, wd_ref[...], preferred_element_type=F32)

    share = tm // MOE_ISSUE_STEPS
    fetch = active & (f >= 1) & (f <= MOE_ISSUE_STEPS) & (i + 1 < n_used)

    @pl.when(fetch)
    def _():
        compute()
        base = (f - 1) * share
        for k in range(share):
            r = base + k
            _moe_row_copy(h_hbm, xg_ref, sem, tok_ref[(i + 1) * tm + r], 1 - slot, r).start()

    @pl.when(active & jnp.logical_not(fetch))
    def _():
        compute()

    @pl.when(active & (f == pl.num_programs(1) - 1))
    def _():
        o_ref[...] = acc_ref[...]

    @pl.when(jnp.logical_not(active) & (f == 0))
    def _():
        o_ref[...] = jnp.zeros_like(o_ref)


def _moe_ffn(row_tok, blk_expert, n_used, h32, wg, wu, wd, j, tm, tf):
    n_rows = row_tok.shape[0]
    d = h32.shape[1]
    fdim = wg.shape[3]
    nf = fdim // tf
    assert nf > MOE_ISSUE_STEPS and tm % MOE_ISSUE_STEPS == 0 and tm % DMA_UNROLL == 0
    fidx = lambda i, f, nu: jnp.where(i < nu[0], f, nf - 1)
    grid_spec = pltpu.PrefetchScalarGridSpec(
        num_scalar_prefetch=3,
        grid=(n_rows // tm, nf),
        in_specs=[pl.BlockSpec(memory_space=pl.ANY),
                  pl.BlockSpec((None, None, d, tf), lambda i, f, tok, be, nu: (j, be[i], 0, fidx(i, f, nu))),
                  pl.BlockSpec((None, None, d, tf), lambda i, f, tok, be, nu: (j, be[i], 0, fidx(i, f, nu))),
                  pl.BlockSpec((None, None, tf, d), lambda i, f, tok, be, nu: (j, be[i], fidx(i, f, nu), 0))],
        out_specs=pl.BlockSpec((tm, d), lambda i, f, tok, be, nu: (i, 0)),
        scratch_shapes=[pltpu.VMEM((2, tm, d), F32), pltpu.VMEM((tm, d), BF16),
                        pltpu.VMEM((tm, d), F32), pltpu.SemaphoreType.DMA((2,))],
    )
    return pl.pallas_call(
        _moe_ffn_body,
        grid_spec=grid_spec,
        out_shape=jax.ShapeDtypeStruct((n_rows, d), F32),
        compiler_params=_cparams(("arbitrary", "arbitrary")),
        name="moe_ffn",
    )(row_tok, blk_expert, n_used, h32, wg, wu, wd)


def _combine_row_copy(y_hbm, buf_ref, sem, src, slot, k, r):
    return pltpu.make_async_copy(y_hbm.at[pl.ds(src, 1), :], buf_ref.at[slot, k, pl.ds(r, 1), :], sem.at[slot])


def _combine_body(pos_ref, y_hbm, x_ref, rt_ref, gf_ref, o_ref, buf_ref, sem, *, final):
    i = pl.program_id(0)
    tm = x_ref.shape[0]
    slot = i % 2

    def start_gather(step, dst_slot):
        def issue(r4, carry):
            for u in range(DMA_UNROLL // 2):
                r = r4 * (DMA_UNROLL // 2) + u
                for k in range(2):
                    _combine_row_copy(y_hbm, buf_ref, sem, pos_ref[2 * (step * tm + r) + k], dst_slot, k, r).start()
            return carry
        lax.fori_loop(0, tm // (DMA_UNROLL // 2), issue, 0)

    @pl.when(i == 0)
    def _():
        start_gather(0, 0)

    @pl.when(i + 1 < pl.num_programs(0))
    def _():
        start_gather(i + 1, 1 - slot)

    for k in range(2):
        pltpu.make_async_copy(y_hbm.at[pl.ds(0, tm), :], buf_ref.at[slot, k], sem.at[slot]).wait()

    rt = rt_ref[...]
    out = x_ref[...] + rt[:, 2:3] * buf_ref[slot, 0] + rt[:, 3:4] * buf_ref[slot, 1]
    if final:
        ms = jnp.mean(out * out, axis=-1, keepdims=True)
        out = out * lax.rsqrt(ms + NORM_EPS) * gf_ref[...]
    o_ref[...] = out


def _combine(pos_flat, yb, x2, route, g_final, tm, final):
    t, d = x2.shape
    grid_spec = pltpu.PrefetchScalarGridSpec(
        num_scalar_prefetch=1,
        grid=(t // tm,),
        in_specs=[pl.BlockSpec(memory_space=pl.ANY),
                  pl.BlockSpec((tm, d), lambda i, pos: (i, 0)),
                  pl.BlockSpec((tm, LANES), lambda i, pos: (i, 0)),
                  pl.BlockSpec((1, d), lambda i, pos: (0, 0))],
        out_specs=pl.BlockSpec((tm, d), lambda i, pos: (i, 0)),
        scratch_shapes=[pltpu.VMEM((2, 2, tm, d), F32), pltpu.SemaphoreType.DMA((2,))],
    )
    return pl.pallas_call(
        functools.partial(_combine_body, final=final),
        grid_spec=grid_spec,
        out_shape=jax.ShapeDtypeStruct((t, d), F32),
        compiler_params=_cparams(("arbitrary",)),
        name="moe_combine",
    )(pos_flat, yb, x2, route, g_final)


def _route_plan(route, t, tm):
    e_flat = route[:, 0:2].astype(I32).reshape(-1)
    n_assign = e_flat.shape[0]
    onehot = (e_flat[:, None] == jnp.arange(N_EXPERTS, dtype=I32)[None, :]).astype(I32)
    csum = jnp.cumsum(onehot, axis=0)
    rank = jnp.sum(onehot * (csum - 1), axis=1)
    counts = csum[-1]
    padded = ((counts + tm - 1) // tm) * tm
    pend = jnp.cumsum(padded)
    pstart = pend - padded
    dest = (pstart[e_flat] + rank).astype(I32)
    n_rows = n_assign + N_EXPERTS * tm
    row_tok = jnp.zeros((n_rows,), I32).at[dest].set(jnp.arange(n_assign, dtype=I32) // 2)
    blk_start = jnp.arange(n_rows // tm, dtype=I32) * tm
    blk_expert = jnp.minimum(jnp.searchsorted(pend, blk_start, side='right'), N_EXPERTS - 1).astype(I32)
    n_used = (pend[-1:] // tm).astype(I32)
    return row_tok, blk_expert, n_used, dest


def _rope_tables(seq):
    half = HEAD_DIM // 2
    inv = ROPE_THETA ** (-jnp.arange(0, HEAD_DIM, 2, dtype=F32) / HEAD_DIM)
    ang = jnp.arange(seq, dtype=F32)[:, None] * inv[None, :]
    cos, sin = jnp.cos(ang), jnp.sin(ang)
    reps = LANES // HEAD_DIM
    cos_t = jnp.concatenate([cos, cos] * reps, axis=1)
    sin_t = jnp.concatenate([-sin, sin] * reps, axis=1)
    del half
    return cos_t, sin_t


def _pick_tile(n, pref):
    while n % pref:
        pref //= 2
    return pref


def kernel(x, norm_mix, w_in, na_rpb, conv_w, conv_b, dt_bias, a_log, d_skip, ssm_norm, w_out, norm_ffn,
           ffn_w_gate, ffn_w_up, ffn_w_down, router_w, exp_w_gate, exp_w_up, exp_w_down, norm_final):
    bsz, seq, d = x.shape
    t = bsz * seq
    depth = w_in.shape[0]
    assert w_in.shape[2] == MAIN_COLS + 2 * SSM_HEADS and seq % (GRID_W * NA_RB) == 0
    x2 = x.reshape(t, d).astype(F32)

    tm_proj = _pick_tile(seq, 1024)
    tm_out = _pick_tile(t, 512)
    tm_ffn = _pick_tile(t, 512)
    tf = 512
    tm_moe = _pick_tile(t, 512)
    tm_comb = _pick_tile(t, 256)
    conv_rows = _pick_tile(seq, 1024)

    cos_t, sin_t = _rope_tables(seq)
    w_main = w_in.astype(BF16)
    w_dt = jnp.pad(w_in[:, :, MAIN_COLS:], ((0, 0), (0, 0), (0, LANES - 2 * SSM_HEADS))).astype(BF16)
    w_out_b = w_out.astype(BF16)
    wg_b, wu_b, wd_b = ffn_w_gate.astype(BF16), ffn_w_up.astype(BF16), ffn_w_down.astype(BF16)
    eg_b, eu_b, ed_b = exp_w_gate.astype(BF16), exp_w_up.astype(BF16), exp_w_down.astype(BF16)
    e_fwd, e_bwd = _expand_matrix(False), _expand_matrix(True)
    na_tabs = _na_bias_tables(na_rpb, seq // GRID_W)
    pad_row = lambda v: jnp.pad(v.reshape(1, -1).astype(F32), ((0, 0), (0, LANES - v.size)))

    for layer in range(depth):
        proj, dt_raw = _inproj(x2, norm_mix[layer].reshape(1, d), w_main, w_dt, layer,
                               cos_t, sin_t, seq, tm_proj)
        y_na = _na_attention(proj, na_tabs, layer, bsz, seq)
        y_dil = _dilated_attention(proj, bsz, seq)
        xc = _conv_silu(proj, conv_w[layer].astype(F32), conv_b[layer].reshape(1, -1).astype(F32), seq, conv_rows)
        bias_row = pad_row(dt_bias[layer])
        a_row = pad_row(-jnp.exp(a_log[layer].astype(F32)))
        y_fwd = _ssd_scan(xc, dt_raw, bias_row, a_row, e_fwd, bsz, seq, False)
        dsk_row = jnp.repeat(d_skip[layer].astype(F32), HEAD_DIM).reshape(1, SSM_W)
        y_ssd = _ssd_scan(xc, dt_raw, bias_row, a_row, e_bwd, bsz, seq, True,
                          extra=(y_fwd, proj, dsk_row, ssm_norm[layer].reshape(1, SSM_W).astype(F32)))
        j = layer // 2
        g_ffn = norm_ffn[layer].reshape(1, d).astype(F32)
        if layer % 2 == 0:
            x2, hn = _outproj(y_na, y_dil, y_ssd, x2, w_out_b, layer, g_ffn, None, tm_out, BF16)
            x2 = _ffn(hn, x2, wg_b, wu_b, wd_b, j, tm_ffn, tf)
        else:
            wr = jnp.pad(router_w[j].astype(F32), ((0, 0), (0, LANES - N_EXPERTS)))
            x2, h32, route = _outproj(y_na, y_dil, y_ssd, x2, w_out_b, layer, g_ffn, wr, tm_out, F32)
            row_tok, blk_expert, n_used, dest = _route_plan(route, t, tm_moe)
            yb = _moe_ffn(row_tok, blk_expert, n_used, h32, eg_b, eu_b, ed_b, j, tm_moe, tf)
            final = layer == depth - 1
            x2 = _combine(dest, yb, x2, route, norm_final.reshape(1, d).astype(F32), tm_comb, final)
    if depth % 2 == 1:
        raise NotImplementedError("final norm is fused into the expert combine of the last (odd) layer")
    return x2.reshape(bsz, seq, d).astype(x.dtype)
```

```python
import functools

import numpy as np
import jax
import jax.numpy as jnp
from jax import lax
from jax.experimental import pallas as pl
from jax.experimental.pallas import tpu as pltpu

F32 = jnp.float32
BF16 = jnp.bfloat16
I32 = jnp.int32

GRID_W = 64
HEAD_DIM = 64
NA_KH = 8
NA_KW = 16
N_HEADS = 8
ATT_W = N_HEADS * HEAD_DIM
DIL_BRANCHES = ((128, 1), (512, 4), (2048, 16))
DIL_QBLK = 128
DIL_HALF = 64
SSM_HEADS = 16
SSM_GROUPS = 2
SSM_STATE = 128
SSM_CONV = 5
SSM_CHUNK = 128
SSM_W = SSM_HEADS * HEAD_DIM
SSM_BC = SSM_GROUPS * SSM_STATE
SSM_CONV_CH = SSM_W + 2 * SSM_BC
N_EXPERTS = 8
ROPE_THETA = 10000.0
NORM_EPS = 1e-6
NEG = -1e30

LANES = 128
VMEM_LIMIT = 56 * 1024 * 1024
PROJ_TN = 512
MAIN_COLS = 11 * PROJ_TN
INPROJ_TN = 11 * LANES
_GB = ATT_W // LANES
Q_LANE_BLOCKS = frozenset(range(0, _GB)) | frozenset(range(3 * _GB, 4 * _GB))
ROPE_LANE_BLOCKS = frozenset(range(3 * _GB, 5 * _GB))
NA_RB = 4
INPROJ_SPLIT = 4
OUTPROJ_SPLIT = 4
MOE_FIRST_SPLIT = 2
MOE_ISSUE_STEPS = 8
SSM_STEP_CHUNKS = 4
DMA_UNROLL = 8


def _cparams(sem):
    return pltpu.CompilerParams(dimension_semantics=sem, vmem_limit_bytes=VMEM_LIMIT)


def _split3(x):
    hi = x.astype(BF16)
    r = x - hi.astype(F32)
    mid = r.astype(BF16)
    lo = (r - mid.astype(F32)).astype(BF16)
    return hi, mid, lo


def _dot01_l(m01, x):
    hi, mid, lo = _split3(x)
    d = lambda a: jnp.dot(m01, a, preferred_element_type=F32)
    return d(hi) + d(mid) + d(lo)


def _dot01_r(x, m01):
    hi, mid, lo = _split3(x)
    d = lambda a: jnp.dot(a, m01, preferred_element_type=F32)
    return d(hi) + d(mid) + d(lo)


def _silu(x):
    return x / (1.0 + jnp.exp(-x))


def _inproj_body(x_ref, g_ref, w_ref, wdt_ref, cos_ref, sin_ref, o_ref, odt_ref, xn_ref):
    j = pl.program_id(1)
    tm = x_ref.shape[0]
    nblk = INPROJ_TN // LANES
    parts = [slice(p * tm // INPROJ_SPLIT, (p + 1) * tm // INPROJ_SPLIT) for p in range(INPROJ_SPLIT)]
    lane = lax.broadcasted_iota(I32, (tm // INPROJ_SPLIT, LANES), 1)
    first = (lane % HEAD_DIM) < (HEAD_DIM // 2)

    def rotary(xs, cos, sin):
        partner = jnp.where(first, pltpu.roll(xs, LANES - HEAD_DIM // 2, 1), pltpu.roll(xs, HEAD_DIM // 2, 1))
        return xs * cos + partner * sin

    def tile(jt):
        blocks = range(jt * nblk, (jt + 1) * nblk)
        for rows in parts:
            if jt == 0:
                x = x_ref[rows, :]
                ms = jnp.mean(x * x, axis=-1, keepdims=True)
                xn = (x * lax.rsqrt(ms + NORM_EPS) * g_ref[...]).astype(BF16)
                xn_ref[rows, :] = xn
                odt_ref[rows, :] = jnp.dot(xn, wdt_ref[...], preferred_element_type=F32)
            else:
                xn = xn_ref[rows, :]
            acc = jnp.dot(xn, w_ref[...], preferred_element_type=F32)
            pieces = []
            for c, gb in enumerate(blocks):
                xs = acc[:, c * LANES:(c + 1) * LANES]
                if gb in Q_LANE_BLOCKS:
                    xs = xs * HEAD_DIM ** -0.5
                if gb in ROPE_LANE_BLOCKS:
                    xs = rotary(xs, cos_ref[rows, :], sin_ref[rows, :])
                pieces.append(xs.astype(BF16))
            o_ref[rows, :] = jnp.concatenate(pieces, axis=1)

    for jt in range(MAIN_COLS // INPROJ_TN):
        pl.when(j == jt)(functools.partial(tile, jt))


def _inproj(x2, g, w_main, w_dt, layer, cos_t, sin_t, seq, tm):
    t, d = x2.shape
    n_s = seq // tm
    return pl.pallas_call(
        _inproj_body,
        grid=(t // tm, MAIN_COLS // INPROJ_TN),
        in_specs=[
            pl.BlockSpec((tm, d), lambda i, j: (i, 0)),
            pl.BlockSpec((1, d), lambda i, j: (0, 0)),
            pl.BlockSpec((None, d, INPROJ_TN), lambda i, j: (layer, 0, j)),
            pl.BlockSpec((None, d, LANES), lambda i, j: (layer, 0, 0)),
            pl.BlockSpec((tm, LANES), lambda i, j: (i % n_s, 0)),
            pl.BlockSpec((tm, LANES), lambda i, j: (i % n_s, 0)),
        ],
        out_specs=[
            pl.BlockSpec((tm, INPROJ_TN), lambda i, j: (i, j)),
            pl.BlockSpec((tm, LANES), lambda i, j: (i, 0)),
        ],
        out_shape=[jax.ShapeDtypeStruct((t, MAIN_COLS), BF16),
                   jax.ShapeDtypeStruct((t, LANES), F32)],
        scratch_shapes=[pltpu.VMEM((tm, d), BF16)],
        compiler_params=_cparams(("arbitrary", "arbitrary")),
        name="inproj",
    )(x2, g, w_main, w_dt, cos_t, sin_t)


def _na_bias_tables(rpb, rows):
    nb = rows // NA_RB
    c = np.arange(GRID_W)
    cs = np.clip(c - NA_KW // 2, 0, GRID_W - NA_KW)
    ck = np.arange(GRID_W)
    col_ok = (ck[None, :] >= cs[:, None]) & (ck[None, :] < cs[:, None] + NA_KW)
    dc = np.clip(ck[None, :] - c[:, None] + NA_KW - 1, 0, 2 * NA_KW - 2)
    onehot = (dc.reshape(-1)[None, :] == np.arange(2 * NA_KW - 1)[:, None]).astype(np.float32)
    tabs = []
    for i in (0, 1, nb - 1):
        r = NA_RB * i + np.arange(NA_RB)
        rs = np.clip(r - NA_KH // 2, 0, rows - NA_KH)
        absrow = NA_RB * (i - 1) + np.arange(3 * NA_RB)
        row_ok = (absrow[None, :] >= rs[:, None]) & (absrow[None, :] < rs[:, None] + NA_KH)
        dr = np.clip(absrow[None, :] - r[:, None] + NA_KH - 1, 0, 2 * NA_KH - 2)
        by_row = rpb.astype(F32)[:, :, dr, :]
        full = jnp.einsum('lhqkd,dx->lhqkx', by_row, jnp.asarray(onehot), precision=lax.Precision.HIGHEST)
        full = full.reshape(full.shape[:4] + (GRID_W, GRID_W))
        ok = row_ok[:, :, None, None] & col_ok[None, None, :, :]
        full = jnp.where(ok[None, None], full, NEG)
        full = jnp.transpose(full, (0, 1, 2, 4, 3, 5))
        tabs.append(full.reshape(full.shape[:2] + (NA_RB * GRID_W, 3 * NA_RB * GRID_W)))
    return jnp.stack(tabs, axis=1)


def _na_body(q_ref, kp_ref, kc_ref, kn_ref, vp_ref, vc_ref, vn_ref, b_ref, o_ref):
    for h in range(N_HEADS):
        sl = slice(h * HEAD_DIM, (h + 1) * HEAD_DIM)
        q = q_ref[:, sl]
        k = jnp.concatenate([kp_ref[:, sl], kc_ref[:, sl], kn_ref[:, sl]], axis=0)
        v = jnp.concatenate([vp_ref[:, sl], vc_ref[:, sl], vn_ref[:, sl]], axis=0)
        s = lax.dot_general(q, k, (((1,), (1,)), ((), ())), preferred_element_type=F32)
        s = s + b_ref[h]
        m = jnp.max(s, axis=-1, keepdims=True)
        p = jnp.exp(s - m)
        l = jnp.sum(p, axis=-1, keepdims=True)
        o = jnp.dot(p.astype(BF16), v, preferred_element_type=F32) / l
        o_ref[:, sl] = o.astype(BF16)


def _na_attention(proj, bias_tabs, layer, bsz, seq):
    t = bsz * seq
    tq = NA_RB * GRID_W
    nb = seq // tq
    assert nb >= 3 and seq // GRID_W >= NA_KH
    sel = lambda i: jnp.where(i == 0, 0, jnp.where(i == nb - 1, 2, 1))
    prev = lambda b, i: b * nb + jnp.maximum(i - 1, 0)
    cur = lambda b, i: b * nb + i
    nxt = lambda b, i: b * nb + jnp.minimum(i + 1, nb - 1)
    blk = lambda rowf, col: pl.BlockSpec((tq, ATT_W), lambda b, i: (rowf(b, i), col))
    return pl.pallas_call(
        _na_body,
        grid=(bsz, nb),
        in_specs=[blk(cur, 0), blk(prev, 1), blk(cur, 1), blk(nxt, 1),
                  blk(prev, 2), blk(cur, 2), blk(nxt, 2),
                  pl.BlockSpec((None, None, N_HEADS, tq, 3 * tq), lambda b, i: (layer, sel(i), 0, 0, 0))],
        out_specs=pl.BlockSpec((tq, ATT_W), lambda b, i: (cur(b, i), 0)),
        out_shape=jax.ShapeDtypeStruct((t, ATT_W), BF16),
        compiler_params=_cparams(("arbitrary", "arbitrary")),
        name="na_attn",
    )(proj, proj, proj, proj, proj, proj, proj, bias_tabs)


DIL_SUPER = 2048
DIL_STEP_Q = 1024


def _dil_split_body(x_ref, *rest):
    o_refs, scr = rest[:-1], rest[-1]
    for c in range(ATT_W // LANES):
        scr[c] = x_ref[:, c * LANES:(c + 1) * LANES].astype(F32)
    for o_ref in o_refs:
        dil = o_ref.shape[2]
        n = o_ref.shape[3]
        for rho in range(dil):
            for c in range(ATT_W // LANES):
                o_ref[0, 0, rho, :, c * LANES:(c + 1) * LANES] = (
                    scr[c, pl.ds(rho, n, stride=dil), :].astype(BF16))


def _dil_split(proj, bsz, seq, dils):
    t = bsz * seq
    nsb = seq // DIL_SUPER
    out_specs = [pl.BlockSpec((1, 1, dil, DIL_SUPER // dil, ATT_W),
                              lambda i, c: (c, i // nsb, 0, i % nsb, 0)) for dil in dils]
    out_shape = [jax.ShapeDtypeStruct((3, bsz, dil, seq // dil, ATT_W), BF16) for dil in dils]
    return pl.pallas_call(
        _dil_split_body,
        grid=(t // DIL_SUPER, 3),
        in_specs=[pl.BlockSpec((DIL_SUPER, ATT_W), lambda i, c: (i, 3 + c))],
        out_specs=out_specs, out_shape=out_shape,
        scratch_shapes=[pltpu.VMEM((ATT_W // LANES, DIL_SUPER, LANES), F32)],
        compiler_params=_cparams(("arbitrary", "arbitrary")),
        name="dil_split",
    )(proj)


def _dil_body(q_ref, kp_ref, kc_ref, kn_ref, vp_ref, vc_ref, vn_ref, o_ref, l_ref, *, n_sub):
    u = pl.program_id(2)
    qb = q_ref.shape[0]
    nsb = qb // DIL_QBLK
    nk = 2 * DIL_QBLK
    a = lax.broadcasted_iota(I32, (DIL_QBLK, nk), 0)
    kk = lax.broadcasted_iota(I32, (DIL_QBLK, nk), 1)
    band = (kk >= a) & (kk <= a + 2 * DIL_HALF)
    oks = []
    for j in range(nsb):
        uk = u * qb + j * DIL_QBLK - DIL_HALF + kk
        oks.append(band & (uk >= 0) & (uk < n_sub))
    ok = jnp.concatenate(oks, axis=0) if nsb > 1 else oks[0]
    lane = lax.broadcasted_iota(I32, (qb, LANES), 1)
    lse_tile = jnp.zeros((qb, LANES), F32)
    dn = (((1,), (1,)), ((), ()))
    for h in range(N_HEADS):
        sl = slice(h * HEAD_DIM, (h + 1) * HEAD_DIM)
        k = jnp.concatenate([kp_ref[:, sl], kc_ref[:, sl], kn_ref[:, sl]], axis=0)
        v = jnp.concatenate([vp_ref[:, sl], vc_ref[:, sl], vn_ref[:, sl]], axis=0)
        parts = [lax.dot_general(q_ref[j * DIL_QBLK:(j + 1) * DIL_QBLK, sl], k[j * DIL_QBLK:j * DIL_QBLK + nk],
                                 dn, preferred_element_type=F32) for j in range(nsb)]
        s = jnp.concatenate(parts, axis=0) if nsb > 1 else parts[0]
        s = jnp.where(ok, s, NEG)
        m = jnp.max(s, axis=-1, keepdims=True)
        p = jnp.exp(s - m)
        l = jnp.sum(p, axis=-1, keepdims=True)
        pb = p.astype(BF16)
        outs = [jnp.dot(pb[j * DIL_QBLK:(j + 1) * DIL_QBLK], v[j * DIL_QBLK:j * DIL_QBLK + nk],
                        preferred_element_type=F32) for j in range(nsb)]
        o = jnp.concatenate(outs, axis=0) if nsb > 1 else outs[0]
        o_ref[:, sl] = (o / l).astype(BF16)
        lse_tile = jnp.where(lane == h, m + jnp.log(l), lse_tile)
    l_ref[...] = lse_tile


def _dil_branch(src, bsz, seq, dil):
    n_sub = seq // dil
    qb = min(DIL_STEP_Q, n_sub)
    nu = n_sub // qb
    hpb = qb // DIL_HALF
    nhalo = n_sub // DIL_HALF
    prev = lambda u: jnp.maximum(u * hpb - 1, 0)
    nxt = lambda u: jnp.minimum((u + 1) * hpb, nhalo - 1)
    if dil == 1:
        cur = lambda c: pl.BlockSpec((qb, ATT_W), lambda b, r, u: (b * nu + u, 3 + c))
        halo = lambda hf, c: pl.BlockSpec((DIL_HALF, ATT_W), lambda b, r, u: (b * nhalo + hf(u), 3 + c))
    else:
        cur = lambda c: pl.BlockSpec((None, None, None, qb, ATT_W), lambda b, r, u: (c, b, r, u, 0))
        halo = lambda hf, c: pl.BlockSpec((None, None, None, DIL_HALF, ATT_W),
                                          lambda b, r, u: (c, b, r, hf(u), 0))
    out_blk = lambda w: pl.BlockSpec((None, None, qb, w), lambda b, r, u: (b, r, u, 0))
    return pl.pallas_call(
        functools.partial(_dil_body, n_sub=n_sub),
        grid=(bsz, dil, nu),
        in_specs=[cur(0), halo(prev, 1), cur(1), halo(nxt, 1), halo(prev, 2), cur(2), halo(nxt, 2)],
        out_specs=[out_blk(ATT_W), out_blk(LANES)],
        out_shape=[jax.ShapeDtypeStruct((bsz, dil, n_sub, ATT_W), BF16),
                   jax.ShapeDtypeStruct((bsz, dil, n_sub, LANES), F32)],
        compiler_params=_cparams(("arbitrary", "arbitrary", "arbitrary")),
        name=f"dil_attn_d{dil}",
    )(*([src] * 7))


def _dil_merge_body(*refs, dils):
    nbr = len(dils)
    o_refs, l_refs = refs[:nbr], refs[nbr:2 * nbr]
    e_ref, out_ref = refs[2 * nbr], refs[2 * nbr + 1]
    scr = refs[2 * nbr + 2:]
    outs, lses = [], []
    si = 0
    for n, dil in enumerate(dils):
        if dil == 1:
            outs.append(o_refs[n][0, 0].astype(F32))
            lses.append(l_refs[n][0, 0])
            continue
        so, sl = scr[si], scr[si + 1]
        si += 2
        rows = DIL_SUPER // dil
        for rho in range(dil):
            for c in range(ATT_W // LANES):
                so[c, pl.ds(rho, rows, stride=dil), :] = (
                    o_refs[n][0, rho, :, c * LANES:(c + 1) * LANES].astype(F32))
            sl[pl.ds(rho, rows, stride=dil), :] = l_refs[n][0, rho]
        outs.append(jnp.concatenate([so[c] for c in range(ATT_W // LANES)], axis=1))
        lses.append(sl[...])
    m = functools.reduce(jnp.maximum, lses)
    es = [jnp.exp(l - m) for l in lses]
    tot = functools.reduce(lambda x, y: x + y, es)
    acc = None
    for o, e in zip(outs, es):
        w = jnp.dot((e / tot).astype(BF16), e_ref[...], preferred_element_type=F32)
        acc = w * o if acc is None else acc + w * o
    out_ref[...] = acc.astype(BF16)


def _dil_merge(outs, lses, bsz, seq, dils):
    t = bsz * seq
    nsb = seq // DIL_SUPER
    spec = lambda dil, w: pl.BlockSpec((1, dil, DIL_SUPER // dil, w), lambda i: (i // nsb, 0, i % nsb, 0))
    e8 = np.zeros((LANES, ATT_W), np.float32)
    for h in range(N_HEADS):
        e8[h, h * HEAD_DIM:(h + 1) * HEAD_DIM] = 1.0
    scratch = []
    for dil in dils:
        if dil != 1:
            scratch += [pltpu.VMEM((ATT_W // LANES, DIL_SUPER, LANES), F32), pltpu.VMEM((DIL_SUPER, LANES), F32)]
    return pl.pallas_call(
        functools.partial(_dil_merge_body, dils=dils),
        grid=(t // DIL_SUPER,),
        in_specs=[spec(dil, ATT_W) for dil in dils] + [spec(dil, LANES) for dil in dils]
                 + [pl.BlockSpec((LANES, ATT_W), lambda i: (0, 0))],
        out_specs=pl.BlockSpec((DIL_SUPER, ATT_W), lambda i: (i, 0)),
        out_shape=jax.ShapeDtypeStruct((t, ATT_W), BF16),
        scratch_shapes=scratch,
        compiler_params=_cparams(("arbitrary",)),
        name="dil_merge",
    )(*outs, *lses, jnp.asarray(e8, BF16))


def _dilated_attention(proj, bsz, seq):
    dils = tuple(dil for _, dil in DIL_BRANCHES)
    for win, dil in DIL_BRANCHES:
        assert win // 2 // dil == DIL_HALF and DIL_SUPER % (dil * DIL_QBLK) == 0
    assert seq % DIL_SUPER == 0 and dils[0] == 1
    split = _dil_split(proj, bsz, seq, dils[1:])
    outs, lses = [], []
    for n, dil in enumerate(dils):
        o, l = _dil_branch(proj if dil == 1 else split[n - 1], bsz, seq, dil)
        outs.append(o)
        lses.append(l)
    return _dil_merge(outs, lses, bsz, seq, dils)


CONV_HALO = 16


def _conv_body(xp_ref, xc_ref, xn_ref, w_ref, b_ref, o_ref, *, blocks_per_seq):
    i = pl.program_id(0)
    r = xc_ref.shape[0]
    pos = i % blocks_per_seq
    keep_p = jnp.where(pos == 0, 0.0, 1.0).astype(F32)
    keep_n = jnp.where(pos == blocks_per_seq - 1, 0.0, 1.0).astype(F32)
    ext = jnp.concatenate([xp_ref[...].astype(F32) * keep_p, xc_ref[...].astype(F32),
                           xn_ref[...].astype(F32) * keep_n], axis=0)
    w = w_ref[...]
    half = SSM_CONV // 2
    y = b_ref[...] + jnp.zeros((r, w.shape[1]), F32)
    for k in range(SSM_CONV):
        st = CONV_HALO + k - half
        y = y + ext[st:st + r, :] * w[k:k + 1, :]
    o_ref[...] = _silu(y).astype(BF16)


def _conv_silu(proj, conv_w, conv_b, seq, rblk):
    t = proj.shape[0]
    ncol = SSM_CONV_CH // PROJ_TN
    col0 = (MAIN_COLS - SSM_CONV_CH) // PROJ_TN
    rh = rblk // CONV_HALO
    nh = t // CONV_HALO
    return pl.pallas_call(
        functools.partial(_conv_body, blocks_per_seq=seq // rblk),
        grid=(t // rblk, ncol),
        in_specs=[
            pl.BlockSpec((CONV_HALO, PROJ_TN), lambda i, c: (jnp.maximum(i * rh - 1, 0), col0 + c)),
            pl.BlockSpec((rblk, PROJ_TN), lambda i, c: (i, col0 + c)),
            pl.BlockSpec((CONV_HALO, PROJ_TN), lambda i, c: (jnp.minimum((i + 1) * rh, nh - 1), col0 + c)),
            pl.BlockSpec((SSM_CONV, PROJ_TN), lambda i, c: (0, c)),
            pl.BlockSpec((1, PROJ_TN), lambda i, c: (0, c)),
        ],
        out_specs=pl.BlockSpec((rblk, PROJ_TN), lambda i, c: (i, c)),
        out_shape=jax.ShapeDtypeStruct((t, SSM_CONV_CH), BF16),
        compiler_params=_cparams(("arbitrary", "arbitrary")),
        name="ssd_conv",
    )(proj, proj, proj, conv_w, conv_b)


def _ssd_body(*refs, reverse, final):
    if final:
        (x_ref, b_ref, c_ref, dt_ref, bias_ref, a_ref, e_ref,
         yf_ref, z_ref, dsk_ref, ng_ref, o_ref, state_ref, y_ref) = refs
    else:
        x_ref, b_ref, c_ref, dt_ref, bias_ref, a_ref, e_ref, o_ref, state_ref, y_ref = refs
    q = SSM_CHUNK
    gw = SSM_W // SSM_GROUPS
    hpg = SSM_HEADS // SSM_GROUPS
    off = SSM_HEADS if reverse else 0
    end = 0 if reverse else q - 1

    @pl.when(pl.program_id(1) == 0)
    def _():
        state_ref[...] = jnp.zeros_like(state_ref)

    row = lax.broadcasted_iota(I32, (q, q), 0)
    col = lax.broadcasted_iota(I32, (q, q), 1)
    tri = (col >= row) if reverse else (row >= col)
    tri01 = jnp.where(tri, 1.0, 0.0).astype(BF16)
    lane_lo = lax.broadcasted_iota(I32, (q, 2 * HEAD_DIM), 1) < HEAD_DIM
    n_chunks = x_ref.shape[0] // q
    for k in (range(n_chunks - 1, -1, -1) if reverse else range(n_chunks)):
        rows = slice(k * q, (k + 1) * q)
        xr = dt_ref[rows, :] + bias_ref[...]
        dt = jnp.maximum(xr, 0.0) + jnp.log(1.0 + jnp.exp(-jnp.abs(xr)))
        dta = dt * a_ref[...]
        acs = _dot01_l(tri01, dta)
        acs_t = acs.T
        acs_end = acs[end:end + 1, :]
        wide = jnp.dot(jnp.concatenate([dt, jnp.exp(acs), jnp.exp(acs_end - acs)], axis=0).astype(BF16),
                       e_ref[...], preferred_element_type=F32)
        dt_x, e_acs, ds_x = wide[:q], wide[q:2 * q], wide[2 * q:]
        cd_x = _dot01_r(jnp.broadcast_to(jnp.exp(acs_end), (8, LANES)), e_ref[...])[0:1, :]
        xs = x_ref[rows, :].astype(F32)
        xdt = xs * dt_x
        xds = (xdt * ds_x).astype(BF16)
        xdt = xdt.astype(BF16)

        for g in range(SSM_GROUPS):
            gs = slice(g * gw, (g + 1) * gw)
            bg = b_ref[rows, g * SSM_STATE:(g + 1) * SSM_STATE]
            cg = c_ref[rows, g * SSM_STATE:(g + 1) * SSM_STATE]
            cb = lax.dot_general(cg, bg, (((1,), (1,)), ((), ())), preferred_element_type=F32)
            for r in range(0, hpg, 2):
                hh = g * hpg + r
                ms = []
                for h2 in (hh, hh + 1):
                    diff = acs[:, off + h2:off + h2 + 1] - acs_t[off + h2:off + h2 + 1, :]
                    ms.append((cb * jnp.exp(jnp.where(tri, diff, -jnp.inf))).astype(BF16))
                ps = slice(hh * HEAD_DIM, (hh + 2) * HEAD_DIM)
                xp = xdt[:, ps]
                zero = jnp.zeros_like(xp)
                rhs = jnp.concatenate([jnp.where(lane_lo, xp, zero), jnp.where(lane_lo, zero, xp)], axis=0)
                y_ref[rows, ps] = jnp.dot(jnp.concatenate(ms, axis=1), rhs, preferred_element_type=F32)
            prev = state_ref[g]
            y_off = jnp.dot(cg, prev.astype(BF16), preferred_element_type=F32) * e_acs[:, gs]
            y_ref[rows, gs] = y_ref[rows, gs] + y_off
            bt = bg.astype(F32).T.astype(BF16)
            st = jnp.dot(bt, xds[:, gs], preferred_element_type=F32)
            state_ref[g] = cd_x[:, gs] * prev + st

        if final:
            y = yf_ref[rows, :] + y_ref[rows, :] + dsk_ref[...] * xs
            y = y * _silu(z_ref[rows, :].astype(F32))
            for g in range(SSM_GROUPS):
                gs = slice(g * gw, (g + 1) * gw)
                yg = y[:, gs]
                yg = yg * lax.rsqrt(jnp.mean(yg * yg, axis=-1, keepdims=True) + NORM_EPS)
                o_ref[rows, gs] = (yg * ng_ref[:, gs]).astype(o_ref.dtype)
        else:
            o_ref[rows, :] = y_ref[rows, :]


def _ssd_scan(xc, dt_raw, bias_row, a_row, e_mat, bsz, seq, reverse, extra=None):
    t = bsz * seq
    q = SSM_CHUNK * SSM_STEP_CHUNKS
    assert seq % q == 0
    nc = seq // q
    final = extra is not None
    rowf =(lambda b, c: b * nc + (nc - 1 - c)) if reverse else (lambda b, c: b * nc + c)
    const = lambda w: pl.BlockSpec((1, w), lambda b, c: (0, 0))
    in_specs = [
        pl.BlockSpec((q, SSM_W), lambda b, c: (rowf(b, c), 0)),
        pl.BlockSpec((q, SSM_BC), lambda b, c: (rowf(b, c), SSM_W // SSM_BC)),
        pl.BlockSpec((q, SSM_BC), lambda b, c: (rowf(b, c), SSM_W // SSM_BC + 1)),
        pl.BlockSpec((q, LANES), lambda b, c: (rowf(b, c), 0)),
        const(LANES), const(LANES),
        pl.BlockSpec((LANES, SSM_W), lambda b, c: (0, 0)),
    ]
    args = [xc, xc, xc, dt_raw, bias_row, a_row, e_mat]
    if final:
        y_fwd, proj, dsk_row, ng_row = extra
        zcol = (3 * ATT_W * 2) // SSM_W
        in_specs += [pl.BlockSpec((q, SSM_W), lambda b, c: (rowf(b, c), 0)),
                     pl.BlockSpec((q, SSM_W), lambda b, c: (rowf(b, c), zcol)),
                     const(SSM_W), const(SSM_W)]
        args += [y_fwd, proj, dsk_row, ng_row]
    return pl.pallas_call(
        functools.partial(_ssd_body, reverse=reverse, final=final),
        grid=(bsz, nc),
        in_specs=in_specs,
        out_specs=pl.BlockSpec((q, SSM_W), lambda b, c: (rowf(b, c), 0)),
        out_shape=jax.ShapeDtypeStruct((t, SSM_W), BF16 if final else F32),
        scratch_shapes=[pltpu.VMEM((SSM_GROUPS, SSM_STATE, SSM_W // SSM_GROUPS), F32),
                        pltpu.VMEM((q, SSM_W), F32)],
        compiler_params=_cparams(("arbitrary", "arbitrary")),
        name="ssd_bwd" if reverse else "ssd_fwd",
    )(*args)


def _expand_matrix(reverse):
    e = np.zeros((LANES, SSM_W), np.float32)
    off = SSM_HEADS if reverse else 0
    for h in range(SSM_HEADS):
        e[off + h, h * HEAD_DIM:(h + 1) * HEAD_DIM] = 1.0
    return jnp.asarray(e, BF16)


def _outproj_body(*refs, router):
    if router:
        yna_ref, ydil_ref, yssd_ref, x_ref, w_ref, g_ref, wr_ref, xo_ref, hn_ref, rt_ref = refs
    else:
        yna_ref, ydil_ref, yssd_ref, x_ref, w_ref, g_ref, xo_ref, hn_ref = refs
    tm = x_ref.shape[0]
    for part in range(OUTPROJ_SPLIT):
        rows = slice(part * tm // OUTPROJ_SPLIT, (part + 1) * tm // OUTPROJ_SPLIT)
        acc = x_ref[rows, :]
        acc = acc + jnp.dot(yna_ref[rows, :], w_ref[0:ATT_W, :], preferred_element_type=F32)
        acc = acc + jnp.dot(ydil_ref[rows, :], w_ref[ATT_W:2 * ATT_W, :], preferred_element_type=F32)
        acc = acc + jnp.dot(yssd_ref[rows, :], w_ref[2 * ATT_W:, :], preferred_element_type=F32)
        xo_ref[rows, :] = acc
        ms = jnp.mean(acc * acc, axis=-1, keepdims=True)
        h = acc * lax.rsqrt(ms + NORM_EPS) * g_ref[...]
        hn_ref[rows, :] = h.astype(hn_ref.dtype)
        if router:
            h_hi = h.astype(BF16)
            h_lo = (h - h_hi.astype(F32)).astype(BF16)
            wr = wr_ref[...]
            w_hi = wr.astype(BF16)
            w_lo = (wr - w_hi.astype(F32)).astype(BF16)
            d = lambda a, b: jnp.dot(a, b, preferred_element_type=F32)
            logits = d(h_hi, w_hi) + d(h_hi, w_lo) + d(h_lo, w_hi)
            lane = lax.broadcasted_iota(I32, logits.shape, 1)
            logits = jnp.where(lane < N_EXPERTS, logits, -jnp.inf)
            m1 = jnp.max(logits, axis=-1, keepdims=True)
            i1 = jnp.min(jnp.where(logits == m1, lane, LANES), axis=-1, keepdims=True)
            rest = jnp.where(lane == i1, -jnp.inf, logits)
            m2 = jnp.max(rest, axis=-1, keepdims=True)
            i2 = jnp.min(jnp.where(rest == m2, lane, LANES), axis=-1, keepdims=True)
            e = jnp.exp(m2 - m1)
            g1 = 1.0 / (1.0 + e)
            g2 = e / (1.0 + e)
            rt = jnp.where(lane == 0, i1.astype(F32),
                           jnp.where(lane == 1, i2.astype(F32),
                                     jnp.where(lane == 2, g1, jnp.where(lane == 3, g2, 0.0))))
            rt_ref[rows, :] = rt


def _outproj(y_na, y_dil, y_ssd, x2, w_out, layer, g, w_router, tm, hn_dtype):
    t, d = x2.shape
    router = w_router is not None
    row = lambda w: pl.BlockSpec((tm, w), lambda i: (i, 0))
    in_specs = [row(ATT_W), row(ATT_W), row(SSM_W), row(d),
                pl.BlockSpec((None, d, d), lambda i: (layer, 0, 0)),
                pl.BlockSpec((1, d), lambda i: (0, 0))]
    args = [y_na, y_dil, y_ssd, x2, w_out, g]
    out_specs = [row(d), row(d)]
    out_shape = [jax.ShapeDtypeStruct((t, d), F32), jax.ShapeDtypeStruct((t, d), hn_dtype)]
    if router:
        in_specs.append(pl.BlockSpec((d, LANES), lambda i: (0, 0)))
        args.append(w_router)
        out_specs.append(row(LANES))
        out_shape.append(jax.ShapeDtypeStruct((t, LANES), F32))
    return pl.pallas_call(
        functools.partial(_outproj_body, router=router),
        grid=(t // tm,),
        in_specs=in_specs, out_specs=out_specs, out_shape=out_shape,
        compiler_params=_cparams(("arbitrary",)),
        name="outproj_router" if router else "outproj",
    )(*args)


def _ffn_body(hn_ref, x_ref, wg_ref, wu_ref, wd_ref, o_ref, acc_ref):
    f = pl.program_id(1)
    last = pl.num_programs(1) - 1

    def swiglu_down():
        h = hn_ref[...]
        g = jnp.dot(h, wg_ref[...], preferred_element_type=F32)
        u = jnp.dot(h, wu_ref[...], preferred_element_type=F32)
        return jnp.dot((_silu(g) * u).astype(BF16), wd_ref[...], preferred_element_type=F32)

    @pl.when(f == 0)
    def _():
        acc_ref[...] = x_ref[...] + swiglu_down()

    @pl.when((f != 0) & (f != last))
    def _():
        acc_ref[...] += swiglu_down()

    @pl.when(f == last)
    def _():
        o_ref[...] = acc_ref[...] + swiglu_down()


def _ffn(hn, x2, wg, wu, wd, j, tm, tf):
    t, d = x2.shape
    fdim = wg.shape[2]
    return pl.pallas_call(
        _ffn_body,
        grid=(t // tm, fdim // tf),
        in_specs=[pl.BlockSpec((tm, d), lambda i, f: (i, 0)),
                  pl.BlockSpec((tm, d), lambda i, f: (i, 0)),
                  pl.BlockSpec((None, d, tf), lambda i, f: (j, 0, f)),
                  pl.BlockSpec((None, d, tf), lambda i, f: (j, 0, f)),
                  pl.BlockSpec((None, tf, d), lambda i, f: (j, f, 0))],
        out_specs=pl.BlockSpec((tm, d), lambda i, f: (i, 0)),
        out_shape=jax.ShapeDtypeStruct((t, d), F32),
        scratch_shapes=[pltpu.VMEM((tm, d), F32)],
        compiler_params=_cparams(("arbitrary", "arbitrary")),
        name="ffn",
    )(hn, x2, wg, wu, wd)


def _moe_row_copy(h_hbm, xg_ref, sem, tok, slot, r):
    return pltpu.make_async_copy(h_hbm.at[pl.ds(tok, 1), :], xg_ref.at[slot, pl.ds(r, 1), :], sem.at[slot])


def _moe_ffn_body(tok_ref, be_ref, nu_ref, h_hbm, wg_ref, wu_ref, wd_ref, o_ref, xg_ref, xb_ref, acc_ref, sem):
    i = pl.program_id(0)
    f = pl.program_id(1)
    tm = xb_ref.shape[0]
    n_used = nu_ref[0]
    active = i < n_used
    slot = i % 2

    def start_gather(blk, dst_slot):
        def issue(r8, carry):
            for k in range(DMA_UNROLL):
                r = r8 * DMA_UNROLL + k
                _moe_row_copy(h_hbm, xg_ref, sem, tok_ref[blk * tm + r], dst_slot, r).start()
            return carry
        lax.fori_loop(0, tm // DMA_UNROLL, issue, 0)

    @pl.when((f == 0) & (i == 0))
    def _():
        start_gather(0, 0)

    nf = pl.num_programs(1)
    share = tm // MOE_ISSUE_STEPS
    first = active & (f == 0)
    last = active & (f == nf - 1)
    fetch = active & (f >= 1) & (f <= MOE_ISSUE_STEPS) & (i + 1 < n_used)

    def swiglu_down(h):
        g = jnp.dot(h, wg_ref[...], preferred_element_type=F32)
        u = jnp.dot(h, wu_ref[...], preferred_element_type=F32)
        return jnp.dot((_silu(g) * u).astype(BF16), wd_ref[...], preferred_element_type=F32)

    @pl.when(first)
    def _():
        pltpu.make_async_copy(h_hbm.at[pl.ds(0, tm), :], xg_ref.at[slot], sem.at[slot]).wait()
        for p in range(MOE_FIRST_SPLIT):
            rows = slice(p * tm // MOE_FIRST_SPLIT, (p + 1) * tm // MOE_FIRST_SPLIT)
            h = xg_ref[slot, rows, :].astype(BF16)
            xb_ref[rows, :] = h
            acc_ref[rows, :] = swiglu_down(h)

    @pl.when(fetch)
    def _():
        acc_ref[...] += swiglu_down(xb_ref[...])
        base = (f - 1) * share
        for k in range(share):
            r = base + k
            _moe_row_copy(h_hbm, xg_ref, sem, tok_ref[(i + 1) * tm + r], 1 - slot, r).start()

    @pl.when(active & jnp.logical_not(first | last | fetch))
    def _():
        acc_ref[...] += swiglu_down(xb_ref[...])

    @pl.when(last)
    def _():
        o_ref[...] = acc_ref[...] + swiglu_down(xb_ref[...])

    @pl.when(jnp.logical_not(active) & (f == 0))
    def _():
        o_ref[...] = jnp.zeros_like(o_ref)


def _moe_ffn(row_tok, blk_expert, n_used, h32, wg, wu, wd, j, tm, tf):
    n_rows = row_tok.shape[0]
    d = h32.shape[1]
    fdim = wg.shape[3]
    nf = fdim // tf
    assert nf > MOE_ISSUE_STEPS and tm % MOE_ISSUE_STEPS == 0 and tm % DMA_UNROLL == 0
    fidx = lambda i, f, nu: jnp.where(i < nu[0], f, nf - 1)
    grid_spec = pltpu.PrefetchScalarGridSpec(
        num_scalar_prefetch=3,
        grid=(n_rows // tm, nf),
        in_specs=[pl.BlockSpec(memory_space=pl.ANY),
                  pl.BlockSpec((None, None, d, tf), lambda i, f, tok, be, nu: (j, be[i], 0, fidx(i, f, nu))),
                  pl.BlockSpec((None, None, d, tf), lambda i, f, tok, be, nu: (j, be[i], 0, fidx(i, f, nu))),
                  pl.BlockSpec((None, None, tf, d), lambda i, f, tok, be, nu: (j, be[i], fidx(i, f, nu), 0))],
        out_specs=pl.BlockSpec((tm, d), lambda i, f, tok, be, nu: (i, 0)),
        scratch_shapes=[pltpu.VMEM((2, tm, d), F32), pltpu.VMEM((tm, d), BF16),
                        pltpu.VMEM((tm, d), F32), pltpu.SemaphoreType.DMA((2,))],
    )
    return pl.pallas_call(
        _moe_ffn_body,
        grid_spec=grid_spec,
        out_shape=jax.ShapeDtypeStruct((n_rows, d), F32),
        compiler_params=_cparams(("arbitrary", "arbitrary")),
        name="moe_ffn",
    )(row_tok, blk_expert, n_used, h32, wg, wu, wd)


def _combine_row_copy(y_hbm, buf_ref, sem, src, slot, k, r):
    return pltpu.make_async_copy(y_hbm.at[pl.ds(src, 1), :], buf_ref.at[slot, k, pl.ds(r, 1), :], sem.at[slot])


def _combine_body(pos_ref, y_hbm, x_ref, rt_ref, gf_ref, o_ref, buf_ref, sem, *, final):
    i = pl.program_id(0)
    tm = x_ref.shape[0]
    slot = i % 2

    def start_gather(step, dst_slot):
        def issue(r4, carry):
            for u in range(DMA_UNROLL // 2):
                r = r4 * (DMA_UNROLL // 2) + u
                for k in range(2):
                    _combine_row_copy(y_hbm, buf_ref, sem, pos_ref[2 * (step * tm + r) + k], dst_slot, k, r).start()
            return carry
        lax.fori_loop(0, tm // (DMA_UNROLL // 2), issue, 0)

    @pl.when(i == 0)
    def _():
        start_gather(0, 0)

    @pl.when(i + 1 < pl.num_programs(0))
    def _():
        start_gather(i + 1, 1 - slot)

    for k in range(2):
        pltpu.make_async_copy(y_hbm.at[pl.ds(0, tm), :], buf_ref.at[slot, k], sem.at[slot]).wait()

    rt = rt_ref[...]
    out = x_ref[...] + rt[:, 2:3] * buf_ref[slot, 0] + rt[:, 3:4] * buf_ref[slot, 1]
    if final:
        ms = jnp.mean(out * out, axis=-1, keepdims=True)
        out = out * lax.rsqrt(ms + NORM_EPS) * gf_ref[...]
    o_ref[...] = out


def _combine(pos_flat, yb, x2, route, g_final, tm, final):
    t, d = x2.shape
    grid_spec = pltpu.PrefetchScalarGridSpec(
        num_scalar_prefetch=1,
        grid=(t // tm,),
        in_specs=[pl.BlockSpec(memory_space=pl.ANY),
                  pl.BlockSpec((tm, d), lambda i, pos: (i, 0)),
                  pl.BlockSpec((tm, LANES), lambda i, pos: (i, 0)),
                  pl.BlockSpec((1, d), lambda i, pos: (0, 0))],
        out_specs=pl.BlockSpec((tm, d), lambda i, pos: (i, 0)),
        scratch_shapes=[pltpu.VMEM((2, 2, tm, d), F32), pltpu.SemaphoreType.DMA((2,))],
    )
    return pl.pallas_call(
        functools.partial(_combine_body, final=final),
        grid_spec=grid_spec,
        out_shape=jax.ShapeDtypeStruct((t, d), F32),
        compiler_params=_cparams(("arbitrary",)),
        name="moe_combine",
    )(pos_flat, yb, x2, route, g_final)


def _route_plan(route, t, tm):
    e_flat = route[:, 0:2].astype(I32).reshape(-1)
    n_assign = e_flat.shape[0]
    onehot = (e_flat[:, None] == jnp.arange(N_EXPERTS, dtype=I32)[None, :]).astype(I32)
    csum = jnp.cumsum(onehot, axis=0)
    rank = jnp.sum(onehot * (csum - 1), axis=1)
    counts = csum[-1]
    padded = ((counts + tm - 1) // tm) * tm
    pend = jnp.cumsum(padded)
    pstart = pend - padded
    dest = (pstart[e_flat] + rank).astype(I32)
    n_rows = n_assign + N_EXPERTS * tm
    row_tok = jnp.zeros((n_rows,), I32).at[dest].set(jnp.arange(n_assign, dtype=I32) // 2)
    blk_start = jnp.arange(n_rows // tm, dtype=I32) * tm
    blk_expert = jnp.minimum(jnp.searchsorted(pend, blk_start, side='right'), N_EXPERTS - 1).astype(I32)
    n_used = (pend[-1:] // tm).astype(I32)
    return row_tok, blk_expert, n_used, dest


def _rope_tables(seq):
    half = HEAD_DIM // 2
    inv = ROPE_THETA ** (-jnp.arange(0, HEAD_DIM, 2, dtype=F32) / HEAD_DIM)
    ang = jnp.arange(seq, dtype=F32)[:, None] * inv[None, :]
    cos, sin = jnp.cos(ang), jnp.sin(ang)
    reps = LANES // HEAD_DIM
    cos_t = jnp.concatenate([cos, cos] * reps, axis=1)
    sin_t = jnp.concatenate([-sin, sin] * reps, axis=1)
    del half
    return cos_t, sin_t


def _pick_tile(n, pref):
    while n % pref:
        pref //= 2
    return pref


def kernel(x, norm_mix, w_in, na_rpb, conv_w, conv_b, dt_bias, a_log, d_skip, ssm_norm, w_out, norm_ffn,
           ffn_w_gate, ffn_w_up, ffn_w_down, router_w, exp_w_gate, exp_w_up, exp_w_down, norm_final):
    bsz, seq, d = x.shape
    t = bsz * seq
    depth = w_in.shape[0]
    assert w_in.shape[2] == MAIN_COLS + 2 * SSM_HEADS and seq % (GRID_W * NA_RB) == 0
    x2 = x.reshape(t, d).astype(F32)

    tm_proj = _pick_tile(seq, 1024)
    tm_out = _pick_tile(t, 512)
    tm_ffn = _pick_tile(t, 512)
    tf = 512
    tm_moe = _pick_tile(t, 512)
    tm_comb = _pick_tile(t, 256)
    conv_rows = _pick_tile(seq, 1024)

    cos_t, sin_t = _rope_tables(seq)
    w_main = w_in.astype(BF16)
    w_dt = jnp.pad(w_in[:, :, MAIN_COLS:], ((0, 0), (0, 0), (0, LANES - 2 * SSM_HEADS))).astype(BF16)
    w_out_b = w_out.astype(BF16)
    wg_b, wu_b, wd_b = ffn_w_gate.astype(BF16), ffn_w_up.astype(BF16), ffn_w_down.astype(BF16)
    eg_b, eu_b, ed_b = exp_w_gate.astype(BF16), exp_w_up.astype(BF16), exp_w_down.astype(BF16)
    e_fwd, e_bwd = _expand_matrix(False), _expand_matrix(True)
    na_tabs = _na_bias_tables(na_rpb, seq // GRID_W)
    pad_row = lambda v: jnp.pad(v.reshape(1, -1).astype(F32), ((0, 0), (0, LANES - v.size)))

    for layer in range(depth):
        proj, dt_raw = _inproj(x2, norm_mix[layer].reshape(1, d), w_main, w_dt, layer,
                               cos_t, sin_t, seq, tm_proj)
        y_na = _na_attention(proj, na_tabs, layer, bsz, seq)
        y_dil = _dilated_attention(proj, bsz, seq)
        xc = _conv_silu(proj, conv_w[layer].astype(F32), conv_b[layer].reshape(1, -1).astype(F32), seq, conv_rows)
        bias_row = pad_row(dt_bias[layer])
        a_row = pad_row(-jnp.exp(a_log[layer].astype(F32)))
        y_fwd = _ssd_scan(xc, dt_raw, bias_row, a_row, e_fwd, bsz, seq, False)
        dsk_row = jnp.repeat(d_skip[layer].astype(F32), HEAD_DIM).reshape(1, SSM_W)
        y_ssd = _ssd_scan(xc, dt_raw, bias_row, a_row, e_bwd, bsz, seq, True,
                          extra=(y_fwd, proj, dsk_row, ssm_norm[layer].reshape(1, SSM_W).astype(F32)))
        j = layer // 2
        g_ffn = norm_ffn[layer].reshape(1, d).astype(F32)
        if layer % 2 == 0:
            x2, hn = _outproj(y_na, y_dil, y_ssd, x2, w_out_b, layer, g_ffn, None, tm_out, BF16)
            x2 = _ffn(hn, x2, wg_b, wu_b, wd_b, j, tm_ffn, tf)
        else:
            wr = jnp.pad(router_w[j].astype(F32), ((0, 0), (0, LANES - N_EXPERTS)))
            x2, h32, route = _outproj(y_na, y_dil, y_ssd, x2, w_out_b, layer, g_ffn, wr, tm_out, F32)
            row_tok, blk_expert, n_used, dest = _route_plan(route, t, tm_moe)
            yb = _moe_ffn(row_tok, blk_expert, n_used, h32, eg_b, eu_b, ed_b, j, tm_moe, tf)
            final = layer == depth - 1
            x2 = _combine(dest, yb, x2, route, norm_final.reshape(1, d).astype(F32), tm_comb, final)
    if depth % 2 == 1:
        raise NotImplementedError("final norm is fused into the expert combine of the last (odd) layer")
    return x2.reshape(bsz, seq, d).astype(x.dtype)
```

```python
import functools

import numpy as np
import jax
import jax.numpy as jnp
from jax import lax
from jax.experimental import pallas as pl
from jax.experimental.pallas import tpu as pltpu

F32 = jnp.float32
BF16 = jnp.bfloat16
I32 = jnp.int32

GRID_W = 64
HEAD_DIM = 64
NA_KH = 8
NA_KW = 16
N_HEADS = 8
ATT_W = N_HEADS * HEAD_DIM
DIL_BRANCHES = ((128, 1), (512, 4), (2048, 16))
DIL_QBLK = 128
DIL_HALF = 64
SSM_HEADS = 16
SSM_GROUPS = 2
SSM_STATE = 128
SSM_CONV = 5
SSM_CHUNK = 128
SSM_W = SSM_HEADS * HEAD_DIM
SSM_BC = SSM_GROUPS * SSM_STATE
SSM_CONV_CH = SSM_W + 2 * SSM_BC
N_EXPERTS = 8
ROPE_THETA = 10000.0
NORM_EPS = 1e-6
NEG = -1e30

LANES = 128
VMEM_LIMIT = 56 * 1024 * 1024
PROJ_TN = 512
MAIN_COLS = 11 * PROJ_TN
INPROJ_TN = 11 * LANES
_GB = ATT_W // LANES
Q_LANE_BLOCKS = frozenset(range(0, _GB)) | frozenset(range(3 * _GB, 4 * _GB))
ROPE_LANE_BLOCKS = frozenset(range(3 * _GB, 5 * _GB))
NA_RB = 4
INPROJ_SPLIT = 4
OUTPROJ_SPLIT = 4
MOE_FIRST_SPLIT = 2
MOE_ISSUE_STEPS = 8
SSM_STEP_CHUNKS = 8
DMA_UNROLL = 8


def _cparams(sem):
    return pltpu.CompilerParams(dimension_semantics=sem, vmem_limit_bytes=VMEM_LIMIT)


def _split3(x):
    hi = x.astype(BF16)
    r = x - hi.astype(F32)
    mid = r.astype(BF16)
    lo = (r - mid.astype(F32)).astype(BF16)
    return hi, mid, lo


def _dot01_l(m01, x):
    hi, mid, lo = _split3(x)
    d = lambda a: jnp.dot(m01, a, preferred_element_type=F32)
    return d(hi) + d(mid) + d(lo)


def _dot01_r(x, m01):
    hi, mid, lo = _split3(x)
    d = lambda a: jnp.dot(a, m01, preferred_element_type=F32)
    return d(hi) + d(mid) + d(lo)


def _silu(x):
    return x / (1.0 + jnp.exp(-x))


def _inproj_body(x_ref, g_ref, w_ref, wdt_ref, cos_ref, sin_ref, o_ref, odt_ref, xn_ref):
    j = pl.program_id(1)
    tm = x_ref.shape[0]
    nblk = INPROJ_TN // LANES
    parts = [slice(p * tm // INPROJ_SPLIT, (p + 1) * tm // INPROJ_SPLIT) for p in range(INPROJ_SPLIT)]
    lane = lax.broadcasted_iota(I32, (tm // INPROJ_SPLIT, LANES), 1)
    first = (lane % HEAD_DIM) < (HEAD_DIM // 2)

    def rotary(xs, cos, sin):
        partner = jnp.where(first, pltpu.roll(xs, LANES - HEAD_DIM // 2, 1), pltpu.roll(xs, HEAD_DIM // 2, 1))
        return xs * cos + partner * sin

    def tile(jt):
        blocks = range(jt * nblk, (jt + 1) * nblk)
        for rows in parts:
            if jt == 0:
                x = x_ref[rows, :]
                ms = jnp.mean(x * x, axis=-1, keepdims=True)
                xn = (x * lax.rsqrt(ms + NORM_EPS) * g_ref[...]).astype(BF16)
                xn_ref[rows, :] = xn
                odt_ref[rows, :] = jnp.dot(xn, wdt_ref[...], preferred_element_type=F32)
            else:
                xn = xn_ref[rows, :]
            acc = jnp.dot(xn, w_ref[...], preferred_element_type=F32)
            pieces = []
            for c, gb in enumerate(blocks):
                xs = acc[:, c * LANES:(c + 1) * LANES]
                if gb in Q_LANE_BLOCKS:
                    xs = xs * HEAD_DIM ** -0.5
                if gb in ROPE_LANE_BLOCKS:
                    xs = rotary(xs, cos_ref[rows, :], sin_ref[rows, :])
                pieces.append(xs.astype(BF16))
            o_ref[rows, :] = jnp.concatenate(pieces, axis=1)

    for jt in range(MAIN_COLS // INPROJ_TN):
        pl.when(j == jt)(functools.partial(tile, jt))


def _inproj(x2, g, w_main, w_dt, layer, cos_t, sin_t, seq, tm):
    t, d = x2.shape
    n_s = seq // tm
    return pl.pallas_call(
        _inproj_body,
        grid=(t // tm, MAIN_COLS // INPROJ_TN),
        in_specs=[
            pl.BlockSpec((tm, d), lambda i, j: (i, 0)),
            pl.BlockSpec((1, d), lambda i, j: (0, 0)),
            pl.BlockSpec((None, d, INPROJ_TN), lambda i, j: (layer, 0, j)),
            pl.BlockSpec((None, d, LANES), lambda i, j: (layer, 0, 0)),
            pl.BlockSpec((tm, LANES), lambda i, j: (i % n_s, 0)),
            pl.BlockSpec((tm, LANES), lambda i, j: (i % n_s, 0)),
        ],
        out_specs=[
            pl.BlockSpec((tm, INPROJ_TN), lambda i, j: (i, j)),
            pl.BlockSpec((tm, LANES), lambda i, j: (i, 0)),
        ],
        out_shape=[jax.ShapeDtypeStruct((t, MAIN_COLS), BF16),
                   jax.ShapeDtypeStruct((t, LANES), F32)],
        scratch_shapes=[pltpu.VMEM((tm, d), BF16)],
        compiler_params=_cparams(("arbitrary", "arbitrary")),
        name="inproj",
    )(x2, g, w_main, w_dt, cos_t, sin_t)


def _na_bias_tables(rpb, rows):
    nb = rows // NA_RB
    c = np.arange(GRID_W)
    cs = np.clip(c - NA_KW // 2, 0, GRID_W - NA_KW)
    ck = np.arange(GRID_W)
    col_ok = (ck[None, :] >= cs[:, None]) & (ck[None, :] < cs[:, None] + NA_KW)
    dc = np.clip(ck[None, :] - c[:, None] + NA_KW - 1, 0, 2 * NA_KW - 2)
    onehot = (dc.reshape(-1)[None, :] == np.arange(2 * NA_KW - 1)[:, None]).astype(np.float32)
    dr = np.clip(np.arange(3 * NA_RB)[None, :] - NA_RB - np.arange(NA_RB)[:, None] + NA_KH - 1, 0, 2 * NA_KH - 2)
    by_row = rpb.astype(F32)[:, :, dr, :]
    full = jnp.einsum('lhqkd,dx->lhqkx', by_row, jnp.asarray(onehot), precision=lax.Precision.HIGHEST)
    full = full.reshape(full.shape[:4] + (GRID_W, GRID_W))
    full = jnp.transpose(full, (0, 1, 2, 4, 3, 5))
    tabs = []
    for i in (0, 1, nb - 1):
        r = NA_RB * i + np.arange(NA_RB)
        rs = np.clip(r - NA_KH // 2, 0, rows - NA_KH)
        absrow = NA_RB * (i - 1) + np.arange(3 * NA_RB)
        row_ok = (absrow[None, :] >= rs[:, None]) & (absrow[None, :] < rs[:, None] + NA_KH)
        ok = row_ok[:, None, :, None] & col_ok[None, :, None, :]
        tab = jnp.where(ok[None, None], full, NEG)
        tabs.append(tab.reshape(tab.shape[:2] + (NA_RB * GRID_W, 3 * NA_RB * GRID_W)))
    return jnp.stack(tabs, axis=1)


def _na_body(q_ref, kp_ref, kc_ref, kn_ref, vp_ref, vc_ref, vn_ref, b_ref, o_ref):
    for h in range(N_HEADS):
        sl = slice(h * HEAD_DIM, (h + 1) * HEAD_DIM)
        q = q_ref[:, sl]
        k = jnp.concatenate([kp_ref[:, sl], kc_ref[:, sl], kn_ref[:, sl]], axis=0)
        v = jnp.concatenate([vp_ref[:, sl], vc_ref[:, sl], vn_ref[:, sl]], axis=0)
        s = lax.dot_general(q, k, (((1,), (1,)), ((), ())), preferred_element_type=F32)
        s = s + b_ref[h]
        m = jnp.max(s, axis=-1, keepdims=True)
        p = jnp.exp(s - m)
        l = jnp.sum(p, axis=-1, keepdims=True)
        o = jnp.dot(p.astype(BF16), v, preferred_element_type=F32) / l
        o_ref[:, sl] = o.astype(BF16)


def _na_attention(proj, bias_tabs, layer, bsz, seq):
    t = bsz * seq
    tq = NA_RB * GRID_W
    nb = seq // tq
    assert nb >= 3 and seq // GRID_W >= NA_KH
    sel = lambda i: jnp.where(i == 0, 0, jnp.where(i == nb - 1, 2, 1))
    prev = lambda b, i: b * nb + jnp.maximum(i - 1, 0)
    cur = lambda b, i: b * nb + i
    nxt = lambda b, i: b * nb + jnp.minimum(i + 1, nb - 1)
    blk = lambda rowf, col: pl.BlockSpec((tq, ATT_W), lambda b, i: (rowf(b, i), col))
    return pl.pallas_call(
        _na_body,
        grid=(bsz, nb),
        in_specs=[blk(cur, 0), blk(prev, 1), blk(cur, 1), blk(nxt, 1),
                  blk(prev, 2), blk(cur, 2), blk(nxt, 2),
                  pl.BlockSpec((None, None, N_HEADS, tq, 3 * tq), lambda b, i: (layer, sel(i), 0, 0, 0))],
        out_specs=pl.BlockSpec((tq, ATT_W), lambda b, i: (cur(b, i), 0)),
        out_shape=jax.ShapeDtypeStruct((t, ATT_W), BF16),
        compiler_params=_cparams(("arbitrary", "arbitrary")),
        name="na_attn",
    )(proj, proj, proj, proj, proj, proj, proj, bias_tabs)


DIL_SUPER = 2048
DIL_STEP_Q = 1024


def _dil_split_body(x_ref, *rest):
    o_refs, scr = rest[:-1], rest[-1]
    for c in range(ATT_W // LANES):
        scr[c] = x_ref[:, c * LANES:(c + 1) * LANES].astype(F32)
    for o_ref in o_refs:
        dil = o_ref.shape[2]
        n = o_ref.shape[3]
        for rho in range(dil):
            for c in range(ATT_W // LANES):
                o_ref[0, 0, rho, :, c * LANES:(c + 1) * LANES] = (
                    scr[c, pl.ds(rho, n, stride=dil), :].astype(BF16))


def _dil_split(proj, bsz, seq, dils):
    t = bsz * seq
    nsb = seq // DIL_SUPER
    out_specs = [pl.BlockSpec((1, 1, dil, DIL_SUPER // dil, ATT_W),
                              lambda i, c: (c, i // nsb, 0, i % nsb, 0)) for dil in dils]
    out_shape = [jax.ShapeDtypeStruct((3, bsz, dil, seq // dil, ATT_W), BF16) for dil in dils]
    return pl.pallas_call(
        _dil_split_body,
        grid=(t // DIL_SUPER, 3),
        in_specs=[pl.BlockSpec((DIL_SUPER, ATT_W), lambda i, c: (i, 3 + c))],
        out_specs=out_specs, out_shape=out_shape,
        scratch_shapes=[pltpu.VMEM((ATT_W // LANES, DIL_SUPER, LANES), F32)],
        compiler_params=_cparams(("arbitrary", "arbitrary")),
        name="dil_split",
    )(proj)


def _dil_body(q_ref, kp_ref, kc_ref, kn_ref, vp_ref, vc_ref, vn_ref, o_ref, l_ref, *, n_sub):
    u = pl.program_id(2)
    qb = q_ref.shape[0]
    nsb = qb // DIL_QBLK
    nk = 2 * DIL_QBLK
    a = lax.broadcasted_iota(I32, (DIL_QBLK, nk), 0)
    kk = lax.broadcasted_iota(I32, (DIL_QBLK, nk), 1)
    band = (kk >= a) & (kk <= a + 2 * DIL_HALF)
    oks = []
    for j in range(nsb):
        uk = u * qb + j * DIL_QBLK - DIL_HALF + kk
        oks.append(band & (uk >= 0) & (uk < n_sub))
    ok = jnp.concatenate(oks, axis=0) if nsb > 1 else oks[0]
    lane = lax.broadcasted_iota(I32, (qb, LANES), 1)
    lse_tile = jnp.zeros((qb, LANES), F32)
    dn = (((1,), (1,)), ((), ()))
    for h in range(N_HEADS):
        sl = slice(h * HEAD_DIM, (h + 1) * HEAD_DIM)
        k = jnp.concatenate([kp_ref[:, sl], kc_ref[:, sl], kn_ref[:, sl]], axis=0)
        v = jnp.concatenate([vp_ref[:, sl], vc_ref[:, sl], vn_ref[:, sl]], axis=0)
        parts = [lax.dot_general(q_ref[j * DIL_QBLK:(j + 1) * DIL_QBLK, sl], k[j * DIL_QBLK:j * DIL_QBLK + nk],
                                 dn, preferred_element_type=F32) for j in range(nsb)]
        s = jnp.concatenate(parts, axis=0) if nsb > 1 else parts[0]
        s = jnp.where(ok, s, NEG)
        m = jnp.max(s, axis=-1, keepdims=True)
        p = jnp.exp(s - m)
        l = jnp.sum(p, axis=-1, keepdims=True)
        pb = p.astype(BF16)
        outs = [jnp.dot(pb[j * DIL_QBLK:(j + 1) * DIL_QBLK], v[j * DIL_QBLK:j * DIL_QBLK + nk],
                        preferred_element_type=F32) for j in range(nsb)]
        o = jnp.concatenate(outs, axis=0) if nsb > 1 else outs[0]
        o_ref[:, sl] = (o / l).astype(BF16)
        lse_tile = jnp.where(lane == h, m + jnp.log(l), lse_tile)
    l_ref[...] = lse_tile


def _dil_branch(src, bsz, seq, dil):
    n_sub = seq // dil
    qb = min(DIL_STEP_Q, n_sub)
    nu = n_sub // qb
    hpb = qb // DIL_HALF
    nhalo = n_sub // DIL_HALF
    prev = lambda u: jnp.maximum(u * hpb - 1, 0)
    nxt = lambda u: jnp.minimum((u + 1) * hpb, nhalo - 1)
    if dil == 1:
        cur = lambda c: pl.BlockSpec((qb, ATT_W), lambda b, r, u: (b * nu + u, 3 + c))
        halo = lambda hf, c: pl.BlockSpec((DIL_HALF, ATT_W), lambda b, r, u: (b * nhalo + hf(u), 3 + c))
    else:
        cur = lambda c: pl.BlockSpec((None, None, None, qb, ATT_W), lambda b, r, u: (c, b, r, u, 0))
        halo = lambda hf, c: pl.BlockSpec((None, None, None, DIL_HALF, ATT_W),
                                          lambda b, r, u: (c, b, r, hf(u), 0))
    out_blk = lambda w: pl.BlockSpec((None, None, qb, w), lambda b, r, u: (b, r, u, 0))
    return pl.pallas_call(
        functools.partial(_dil_body, n_sub=n_sub),
        grid=(bsz, dil, nu),
        in_specs=[cur(0), halo(prev, 1), cur(1), halo(nxt, 1), halo(prev, 2), cur(2), halo(nxt, 2)],
        out_specs=[out_blk(ATT_W), out_blk(LANES)],
        out_shape=[jax.ShapeDtypeStruct((bsz, dil, n_sub, ATT_W), BF16),
                   jax.ShapeDtypeStruct((bsz, dil, n_sub, LANES), F32)],
        compiler_params=_cparams(("arbitrary", "arbitrary", "arbitrary")),
        name=f"dil_attn_d{dil}",
    )(*([src] * 7))


def _dil_merge_body(*refs, dils):
    nbr = len(dils)
    o_refs, l_refs = refs[:nbr], refs[nbr:2 * nbr]
    e_ref, out_ref = refs[2 * nbr], refs[2 * nbr + 1]
    scr = refs[2 * nbr + 2:]
    outs, lses = [], []
    si = 0
    for n, dil in enumerate(dils):
        if dil == 1:
            outs.append(o_refs[n][0, 0].astype(F32))
            lses.append(l_refs[n][0, 0])
            continue
        so, sl = scr[si], scr[si + 1]
        si += 2
        rows = DIL_SUPER // dil
        for rho in range(dil):
            for c in range(ATT_W // LANES):
                so[c, pl.ds(rho, rows, stride=dil), :] = (
                    o_refs[n][0, rho, :, c * LANES:(c + 1) * LANES].astype(F32))
            sl[pl.ds(rho, rows, stride=dil), :] = l_refs[n][0, rho]
        outs.append(jnp.concatenate([so[c] for c in range(ATT_W // LANES)], axis=1))
        lses.append(sl[...])
    m = functools.reduce(jnp.maximum, lses)
    es = [jnp.exp(l - m) for l in lses]
    tot = functools.reduce(lambda x, y: x + y, es)
    acc = None
    for o, e in zip(outs, es):
        w = jnp.dot((e / tot).astype(BF16), e_ref[...], preferred_element_type=F32)
        acc = w * o if acc is None else acc + w * o
    out_ref[...] = acc.astype(BF16)


def _dil_merge(outs, lses, bsz, seq, dils):
    t = bsz * seq
    nsb = seq // DIL_SUPER
    spec = lambda dil, w: pl.BlockSpec((1, dil, DIL_SUPER // dil, w), lambda i: (i // nsb, 0, i % nsb, 0))
    e8 = np.zeros((LANES, ATT_W), np.float32)
    for h in range(N_HEADS):
        e8[h, h * HEAD_DIM:(h + 1) * HEAD_DIM] = 1.0
    scratch = []
    for dil in dils:
        if dil != 1:
            scratch += [pltpu.VMEM((ATT_W // LANES, DIL_SUPER, LANES), F32), pltpu.VMEM((DIL_SUPER, LANES), F32)]
    return pl.pallas_call(
        functools.partial(_dil_merge_body, dils=dils),
        grid=(t // DIL_SUPER,),
        in_specs=[spec(dil, ATT_W) for dil in dils] + [spec(dil, LANES) for dil in dils]
                 + [pl.BlockSpec((LANES, ATT_W), lambda i: (0, 0))],
        out_specs=pl.BlockSpec((DIL_SUPER, ATT_W), lambda i: (i, 0)),
        out_shape=jax.ShapeDtypeStruct((t, ATT_W), BF16),
        scratch_shapes=scratch,
        compiler_params=_cparams(("arbitrary",)),
        name="dil_merge",
    )(*outs, *lses, jnp.asarray(e8, BF16))


def _dilated_attention(proj, bsz, seq):
    dils = tuple(dil for _, dil in DIL_BRANCHES)
    for win, dil in DIL_BRANCHES:
        assert win // 2 // dil == DIL_HALF and DIL_SUPER % (dil * DIL_QBLK) == 0
    assert seq % DIL_SUPER == 0 and dils[0] == 1
    split = _dil_split(proj, bsz, seq, dils[1:])
    outs, lses = [], []
    for n, dil in enumerate(dils):
        o, l = _dil_branch(proj if dil == 1 else split[n - 1], bsz, seq, dil)
        outs.append(o)
        lses.append(l)
    return _dil_merge(outs, lses, bsz, seq, dils)


CONV_HALO = 16


def _conv_body(xp_ref, xc_ref, xn_ref, w_ref, b_ref, o_ref, *, blocks_per_seq):
    i = pl.program_id(0)
    r = xc_ref.shape[0]
    pos = i % blocks_per_seq
    keep_p = jnp.where(pos == 0, 0.0, 1.0).astype(F32)
    keep_n = jnp.where(pos == blocks_per_seq - 1, 0.0, 1.0).astype(F32)
    ext = jnp.concatenate([xp_ref[...].astype(F32) * keep_p, xc_ref[...].astype(F32),
                           xn_ref[...].astype(F32) * keep_n], axis=0)
    w = w_ref[...]
    half = SSM_CONV // 2
    y = b_ref[...] + jnp.zeros((r, w.shape[1]), F32)
    for k in range(SSM_CONV):
        st = CONV_HALO + k - half
        y = y + ext[st:st + r, :] * w[k:k + 1, :]
    o_ref[...] = _silu(y).astype(BF16)


def _conv_silu(proj, conv_w, conv_b, seq, rblk):
    t = proj.shape[0]
    ncol = SSM_CONV_CH // PROJ_TN
    col0 = (MAIN_COLS - SSM_CONV_CH) // PROJ_TN
    rh = rblk // CONV_HALO
    nh = t // CONV_HALO
    return pl.pallas_call(
        functools.partial(_conv_body, blocks_per_seq=seq // rblk),
        grid=(t // rblk, ncol),
        in_specs=[
            pl.BlockSpec((CONV_HALO, PROJ_TN), lambda i, c: (jnp.maximum(i * rh - 1, 0), col0 + c)),
            pl.BlockSpec((rblk, PROJ_TN), lambda i, c: (i, col0 + c)),
            pl.BlockSpec((CONV_HALO, PROJ_TN), lambda i, c: (jnp.minimum((i + 1) * rh, nh - 1), col0 + c)),
            pl.BlockSpec((SSM_CONV, PROJ_TN), lambda i, c: (0, c)),
            pl.BlockSpec((1, PROJ_TN), lambda i, c: (0, c)),
        ],
        out_specs=pl.BlockSpec((rblk, PROJ_TN), lambda i, c: (i, c)),
        out_shape=jax.ShapeDtypeStruct((t, SSM_CONV_CH), BF16),
        compiler_params=_cparams(("arbitrary", "arbitrary")),
        name="ssd_conv",
    )(proj, proj, proj, conv_w, conv_b)


def _ssd_body(*refs, reverse, final):
    if final:
        (x_ref, b_ref, c_ref, dt_ref, bias_ref, a_ref, e_ref,
         yf_ref, z_ref, dsk_ref, ng_ref, o_ref, state_ref, y_ref) = refs
    else:
        x_ref, b_ref, c_ref, dt_ref, bias_ref, a_ref, e_ref, o_ref, state_ref, y_ref = refs
    q = SSM_CHUNK
    gw = SSM_W // SSM_GROUPS
    hpg = SSM_HEADS // SSM_GROUPS
    off = SSM_HEADS if reverse else 0
    end = 0 if reverse else q - 1

    @pl.when(pl.program_id(1) == 0)
    def _():
        state_ref[...] = jnp.zeros_like(state_ref)

    row = lax.broadcasted_iota(I32, (q, q), 0)
    col = lax.broadcasted_iota(I32, (q, q), 1)
    tri = (col >= row) if reverse else (row >= col)
    tri01 = jnp.where(tri, 1.0, 0.0).astype(BF16)
    lane_lo = lax.broadcasted_iota(I32, (q, 2 * HEAD_DIM), 1) < HEAD_DIM
    n_chunks = x_ref.shape[0] // q
    for k in (range(n_chunks - 1, -1, -1) if reverse else range(n_chunks)):
        rows = slice(k * q, (k + 1) * q)
        xr = dt_ref[rows, :] + bias_ref[...]
        dt = jnp.maximum(xr, 0.0) + jnp.log(1.0 + jnp.exp(-jnp.abs(xr)))
        dta = dt * a_ref[...]
        acs = _dot01_l(tri01, dta)
        acs_t = acs.T
        acs_end = acs[end:end + 1, :]
        wide = jnp.dot(jnp.concatenate([dt, jnp.exp(acs), jnp.exp(acs_end - acs)], axis=0).astype(BF16),
                       e_ref[...], preferred_element_type=F32)
        dt_x, e_acs, ds_x = wide[:q], wide[q:2 * q], wide[2 * q:]
        cd_x = _dot01_r(jnp.broadcast_to(jnp.exp(acs_end), (8, LANES)), e_ref[...])[0:1, :]
        xs = x_ref[rows, :].astype(F32)
        xdt = xs * dt_x
        xds = (xdt * ds_x).astype(BF16)
        xdt = xdt.astype(BF16)

        for g in range(SSM_GROUPS):
            gs = slice(g * gw, (g + 1) * gw)
            bg = b_ref[rows, g * SSM_STATE:(g + 1) * SSM_STATE]
            cg = c_ref[rows, g * SSM_STATE:(g + 1) * SSM_STATE]
            cb = lax.dot_general(cg, bg, (((1,), (1,)), ((), ())), preferred_element_type=F32)
            for r in range(0, hpg, 2):
                hh = g * hpg + r
                ms = []
                for h2 in (hh, hh + 1):
                    diff = acs[:, off + h2:off + h2 + 1] - acs_t[off + h2:off + h2 + 1, :]
                    ms.append((cb * jnp.exp(jnp.where(tri, diff, -jnp.inf))).astype(BF16))
                ps = slice(hh * HEAD_DIM, (hh + 2) * HEAD_DIM)
                xp = xdt[:, ps]
                zero = jnp.zeros_like(xp)
                rhs = jnp.concatenate([jnp.where(lane_lo, xp, zero), jnp.where(lane_lo, zero, xp)], axis=0)
                y_ref[rows, ps] = jnp.dot(jnp.concatenate(ms, axis=1), rhs, preferred_element_type=F32)
            prev = state_ref[g]
            y_off = jnp.dot(cg, prev.astype(BF16), preferred_element_type=F32) * e_acs[:, gs]
            y_ref[rows, gs] = y_ref[rows, gs] + y_off
            bt = bg.astype(F32).T.astype(BF16)
            st = jnp.dot(bt, xds[:, gs], preferred_element_type=F32)
            state_ref[g] = cd_x[:, gs] * prev + st

        if final:
            y = yf_ref[rows, :] + y_ref[rows, :] + dsk_ref[...] * xs
            y = y * _silu(z_ref[rows, :].astype(F32))
            for g in range(SSM_GROUPS):
                gs = slice(g * gw, (g + 1) * gw)
                yg = y[:, gs]
                yg = yg * lax.rsqrt(jnp.mean(yg * yg, axis=-1, keepdims=True) + NORM_EPS)
                o_ref[rows, gs] = (yg * ng_ref[:, gs]).astype(o_ref.dtype)
        else:
            o_ref[rows, :] = y_ref[rows, :]


def _ssd_scan(xc, dt_raw, bias_row, a_row, e_mat, bsz, seq, reverse, extra=None):
    t = bsz * seq
    q = SSM_CHUNK * SSM_STEP_CHUNKS
    assert seq % q == 0
    nc = seq // q
    final = extra is not None
    rowf =(lambda b, c: b * nc + (nc - 1 - c)) if reverse else (lambda b, c: b * nc + c)
    const = lambda w: pl.BlockSpec((1, w), lambda b, c: (0, 0))
    in_specs = [
        pl.BlockSpec((q, SSM_W), lambda b, c: (rowf(b, c), 0)),
        pl.BlockSpec((q, SSM_BC), lambda b, c: (rowf(b, c), SSM_W // SSM_BC)),
        pl.BlockSpec((q, SSM_BC), lambda b, c: (rowf(b, c), SSM_W // SSM_BC + 1)),
        pl.BlockSpec((q, LANES), lambda b, c: (rowf(b, c), 0)),
        const(LANES), const(LANES),
        pl.BlockSpec((LANES, SSM_W), lambda b, c: (0, 0)),
    ]
    args = [xc, xc, xc, dt_raw, bias_row, a_row, e_mat]
    if final:
        y_fwd, proj, dsk_row, ng_row = extra
        zcol = (3 * ATT_W * 2) // SSM_W
        in_specs += [pl.BlockSpec((q, SSM_W), lambda b, c: (rowf(b, c), 0)),
                     pl.BlockSpec((q, SSM_W), lambda b, c: (rowf(b, c), zcol)),
                     const(SSM_W), const(SSM_W)]
        args += [y_fwd, proj, dsk_row, ng_row]
    return pl.pallas_call(
        functools.partial(_ssd_body, reverse=reverse, final=final),
        grid=(bsz, nc),
        in_specs=in_specs,
        out_specs=pl.BlockSpec((q, SSM_W), lambda b, c: (rowf(b, c), 0)),
        out_shape=jax.ShapeDtypeStruct((t, SSM_W), BF16 if final else F32),
        scratch_shapes=[pltpu.VMEM((SSM_GROUPS, SSM_STATE, SSM_W // SSM_GROUPS), F32),
                        pltpu.VMEM((q, SSM_W), F32)],
        compiler_params=_cparams(("arbitrary", "arbitrary")),
        name="ssd_bwd" if reverse else "ssd_fwd",
    )(*args)


def _expand_matrix(reverse):
    e = np.zeros((LANES, SSM_W), np.float32)
    off = SSM_HEADS if reverse else 0
    for h in range(SSM_HEADS):
        e[off + h, h * HEAD_DIM:(h + 1) * HEAD_DIM] = 1.0
    return jnp.asarray(e, BF16)


def _outproj_body(*refs, router):
    if router:
        yna_ref, ydil_ref, yssd_ref, x_ref, w_ref, g_ref, wr_ref, xo_ref, hn_ref, rt_ref = refs
    else:
        yna_ref, ydil_ref, yssd_ref, x_ref, w_ref, g_ref, xo_ref, hn_ref = refs
    tm = x_ref.shape[0]
    n_parts = OUTPROJ_SPLIT if router else 1
    for part in range(n_parts):
        rows = slice(part * tm // n_parts, (part + 1) * tm // n_parts)
        acc = x_ref[rows, :]
        acc = acc + jnp.dot(yna_ref[rows, :], w_ref[0:ATT_W, :], preferred_element_type=F32)
        acc = acc + jnp.dot(ydil_ref[rows, :], w_ref[ATT_W:2 * ATT_W, :], preferred_element_type=F32)
        acc = acc + jnp.dot(yssd_ref[rows, :], w_ref[2 * ATT_W:, :], preferred_element_type=F32)
        xo_ref[rows, :] = acc
        ms = jnp.mean(acc * acc, axis=-1, keepdims=True)
        h = acc * lax.rsqrt(ms + NORM_EPS) * g_ref[...]
        hn_ref[rows, :] = h.astype(hn_ref.dtype)
        if router:
            h_hi = h.astype(BF16)
            h_lo = (h - h_hi.astype(F32)).astype(BF16)
            wr = wr_ref[...]
            w_hi = wr.astype(BF16)
            w_lo = (wr - w_hi.astype(F32)).astype(BF16)
            d = lambda a, b: jnp.dot(a, b, preferred_element_type=F32)
            logits = d(h_hi, w_hi) + d(h_hi, w_lo) + d(h_lo, w_hi)
            lane = lax.broadcasted_iota(I32, logits.shape, 1)
            logits = jnp.where(lane < N_EXPERTS, logits, -jnp.inf)
            m1 = jnp.max(logits, axis=-1, keepdims=True)
            i1 = jnp.min(jnp.where(logits == m1, lane, LANES), axis=-1, keepdims=True)
            rest = jnp.where(lane == i1, -jnp.inf, logits)
            m2 = jnp.max(rest, axis=-1, keepdims=True)
            i2 = jnp.min(jnp.where(rest == m2, lane, LANES), axis=-1, keepdims=True)
            e = jnp.exp(m2 - m1)
            g1 = 1.0 / (1.0 + e)
            g2 = e / (1.0 + e)
            rt = jnp.where(lane == 0, i1.astype(F32),
                           jnp.where(lane == 1, i2.astype(F32),
                                     jnp.where(lane == 2, g1, jnp.where(lane == 3, g2, 0.0))))
            rt_ref[rows, :] = rt


def _outproj(y_na, y_dil, y_ssd, x2, w_out, layer, g, w_router, tm, hn_dtype):
    t, d = x2.shape
    router = w_router is not None
    row = lambda w: pl.BlockSpec((tm, w), lambda i: (i, 0))
    in_specs = [row(ATT_W), row(ATT_W), row(SSM_W), row(d),
                pl.BlockSpec((None, d, d), lambda i: (layer, 0, 0)),
                pl.BlockSpec((1, d), lambda i: (0, 0))]
    args = [y_na, y_dil, y_ssd, x2, w_out, g]
    out_specs = [row(d), row(d)]
    out_shape = [jax.ShapeDtypeStruct((t, d), F32), jax.ShapeDtypeStruct((t, d), hn_dtype)]
    if router:
        in_specs.append(pl.BlockSpec((d, LANES), lambda i: (0, 0)))
        args.append(w_router)
        out_specs.append(row(LANES))
        out_shape.append(jax.ShapeDtypeStruct((t, LANES), F32))
    return pl.pallas_call(
        functools.partial(_outproj_body, router=router),
        grid=(t // tm,),
        in_specs=in_specs, out_specs=out_specs, out_shape=out_shape,
        compiler_params=_cparams(("arbitrary",)),
        name="outproj_router" if router else "outproj",
    )(*args)


def _ffn_body(hn_ref, x_ref, wg_ref, wu_ref, wd_ref, o_ref, acc_ref):
    f = pl.program_id(1)
    last = pl.num_programs(1) - 1

    def swiglu_down():
        h = hn_ref[...]
        g = jnp.dot(h, wg_ref[...], preferred_element_type=F32)
        u = jnp.dot(h, wu_ref[...], preferred_element_type=F32)
        return jnp.dot((_silu(g) * u).astype(BF16), wd_ref[...], preferred_element_type=F32)

    @pl.when(f == 0)
    def _():
        acc_ref[...] = x_ref[...] + swiglu_down()

    @pl.when((f != 0) & (f != last))
    def _():
        acc_ref[...] += swiglu_down()

    @pl.when(f == last)
    def _():
        o_ref[...] = acc_ref[...] + swiglu_down()


def _ffn(hn, x2, wg, wu, wd, j, tm, tf):
    t, d = x2.shape
    fdim = wg.shape[2]
    return pl.pallas_call(
        _ffn_body,
        grid=(t // tm, fdim // tf),
        in_specs=[pl.BlockSpec((tm, d), lambda i, f: (i, 0)),
                  pl.BlockSpec((tm, d), lambda i, f: (i, 0)),
                  pl.BlockSpec((None, d, tf), lambda i, f: (j, 0, f)),
                  pl.BlockSpec((None, d, tf), lambda i, f: (j, 0, f)),
                  pl.BlockSpec((None, tf, d), lambda i, f: (j, f, 0))],
        out_specs=pl.BlockSpec((tm, d), lambda i, f: (i, 0)),
        out_shape=jax.ShapeDtypeStruct((t, d), F32),
        scratch_shapes=[pltpu.VMEM((tm, d), F32)],
        compiler_params=_cparams(("arbitrary", "arbitrary")),
        name="ffn",
    )(hn, x2, wg, wu, wd)


def _moe_row_copy(h_hbm, xg_ref, sem, tok, slot, r):
    return pltpu.make_async_copy(h_hbm.at[pl.ds(tok, 1), :], xg_ref.at[slot, pl.ds(r, 1), :], sem.at[slot])


def _moe_ffn_body(tok_ref, be_ref, nu_ref, h_hbm, wg_ref, wu_ref, wd_ref, o_ref, xg_ref, xb_ref, acc_ref, sem):
    i = pl.program_id(0)
    f = pl.program_id(1)
    tm = xb_ref.shape[0]
    n_used = nu_ref[0]
    active = i < n_used
    slot = i % 2

    def start_gather(blk, dst_slot):
        def issue(r8, carry):
            for k in range(DMA_UNROLL):
                r = r8 * DMA_UNROLL + k
                _moe_row_copy(h_hbm, xg_ref, sem, tok_ref[blk * tm + r], dst_slot, r).start()
            return carry
        lax.fori_loop(0, tm // DMA_UNROLL, issue, 0)

    @pl.when((f == 0) & (i == 0))
    def _():
        start_gather(0, 0)

    nf = pl.num_programs(1)
    share = tm // MOE_ISSUE_STEPS
    first = active & (f == 0)
    last = active & (f == nf - 1)
    fetch = active & (f >= 1) & (f <= MOE_ISSUE_STEPS) & (i + 1 < n_used)

    def swiglu_down(h):
        g = jnp.dot(h, wg_ref[...], preferred_element_type=F32)
        u = jnp.dot(h, wu_ref[...], preferred_element_type=F32)
        return jnp.dot((_silu(g) * u).astype(BF16), wd_ref[...], preferred_element_type=F32)

    @pl.when(first)
    def _():
        pltpu.make_async_copy(h_hbm.at[pl.ds(0, tm), :], xg_ref.at[slot], sem.at[slot]).wait()
        for p in range(MOE_FIRST_SPLIT):
            rows = slice(p * tm // MOE_FIRST_SPLIT, (p + 1) * tm // MOE_FIRST_SPLIT)
            h = xg_ref[slot, rows, :].astype(BF16)
            xb_ref[rows, :] = h
            acc_ref[rows, :] = swiglu_down(h)

    @pl.when(fetch)
    def _():
        acc_ref[...] += swiglu_down(xb_ref[...])
        base = (f - 1) * share
        for k in range(share):
            r = base + k
            _moe_row_copy(h_hbm, xg_ref, sem, tok_ref[(i + 1) * tm + r], 1 - slot, r).start()

    @pl.when(active & jnp.logical_not(first | last | fetch))
    def _():
        acc_ref[...] += swiglu_down(xb_ref[...])

    @pl.when(last)
    def _():
        o_ref[...] = acc_ref[...] + swiglu_down(xb_ref[...])

    @pl.when(jnp.logical_not(active) & (f == 0))
    def _():
        o_ref[...] = jnp.zeros_like(o_ref)


def _moe_ffn(row_tok, blk_expert, n_used, h32, wg, wu, wd, j, tm, tf):
    n_rows = row_tok.shape[0]
    d = h32.shape[1]
    fdim = wg.shape[3]
    nf = fdim // tf
    assert nf > MOE_ISSUE_STEPS and tm % MOE_ISSUE_STEPS == 0 and tm % DMA_UNROLL == 0
    fidx = lambda i, f, nu: jnp.where(i < nu[0], f, nf - 1)
    grid_spec = pltpu.PrefetchScalarGridSpec(
        num_scalar_prefetch=3,
        grid=(n_rows // tm, nf),
        in_specs=[pl.BlockSpec(memory_space=pl.ANY),
                  pl.BlockSpec((None, None, d, tf), lambda i, f, tok, be, nu: (j, be[i], 0, fidx(i, f, nu))),
                  pl.BlockSpec((None, None, d, tf), lambda i, f, tok, be, nu: (j, be[i], 0, fidx(i, f, nu))),
                  pl.BlockSpec((None, None, tf, d), lambda i, f, tok, be, nu: (j, be[i], fidx(i, f, nu), 0))],
        out_specs=pl.BlockSpec((tm, d), lambda i, f, tok, be, nu: (i, 0)),
        scratch_shapes=[pltpu.VMEM((2, tm, d), F32), pltpu.VMEM((tm, d), BF16),
                        pltpu.VMEM((tm, d), F32), pltpu.SemaphoreType.DMA((2,))],
    )
    return pl.pallas_call(
        _moe_ffn_body,
        grid_spec=grid_spec,
        out_shape=jax.ShapeDtypeStruct((n_rows, d), F32),
        compiler_params=_cparams(("arbitrary", "arbitrary")),
        name="moe_ffn",
    )(row_tok, blk_expert, n_used, h32, wg, wu, wd)


def _combine_row_copy(y_hbm, buf_ref, sem, src, slot, k, r):
    return pltpu.make_async_copy(y_hbm.at[pl.ds(src, 1), :], buf_ref.at[slot, k, pl.ds(r, 1), :], sem.at[slot])


def _combine_body(pos_ref, y_hbm, x_ref, rt_ref, gf_ref, o_ref, buf_ref, sem, *, final):
    i = pl.program_id(0)
    tm = x_ref.shape[0]
    slot = i % 2

    def start_gather(step, dst_slot):
        def issue(r4, carry):
            for u in range(DMA_UNROLL // 2):
                r = r4 * (DMA_UNROLL // 2) + u
                for k in range(2):
                    _combine_row_copy(y_hbm, buf_ref, sem, pos_ref[2 * (step * tm + r) + k], dst_slot, k, r).start()
            return carry
        lax.fori_loop(0, tm // (DMA_UNROLL // 2), issue, 0)

    @pl.when(i == 0)
    def _():
        start_gather(0, 0)

    @pl.when(i + 1 < pl.num_programs(0))
    def _():
        start_gather(i + 1, 1 - slot)

    for k in range(2):
        pltpu.make_async_copy(y_hbm.at[pl.ds(0, tm), :], buf_ref.at[slot, k], sem.at[slot]).wait()

    rt = rt_ref[...]
    out = x_ref[...] + rt[:, 2:3] * buf_ref[slot, 0] + rt[:, 3:4] * buf_ref[slot, 1]
    if final:
        ms = jnp.mean(out * out, axis=-1, keepdims=True)
        out = out * lax.rsqrt(ms + NORM_EPS) * gf_ref[...]
    o_ref[...] = out


def _combine(pos_flat, yb, x2, route, g_final, tm, final):
    t, d = x2.shape
    grid_spec = pltpu.PrefetchScalarGridSpec(
        num_scalar_prefetch=1,
        grid=(t // tm,),
        in_specs=[pl.BlockSpec(memory_space=pl.ANY),
                  pl.BlockSpec((tm, d), lambda i, pos: (i, 0)),
                  pl.BlockSpec((tm, LANES), lambda i, pos: (i, 0)),
                  pl.BlockSpec((1, d), lambda i, pos: (0, 0))],
        out_specs=pl.BlockSpec((tm, d), lambda i, pos: (i, 0)),
        scratch_shapes=[pltpu.VMEM((2, 2, tm, d), F32), pltpu.SemaphoreType.DMA((2,))],
    )
    return pl.pallas_call(
        functools.partial(_combine_body, final=final),
        grid_spec=grid_spec,
        out_shape=jax.ShapeDtypeStruct((t, d), F32),
        compiler_params=_cparams(("arbitrary",)),
        name="moe_combine",
    )(pos_flat, yb, x2, route, g_final)


def _route_plan(route, t, tm):
    e_flat = route[:, 0:2].astype(I32).reshape(-1)
    n_assign = e_flat.shape[0]
    onehot = (e_flat[:, None] == jnp.arange(N_EXPERTS, dtype=I32)[None, :]).astype(I32)
    csum = jnp.cumsum(onehot, axis=0)
    rank = jnp.sum(onehot * (csum - 1), axis=1)
    counts = csum[-1]
    padded = ((counts + tm - 1) // tm) * tm
    pend = jnp.cumsum(padded)
    pstart = pend - padded
    dest = (pstart[e_flat] + rank).astype(I32)
    n_rows = n_assign + N_EXPERTS * tm
    row_tok = jnp.zeros((n_rows,), I32).at[dest].set(jnp.arange(n_assign, dtype=I32) // 2)
    blk_start = jnp.arange(n_rows // tm, dtype=I32) * tm
    blk_expert = jnp.minimum(jnp.searchsorted(pend, blk_start, side='right'), N_EXPERTS - 1).astype(I32)
    n_used = (pend[-1:] // tm).astype(I32)
    return row_tok, blk_expert, n_used, dest


def _rope_tables(seq):
    half = HEAD_DIM // 2
    inv = ROPE_THETA ** (-jnp.arange(0, HEAD_DIM, 2, dtype=F32) / HEAD_DIM)
    ang = jnp.arange(seq, dtype=F32)[:, None] * inv[None, :]
    cos, sin = jnp.cos(ang), jnp.sin(ang)
    reps = LANES // HEAD_DIM
    cos_t = jnp.concatenate([cos, cos] * reps, axis=1)
    sin_t = jnp.concatenate([-sin, sin] * reps, axis=1)
    del half
    return cos_t, sin_t


def _pick_tile(n, pref):
    while n % pref:
        pref //= 2
    return pref


def kernel(x, norm_mix, w_in, na_rpb, conv_w, conv_b, dt_bias, a_log, d_skip, ssm_norm, w_out, norm_ffn,
           ffn_w_gate, ffn_w_up, ffn_w_down, router_w, exp_w_gate, exp_w_up, exp_w_down, norm_final):
    bsz, seq, d = x.shape
    t = bsz * seq
    depth = w_in.shape[0]
    assert w_in.shape[2] == MAIN_COLS + 2 * SSM_HEADS and seq % (GRID_W * NA_RB) == 0
    x2 = x.reshape(t, d).astype(F32)

    tm_proj = _pick_tile(seq, 1024)
    tm_out = _pick_tile(t, 512)
    tm_ffn = _pick_tile(t, 512)
    tf = 512
    tm_moe = _pick_tile(t, 512)
    tm_comb = _pick_tile(t, 256)
    conv_rows = _pick_tile(seq, 1024)

    cos_t, sin_t = _rope_tables(seq)
    w_main = w_in.astype(BF16)
    w_dt = jnp.pad(w_in[:, :, MAIN_COLS:], ((0, 0), (0, 0), (0, LANES - 2 * SSM_HEADS))).astype(BF16)
    w_out_b = w_out.astype(BF16)
    wg_b, wu_b, wd_b = ffn_w_gate.astype(BF16), ffn_w_up.astype(BF16), ffn_w_down.astype(BF16)
    eg_b, eu_b, ed_b = exp_w_gate.astype(BF16), exp_w_up.astype(BF16), exp_w_down.astype(BF16)
    e_fwd, e_bwd = _expand_matrix(False), _expand_matrix(True)
    na_tabs = _na_bias_tables(na_rpb, seq // GRID_W)
    pad_row = lambda v: jnp.pad(v.reshape(1, -1).astype(F32), ((0, 0), (0, LANES - v.size)))

    for layer in range(depth):
        proj, dt_raw = _inproj(x2, norm_mix[layer].reshape(1, d), w_main, w_dt, layer,
                               cos_t, sin_t, seq, tm_proj)
        y_na = _na_attention(proj, na_tabs, layer, bsz, seq)
        y_dil = _dilated_attention(proj, bsz, seq)
        xc = _conv_silu(proj, conv_w[layer].astype(F32), conv_b[layer].reshape(1, -1).astype(F32), seq, conv_rows)
        bias_row = pad_row(dt_bias[layer])
        a_row = pad_row(-jnp.exp(a_log[layer].astype(F32)))
        y_fwd = _ssd_scan(xc, dt_raw, bias_row, a_row, e_fwd, bsz, seq, False)
        dsk_row = jnp.repeat(d_skip[layer].astype(F32), HEAD_DIM).reshape(1, SSM_W)
        y_ssd = _ssd_scan(xc, dt_raw, bias_row, a_row, e_bwd, bsz, seq, True,
                          extra=(y_fwd, proj, dsk_row, ssm_norm[layer].reshape(1, SSM_W).astype(F32)))
        j = layer // 2
        g_ffn = norm_ffn[layer].reshape(1, d).astype(F32)
        if layer % 2 == 0:
            x2, hn = _outproj(y_na, y_dil, y_ssd, x2, w_out_b, layer, g_ffn, None, tm_out, BF16)
            x2 = _ffn(hn, x2, wg_b, wu_b, wd_b, j, tm_ffn, tf)
        else:
            wr = jnp.pad(router_w[j].astype(F32), ((0, 0), (0, LANES - N_EXPERTS)))
            x2, h32, route = _outproj(y_na, y_dil, y_ssd, x2, w_out_b, layer, g_ffn, wr, tm_out, F32)
            row_tok, blk_expert, n_used, dest = _route_plan(route, t, tm_moe)
            yb = _moe_ffn(row_tok, blk_expert, n_used, h32, eg_b, eu_b, ed_b, j, tm_moe, tf)
            final = layer == depth - 1
            x2 = _combine(dest, yb, x2, route, norm_final.reshape(1, d).astype(F32), tm_comb, final)
    if depth % 2 == 1:
        raise NotImplementedError("final norm is fused into the expert combine of the last (odd) layer")
    return x2.reshape(bsz, seq, d).astype(x.dtype)
```

```python
import functools

import numpy as np
import jax
import jax.numpy as jnp
from jax import lax
from jax.experimental import pallas as pl
from jax.experimental.pallas import tpu as pltpu

F32 = jnp.float32
BF16 = jnp.bfloat16
I32 = jnp.int32

GRID_W = 64
HEAD_DIM = 64
NA_KH = 8
NA_KW = 16
N_HEADS = 8
ATT_W = N_HEADS * HEAD_DIM
DIL_BRANCHES = ((128, 1), (512, 4), (2048, 16))
DIL_QBLK = 128
DIL_HALF = 64
SSM_HEADS = 16
SSM_GROUPS = 2
SSM_STATE = 128
SSM_CONV = 5
SSM_CHUNK = 128
SSM_W = SSM_HEADS * HEAD_DIM
SSM_BC = SSM_GROUPS * SSM_STATE
SSM_CONV_CH = SSM_W + 2 * SSM_BC
N_EXPERTS = 8
ROPE_THETA = 10000.0
NORM_EPS = 1e-6
NEG = -1e30

LANES = 128
VMEM_LIMIT = 56 * 1024 * 1024
PROJ_TN = 512
MAIN_COLS = 11 * PROJ_TN
INPROJ_TN = 11 * LANES
_GB = ATT_W // LANES
Q_LANE_BLOCKS = frozenset(range(0, _GB)) | frozenset(range(3 * _GB, 4 * _GB))
ROPE_LANE_BLOCKS = frozenset(range(3 * _GB, 5 * _GB))
NA_RB = 4
INPROJ_SPLIT = 4
OUTPROJ_SPLIT = 4
MOE_FIRST_SPLIT = 2
MOE_ISSUE_STEPS = 8
SSM_STEP_CHUNKS = 8
DMA_UNROLL = 8


def _cparams(sem):
    return pltpu.CompilerParams(dimension_semantics=sem, vmem_limit_bytes=VMEM_LIMIT)


def _split3(x):
    hi = x.astype(BF16)
    r = x - hi.astype(F32)
    mid = r.astype(BF16)
    lo = (r - mid.astype(F32)).astype(BF16)
    return hi, mid, lo


def _dot01_l(m01, x):
    hi, mid, lo = _split3(x)
    d = lambda a: jnp.dot(m01, a, preferred_element_type=F32)
    return d(hi) + d(mid) + d(lo)


def _dot01_r(x, m01):
    hi, mid, lo = _split3(x)
    d = lambda a: jnp.dot(a, m01, preferred_element_type=F32)
    return d(hi) + d(mid) + d(lo)


def _silu(x):
    return x / (1.0 + jnp.exp(-x))


def _inproj_body(x_ref, g_ref, w_ref, wdt_ref, cos_ref, sin_ref, o_ref, odt_ref, xn_ref):
    j = pl.program_id(1)
    tm = x_ref.shape[0]
    nblk = INPROJ_TN // LANES
    parts = [slice(p * tm // INPROJ_SPLIT, (p + 1) * tm // INPROJ_SPLIT) for p in range(INPROJ_SPLIT)]
    lane = lax.broadcasted_iota(I32, (tm // INPROJ_SPLIT, LANES), 1)
    first = (lane % HEAD_DIM) < (HEAD_DIM // 2)

    def rotary(xs, cos, sin):
        partner = jnp.where(first, pltpu.roll(xs, LANES - HEAD_DIM // 2, 1), pltpu.roll(xs, HEAD_DIM // 2, 1))
        return xs * cos + partner * sin

    def tile(jt):
        blocks = range(jt * nblk, (jt + 1) * nblk)
        for rows in parts:
            if jt == 0:
                x = x_ref[rows, :]
                ms = jnp.mean(x * x, axis=-1, keepdims=True)
                xn = (x * lax.rsqrt(ms + NORM_EPS) * g_ref[...]).astype(BF16)
                xn_ref[rows, :] = xn
                odt_ref[rows, :] = jnp.dot(xn, wdt_ref[...], preferred_element_type=F32)
            else:
                xn = xn_ref[rows, :]
            acc = jnp.dot(xn, w_ref[...], preferred_element_type=F32)
            pieces = []
            for c, gb in enumerate(blocks):
                xs = acc[:, c * LANES:(c + 1) * LANES]
                if gb in Q_LANE_BLOCKS:
                    xs = xs * HEAD_DIM ** -0.5
                if gb in ROPE_LANE_BLOCKS:
                    xs = rotary(xs, cos_ref[rows, :], sin_ref[rows, :])
                pieces.append(xs.astype(BF16))
            o_ref[rows, :] = jnp.concatenate(pieces, axis=1)

    for jt in range(MAIN_COLS // INPROJ_TN):
        pl.when(j == jt)(functools.partial(tile, jt))


def _inproj(x2, g, w_main, w_dt, layer, cos_t, sin_t, seq, tm):
    t, d = x2.shape
    n_s = seq // tm
    return pl.pallas_call(
        _inproj_body,
        grid=(t // tm, MAIN_COLS // INPROJ_TN),
        in_specs=[
            pl.BlockSpec((tm, d), lambda i, j: (i, 0)),
            pl.BlockSpec((1, d), lambda i, j: (0, 0)),
            pl.BlockSpec((None, d, INPROJ_TN), lambda i, j: (layer, 0, j)),
            pl.BlockSpec((None, d, LANES), lambda i, j: (layer, 0, 0)),
            pl.BlockSpec((tm, LANES), lambda i, j: (i % n_s, 0)),
            pl.BlockSpec((tm, LANES), lambda i, j: (i % n_s, 0)),
        ],
        out_specs=[
            pl.BlockSpec((tm, INPROJ_TN), lambda i, j: (i, j)),
            pl.BlockSpec((tm, LANES), lambda i, j: (i, 0)),
        ],
        out_shape=[jax.ShapeDtypeStruct((t, MAIN_COLS), BF16),
                   jax.ShapeDtypeStruct((t, LANES), F32)],
        scratch_shapes=[pltpu.VMEM((tm, d), BF16)],
        compiler_params=_cparams(("arbitrary", "arbitrary")),
        name="inproj",
    )(x2, g, w_main, w_dt, cos_t, sin_t)


def _na_bias_tables(rpb, rows):
    nb = rows // NA_RB
    c = np.arange(GRID_W)
    cs = np.clip(c - NA_KW // 2, 0, GRID_W - NA_KW)
    ck = np.arange(GRID_W)
    col_ok = (ck[None, :] >= cs[:, None]) & (ck[None, :] < cs[:, None] + NA_KW)
    dc = np.clip(ck[None, :] - c[:, None] + NA_KW - 1, 0, 2 * NA_KW - 2)
    onehot = (dc.reshape(-1)[None, :] == np.arange(2 * NA_KW - 1)[:, None]).astype(np.float32)
    dr = np.clip(np.arange(3 * NA_RB)[None, :] - NA_RB - np.arange(NA_RB)[:, None] + NA_KH - 1, 0, 2 * NA_KH - 2)
    by_row = rpb.astype(F32)[:, :, dr, :]
    full = jnp.einsum('lhqkd,dx->lhqkx', by_row, jnp.asarray(onehot), precision=lax.Precision.HIGHEST)
    full = full.reshape(full.shape[:4] + (GRID_W, GRID_W))
    full = jnp.transpose(full, (0, 1, 2, 4, 3, 5))
    full = full.reshape(full.shape[:2] + (NA_RB * GRID_W, 3 * NA_RB * GRID_W))
    tabs = []
    for i in (0, 1, nb - 1):
        r = NA_RB * i + np.arange(NA_RB)
        rs = np.clip(r - NA_KH // 2, 0, rows - NA_KH)
        absrow = NA_RB * (i - 1) + np.arange(3 * NA_RB)
        row_ok = (absrow[None, :] >= rs[:, None]) & (absrow[None, :] < rs[:, None] + NA_KH)
        ok = row_ok[:, None, :, None] & col_ok[None, :, None, :]
        tabs.append(jnp.where(ok.reshape(full.shape[2:])[None, None], full, NEG))
    return jnp.stack(tabs, axis=1)


def _na_body(q_ref, kp_ref, kc_ref, kn_ref, vp_ref, vc_ref, vn_ref, b_ref, o_ref):
    for h in range(N_HEADS):
        sl = slice(h * HEAD_DIM, (h + 1) * HEAD_DIM)
        q = q_ref[:, sl]
        k = jnp.concatenate([kp_ref[:, sl], kc_ref[:, sl], kn_ref[:, sl]], axis=0)
        v = jnp.concatenate([vp_ref[:, sl], vc_ref[:, sl], vn_ref[:, sl]], axis=0)
        s = lax.dot_general(q, k, (((1,), (1,)), ((), ())), preferred_element_type=F32)
        s = s + b_ref[h]
        m = jnp.max(s, axis=-1, keepdims=True)
        p = jnp.exp(s - m)
        l = jnp.sum(p, axis=-1, keepdims=True)
        o = jnp.dot(p.astype(BF16), v, preferred_element_type=F32) / l
        o_ref[:, sl] = o.astype(BF16)


def _na_attention(proj, bias_tabs, layer, bsz, seq):
    t = bsz * seq
    tq = NA_RB * GRID_W
    nb = seq // tq
    assert nb >= 3 and seq // GRID_W >= NA_KH
    sel = lambda i: jnp.where(i == 0, 0, jnp.where(i == nb - 1, 2, 1))
    prev = lambda b, i: b * nb + jnp.maximum(i - 1, 0)
    cur = lambda b, i: b * nb + i
    nxt = lambda b, i: b * nb + jnp.minimum(i + 1, nb - 1)
    blk = lambda rowf, col: pl.BlockSpec((tq, ATT_W), lambda b, i: (rowf(b, i), col))
    return pl.pallas_call(
        _na_body,
        grid=(bsz, nb),
        in_specs=[blk(cur, 0), blk(prev, 1), blk(cur, 1), blk(nxt, 1),
                  blk(prev, 2), blk(cur, 2), blk(nxt, 2),
                  pl.BlockSpec((None, None, N_HEADS, tq, 3 * tq), lambda b, i: (layer, sel(i), 0, 0, 0))],
        out_specs=pl.BlockSpec((tq, ATT_W), lambda b, i: (cur(b, i), 0)),
        out_shape=jax.ShapeDtypeStruct((t, ATT_W), BF16),
        compiler_params=_cparams(("arbitrary", "arbitrary")),
        name="na_attn",
    )(proj, proj, proj, proj, proj, proj, proj, bias_tabs)


DIL_SUPER = 2048
DIL_STEP_Q = 1024


def _dil_split_body(x_ref, *rest):
    o_refs, scr = rest[:-1], rest[-1]
    for c in range(ATT_W // LANES):
        scr[c] = x_ref[:, c * LANES:(c + 1) * LANES].astype(F32)
    for o_ref in o_refs:
        dil = o_ref.shape[2]
        n = o_ref.shape[3]
        for rho in range(dil):
            for c in range(ATT_W // LANES):
                o_ref[0, 0, rho, :, c * LANES:(c + 1) * LANES] = (
                    scr[c, pl.ds(rho, n, stride=dil), :].astype(BF16))


def _dil_split(proj, bsz, seq, dils):
    t = bsz * seq
    nsb = seq // DIL_SUPER
    out_specs = [pl.BlockSpec((1, 1, dil, DIL_SUPER // dil, ATT_W),
                              lambda i, c: (c, i // nsb, 0, i % nsb, 0)) for dil in dils]
    out_shape = [jax.ShapeDtypeStruct((3, bsz, dil, seq // dil, ATT_W), BF16) for dil in dils]
    return pl.pallas_call(
        _dil_split_body,
        grid=(t // DIL_SUPER, 3),
        in_specs=[pl.BlockSpec((DIL_SUPER, ATT_W), lambda i, c: (i, 3 + c))],
        out_specs=out_specs, out_shape=out_shape,
        scratch_shapes=[pltpu.VMEM((ATT_W // LANES, DIL_SUPER, LANES), F32)],
        compiler_params=_cparams(("arbitrary", "arbitrary")),
        name="dil_split",
    )(proj)


def _dil_body(q_ref, kp_ref, kc_ref, kn_ref, vp_ref, vc_ref, vn_ref, o_ref, l_ref, *, n_sub):
    u = pl.program_id(2)
    qb = q_ref.shape[0]
    nsb = qb // DIL_QBLK
    nk = 2 * DIL_QBLK
    a = lax.broadcasted_iota(I32, (DIL_QBLK, nk), 0)
    kk = lax.broadcasted_iota(I32, (DIL_QBLK, nk), 1)
    band = (kk >= a) & (kk <= a + 2 * DIL_HALF)
    oks = []
    for j in range(nsb):
        uk = u * qb + j * DIL_QBLK - DIL_HALF + kk
        oks.append(band & (uk >= 0) & (uk < n_sub))
    ok = jnp.concatenate(oks, axis=0) if nsb > 1 else oks[0]
    lane = lax.broadcasted_iota(I32, (qb, LANES), 1)
    lse_tile = jnp.zeros((qb, LANES), F32)
    dn = (((1,), (1,)), ((), ()))
    for h in range(N_HEADS):
        sl = slice(h * HEAD_DIM, (h + 1) * HEAD_DIM)
        k = jnp.concatenate([kp_ref[:, sl], kc_ref[:, sl], kn_ref[:, sl]], axis=0)
        v = jnp.concatenate([vp_ref[:, sl], vc_ref[:, sl], vn_ref[:, sl]], axis=0)
        parts = [lax.dot_general(q_ref[j * DIL_QBLK:(j + 1) * DIL_QBLK, sl], k[j * DIL_QBLK:j * DIL_QBLK + nk],
                                 dn, preferred_element_type=F32) for j in range(nsb)]
        s = jnp.concatenate(parts, axis=0) if nsb > 1 else parts[0]
        s = jnp.where(ok, s, NEG)
        m = jnp.max(s, axis=-1, keepdims=True)
        p = jnp.exp(s - m)
        l = jnp.sum(p, axis=-1, keepdims=True)
        pb = p.astype(BF16)
        outs = [jnp.dot(pb[j * DIL_QBLK:(j + 1) * DIL_QBLK], v[j * DIL_QBLK:j * DIL_QBLK + nk],
                        preferred_element_type=F32) for j in range(nsb)]
        o = jnp.concatenate(outs, axis=0) if nsb > 1 else outs[0]
        o_ref[:, sl] = (o / l).astype(BF16)
        lse_tile = jnp.where(lane == h, m + jnp.log(l), lse_tile)
    l_ref[...] = lse_tile


def _dil_branch(src, bsz, seq, dil):
    n_sub = seq // dil
    qb = min(DIL_STEP_Q, n_sub)
    nu = n_sub // qb
    hpb = qb // DIL_HALF
    nhalo = n_sub // DIL_HALF
    prev = lambda u: jnp.maximum(u * hpb - 1, 0)
    nxt = lambda u: jnp.minimum((u + 1) * hpb, nhalo - 1)
    if dil == 1:
        cur = lambda c: pl.BlockSpec((qb, ATT_W), lambda b, r, u: (b * nu + u, 3 + c))
        halo = lambda hf, c: pl.BlockSpec((DIL_HALF, ATT_W), lambda b, r, u: (b * nhalo + hf(u), 3 + c))
    else:
        cur = lambda c: pl.BlockSpec((None, None, None, qb, ATT_W), lambda b, r, u: (c, b, r, u, 0))
        halo = lambda hf, c: pl.BlockSpec((None, None, None, DIL_HALF, ATT_W),
                                          lambda b, r, u: (c, b, r, hf(u), 0))
    out_blk = lambda w: pl.BlockSpec((None, None, qb, w), lambda b, r, u: (b, r, u, 0))
    return pl.pallas_call(
        functools.partial(_dil_body, n_sub=n_sub),
        grid=(bsz, dil, nu),
        in_specs=[cur(0), halo(prev, 1), cur(1), halo(nxt, 1), halo(prev, 2), cur(2), halo(nxt, 2)],
        out_specs=[out_blk(ATT_W), out_blk(LANES)],
        out_shape=[jax.ShapeDtypeStruct((bsz, dil, n_sub, ATT_W), BF16),
                   jax.ShapeDtypeStruct((bsz, dil, n_sub, LANES), F32)],
        compiler_params=_cparams(("arbitrary", "arbitrary", "arbitrary")),
        name=f"dil_attn_d{dil}",
    )(*([src] * 7))


def _dil_merge_body(*refs, dils):
    nbr = len(dils)
    o_refs, l_refs = refs[:nbr], refs[nbr:2 * nbr]
    e_ref, out_ref = refs[2 * nbr], refs[2 * nbr + 1]
    scr = refs[2 * nbr + 2:]
    outs, lses = [], []
    si = 0
    for n, dil in enumerate(dils):
        if dil == 1:
            outs.append(o_refs[n][0, 0].astype(F32))
            lses.append(l_refs[n][0, 0])
            continue
        so, sl = scr[si], scr[si + 1]
        si += 2
        rows = DIL_SUPER // dil
        for rho in range(dil):
            for c in range(ATT_W // LANES):
                so[c, pl.ds(rho, rows, stride=dil), :] = (
                    o_refs[n][0, rho, :, c * LANES:(c + 1) * LANES].astype(F32))
            sl[pl.ds(rho, rows, stride=dil), :] = l_refs[n][0, rho]
        outs.append(jnp.concatenate([so[c] for c in range(ATT_W // LANES)], axis=1))
        lses.append(sl[...])
    m = functools.reduce(jnp.maximum, lses)
    es = [jnp.exp(l - m) for l in lses]
    tot = functools.reduce(lambda x, y: x + y, es)
    acc = None
    for o, e in zip(outs, es):
        w = jnp.dot((e / tot).astype(BF16), e_ref[...], preferred_element_type=F32)
        acc = w * o if acc is None else acc + w * o
    out_ref[...] = acc.astype(BF16)


def _dil_merge(outs, lses, bsz, seq, dils):
    t = bsz * seq
    nsb = seq // DIL_SUPER
    spec = lambda dil, w: pl.BlockSpec((1, dil, DIL_SUPER // dil, w), lambda i: (i // nsb, 0, i % nsb, 0))
    e8 = np.zeros((LANES, ATT_W), np.float32)
    for h in range(N_HEADS):
        e8[h, h * HEAD_DIM:(h + 1) * HEAD_DIM] = 1.0
    scratch = []
    for dil in dils:
        if dil != 1:
            scratch += [pltpu.VMEM((ATT_W // LANES, DIL_SUPER, LANES), F32), pltpu.VMEM((DIL_SUPER, LANES), F32)]
    return pl.pallas_call(
        functools.partial(_dil_merge_body, dils=dils),
        grid=(t // DIL_SUPER,),
        in_specs=[spec(dil, ATT_W) for dil in dils] + [spec(dil, LANES) for dil in dils]
                 + [pl.BlockSpec((LANES, ATT_W), lambda i: (0, 0))],
        out_specs=pl.BlockSpec((DIL_SUPER, ATT_W), lambda i: (i, 0)),
        out_shape=jax.ShapeDtypeStruct((t, ATT_W), BF16),
        scratch_shapes=scratch,
        compiler_params=_cparams(("arbitrary",)),
        name="dil_merge",
    )(*outs, *lses, jnp.asarray(e8, BF16))


def _dilated_attention(proj, bsz, seq):
    dils = tuple(dil for _, dil in DIL_BRANCHES)
    for win, dil in DIL_BRANCHES:
        assert win // 2 // dil == DIL_HALF and DIL_SUPER % (dil * DIL_QBLK) == 0
    assert seq % DIL_SUPER == 0 and dils[0] == 1
    split = _dil_split(proj, bsz, seq, dils[1:])
    outs, lses = [], []
    for n, dil in enumerate(dils):
        o, l = _dil_branch(proj if dil == 1 else split[n - 1], bsz, seq, dil)
        outs.append(o)
        lses.append(l)
    return _dil_merge(outs, lses, bsz, seq, dils)


CONV_HALO = 16


def _conv_body(xp_ref, xc_ref, xn_ref, w_ref, b_ref, o_ref, *, blocks_per_seq):
    i = pl.program_id(0)
    r = xc_ref.shape[0]
    pos = i % blocks_per_seq
    keep_p = jnp.where(pos == 0, 0.0, 1.0).astype(F32)
    keep_n = jnp.where(pos == blocks_per_seq - 1, 0.0, 1.0).astype(F32)
    ext = jnp.concatenate([xp_ref[...].astype(F32) * keep_p, xc_ref[...].astype(F32),
                           xn_ref[...].astype(F32) * keep_n], axis=0)
    w = w_ref[...]
    half = SSM_CONV // 2
    y = b_ref[...] + jnp.zeros((r, w.shape[1]), F32)
    for k in range(SSM_CONV):
        st = CONV_HALO + k - half
        y = y + ext[st:st + r, :] * w[k:k + 1, :]
    o_ref[...] = _silu(y).astype(BF16)


def _conv_silu(proj, conv_w, conv_b, seq, rblk):
    t = proj.shape[0]
    ncol = SSM_CONV_CH // PROJ_TN
    col0 = (MAIN_COLS - SSM_CONV_CH) // PROJ_TN
    rh = rblk // CONV_HALO
    nh = t // CONV_HALO
    return pl.pallas_call(
        functools.partial(_conv_body, blocks_per_seq=seq // rblk),
        grid=(t // rblk, ncol),
        in_specs=[
            pl.BlockSpec((CONV_HALO, PROJ_TN), lambda i, c: (jnp.maximum(i * rh - 1, 0), col0 + c)),
            pl.BlockSpec((rblk, PROJ_TN), lambda i, c: (i, col0 + c)),
            pl.BlockSpec((CONV_HALO, PROJ_TN), lambda i, c: (jnp.minimum((i + 1) * rh, nh - 1), col0 + c)),
            pl.BlockSpec((SSM_CONV, PROJ_TN), lambda i, c: (0, c)),
            pl.BlockSpec((1, PROJ_TN), lambda i, c: (0, c)),
        ],
        out_specs=pl.BlockSpec((rblk, PROJ_TN), lambda i, c: (i, c)),
        out_shape=jax.ShapeDtypeStruct((t, SSM_CONV_CH), BF16),
        compiler_params=_cparams(("arbitrary", "arbitrary")),
        name="ssd_conv",
    )(proj, proj, proj, conv_w, conv_b)


def _ssd_body(*refs, reverse, final):
    if final:
        (x_ref, b_ref, c_ref, dt_ref, bias_ref, a_ref, e_ref,
         yf_ref, z_ref, dsk_ref, ng_ref, o_ref, state_ref, y_ref) = refs
    else:
        x_ref, b_ref, c_ref, dt_ref, bias_ref, a_ref, e_ref, o_ref, state_ref, y_ref = refs
    q = SSM_CHUNK
    gw = SSM_W // SSM_GROUPS
    hpg = SSM_HEADS // SSM_GROUPS
    off = SSM_HEADS if reverse else 0
    end = 0 if reverse else q - 1

    @pl.when(pl.program_id(1) == 0)
    def _():
        state_ref[...] = jnp.zeros_like(state_ref)

    row = lax.broadcasted_iota(I32, (q, q), 0)
    col = lax.broadcasted_iota(I32, (q, q), 1)
    tri = (col >= row) if reverse else (row >= col)
    tri01 = jnp.where(tri, 1.0, 0.0).astype(BF16)
    lane_lo = lax.broadcasted_iota(I32, (q, 2 * HEAD_DIM), 1) < HEAD_DIM
    n_chunks = x_ref.shape[0] // q
    for k in (range(n_chunks - 1, -1, -1) if reverse else range(n_chunks)):
        rows = slice(k * q, (k + 1) * q)
        xr = dt_ref[rows, :] + bias_ref[...]
        dt = jnp.maximum(xr, 0.0) + jnp.log(1.0 + jnp.exp(-jnp.abs(xr)))
        dta = dt * a_ref[...]
        acs = _dot01_l(tri01, dta)
        acs_t = acs.T
        acs_end = acs[end:end + 1, :]
        wide = jnp.dot(jnp.concatenate([dt, jnp.exp(acs), jnp.exp(acs_end - acs)], axis=0).astype(BF16),
                       e_ref[...], preferred_element_type=F32)
        dt_x, e_acs, ds_x = wide[:q], wide[q:2 * q], wide[2 * q:]
        cd_x = _dot01_r(jnp.broadcast_to(jnp.exp(acs_end), (8, LANES)), e_ref[...])[0:1, :]
        xs = x_ref[rows, :].astype(F32)
        xdt = xs * dt_x
        xds = (xdt * ds_x).astype(BF16)
        xdt = xdt.astype(BF16)

        for g in range(SSM_GROUPS):
            gs = slice(g * gw, (g + 1) * gw)
            bg = b_ref[rows, g * SSM_STATE:(g + 1) * SSM_STATE]
            cg = c_ref[rows, g * SSM_STATE:(g + 1) * SSM_STATE]
            cb = lax.dot_general(cg, bg, (((1,), (1,)), ((), ())), preferred_element_type=F32)
            for r in range(0, hpg, 2):
                hh = g * hpg + r
                ms = []
                for h2 in (hh, hh + 1):
                    diff = acs[:, off + h2:off + h2 + 1] - acs_t[off + h2:off + h2 + 1, :]
                    ms.append((cb * jnp.exp(jnp.where(tri, diff, -jnp.inf))).astype(BF16))
                ps = slice(hh * HEAD_DIM, (hh + 2) * HEAD_DIM)
                xp = xdt[:, ps]
                zero = jnp.zeros_like(xp)
                rhs = jnp.concatenate([jnp.where(lane_lo, xp, zero), jnp.where(lane_lo, zero, xp)], axis=0)
                y_ref[rows, ps] = jnp.dot(jnp.concatenate(ms, axis=1), rhs, preferred_element_type=F32)
            prev = state_ref[g]
            y_off = jnp.dot(cg, prev.astype(BF16), preferred_element_type=F32) * e_acs[:, gs]
            y_ref[rows, gs] = y_ref[rows, gs] + y_off
            bt = bg.astype(F32).T.astype(BF16)
            st = jnp.dot(bt, xds[:, gs], preferred_element_type=F32)
            state_ref[g] = cd_x[:, gs] * prev + st

        if final:
            y = yf_ref[rows, :] + y_ref[rows, :] + dsk_ref[...] * xs
            y = y * _silu(z_ref[rows, :].astype(F32))
            for g in range(SSM_GROUPS):
                gs = slice(g * gw, (g + 1) * gw)
                yg = y[:, gs]
                yg = yg * lax.rsqrt(jnp.mean(yg * yg, axis=-1, keepdims=True) + NORM_EPS)
                o_ref[rows, gs] = (yg * ng_ref[:, gs]).astype(o_ref.dtype)
        else:
            o_ref[rows, :] = y_ref[rows, :]


def _ssd_scan(xc, dt_raw, bias_row, a_row, e_mat, bsz, seq, reverse, extra=None):
    t = bsz * seq
    q = SSM_CHUNK * SSM_STEP_CHUNKS
    assert seq % q == 0
    nc = seq // q
    final = extra is not None
    rowf =(lambda b, c: b * nc + (nc - 1 - c)) if reverse else (lambda b, c: b * nc + c)
    const = lambda w: pl.BlockSpec((1, w), lambda b, c: (0, 0))
    in_specs = [
        pl.BlockSpec((q, SSM_W), lambda b, c: (rowf(b, c), 0)),
        pl.BlockSpec((q, SSM_BC), lambda b, c: (rowf(b, c), SSM_W // SSM_BC)),
        pl.BlockSpec((q, SSM_BC), lambda b, c: (rowf(b, c), SSM_W // SSM_BC + 1)),
        pl.BlockSpec((q, LANES), lambda b, c: (rowf(b, c), 0)),
        const(LANES), const(LANES),
        pl.BlockSpec((LANES, SSM_W), lambda b, c: (0, 0)),
    ]
    args = [xc, xc, xc, dt_raw, bias_row, a_row, e_mat]
    if final:
        y_fwd, proj, dsk_row, ng_row = extra
        zcol = (3 * ATT_W * 2) // SSM_W
        in_specs += [pl.BlockSpec((q, SSM_W), lambda b, c: (rowf(b, c), 0)),
                     pl.BlockSpec((q, SSM_W), lambda b, c: (rowf(b, c), zcol)),
                     const(SSM_W), const(SSM_W)]
        args += [y_fwd, proj, dsk_row, ng_row]
    return pl.pallas_call(
        functools.partial(_ssd_body, reverse=reverse, final=final),
        grid=(bsz, nc),
        in_specs=in_specs,
        out_specs=pl.BlockSpec((q, SSM_W), lambda b, c: (rowf(b, c), 0)),
        out_shape=jax.ShapeDtypeStruct((t, SSM_W), BF16 if final else F32),
        scratch_shapes=[pltpu.VMEM((SSM_GROUPS, SSM_STATE, SSM_W // SSM_GROUPS), F32),
                        pltpu.VMEM((q, SSM_W), F32)],
        compiler_params=_cparams(("arbitrary", "arbitrary")),
        name="ssd_bwd" if reverse else "ssd_fwd",
    )(*args)


def _expand_matrix(reverse):
    e = np.zeros((LANES, SSM_W), np.float32)
    off = SSM_HEADS if reverse else 0
    for h in range(SSM_HEADS):
        e[off + h, h * HEAD_DIM:(h + 1) * HEAD_DIM] = 1.0
    return jnp.asarray(e, BF16)


def _outproj_body(*refs, router):
    if router:
        yna_ref, ydil_ref, yssd_ref, x_ref, w_ref, g_ref, wr_ref, xo_ref, hn_ref, rt_ref = refs
    else:
        yna_ref, ydil_ref, yssd_ref, x_ref, w_ref, g_ref, xo_ref, hn_ref = refs
    tm = x_ref.shape[0]
    n_parts = OUTPROJ_SPLIT if router else 1
    for part in range(n_parts):
        rows = slice(part * tm // n_parts, (part + 1) * tm // n_parts)
        acc = x_ref[rows, :]
        acc = acc + jnp.dot(yna_ref[rows, :], w_ref[0:ATT_W, :], preferred_element_type=F32)
        acc = acc + jnp.dot(ydil_ref[rows, :], w_ref[ATT_W:2 * ATT_W, :], preferred_element_type=F32)
        acc = acc + jnp.dot(yssd_ref[rows, :], w_ref[2 * ATT_W:, :], preferred_element_type=F32)
        xo_ref[rows, :] = acc
        ms = jnp.mean(acc * acc, axis=-1, keepdims=True)
        h = acc * lax.rsqrt(ms + NORM_EPS) * g_ref[...]
        hn_ref[rows, :] = h.astype(hn_ref.dtype)
        if router:
            h_hi = h.astype(BF16)
            h_lo = (h - h_hi.astype(F32)).astype(BF16)
            wr = wr_ref[...]
            w_hi = wr.astype(BF16)
            w_lo = (wr - w_hi.astype(F32)).astype(BF16)
            d = lambda a, b: jnp.dot(a, b, preferred_element_type=F32)
            logits = d(h_hi, w_hi) + d(h_hi, w_lo) + d(h_lo, w_hi)
            lane = lax.broadcasted_iota(I32, logits.shape, 1)
            logits = jnp.where(lane < N_EXPERTS, logits, -jnp.inf)
            m1 = jnp.max(logits, axis=-1, keepdims=True)
            i1 = jnp.min(jnp.where(logits == m1, lane, LANES), axis=-1, keepdims=True)
            rest = jnp.where(lane == i1, -jnp.inf, logits)
            m2 = jnp.max(rest, axis=-1, keepdims=True)
            i2 = jnp.min(jnp.where(rest == m2, lane, LANES), axis=-1, keepdims=True)
            e = jnp.exp(m2 - m1)
            g1 = 1.0 / (1.0 + e)
            g2 = e / (1.0 + e)
            rt = jnp.where(lane == 0, i1.astype(F32),
                           jnp.where(lane == 1, i2.astype(F32),
                                     jnp.where(lane == 2, g1, jnp.where(lane == 3, g2, 0.0))))
            rt_ref[rows, :] = rt


def _outproj(y_na, y_dil, y_ssd, x2, w_out, layer, g, w_router, tm, hn_dtype):
    t, d = x2.shape
    router = w_router is not None
    row = lambda w: pl.BlockSpec((tm, w), lambda i: (i, 0))
    in_specs = [row(ATT_W), row(ATT_W), row(SSM_W), row(d),
                pl.BlockSpec((None, d, d), lambda i: (layer, 0, 0)),
                pl.BlockSpec((1, d), lambda i: (0, 0))]
    args = [y_na, y_dil, y_ssd, x2, w_out, g]
    out_specs = [row(d), row(d)]
    out_shape = [jax.ShapeDtypeStruct((t, d), F32), jax.ShapeDtypeStruct((t, d), hn_dtype)]
    if router:
        in_specs.append(pl.BlockSpec((d, LANES), lambda i: (0, 0)))
        args.append(w_router)
        out_specs.append(row(LANES))
        out_shape.append(jax.ShapeDtypeStruct((t, LANES), F32))
    return pl.pallas_call(
        functools.partial(_outproj_body, router=router),
        grid=(t // tm,),
        in_specs=in_specs, out_specs=out_specs, out_shape=out_shape,
        compiler_params=_cparams(("arbitrary",)),
        name="outproj_router" if router else "outproj",
    )(*args)


def _ffn_body(hn_ref, x_ref, wg_ref, wu_ref, wd_ref, o_ref, acc_ref):
    f = pl.program_id(1)
    last = pl.num_programs(1) - 1

    def swiglu_down():
        h = hn_ref[...]
        g = jnp.dot(h, wg_ref[...], preferred_element_type=F32)
        u = jnp.dot(h, wu_ref[...], preferred_element_type=F32)
        return jnp.dot((_silu(g) * u).astype(BF16), wd_ref[...], preferred_element_type=F32)

    @pl.when(f == 0)
    def _():
        acc_ref[...] = x_ref[...] + swiglu_down()

    @pl.when((f != 0) & (f != last))
    def _():
        acc_ref[...] += swiglu_down()

    @pl.when(f == last)
    def _():
        o_ref[...] = acc_ref[...] + swiglu_down()


def _ffn(hn, x2, wg, wu, wd, j, tm, tf):
    t, d = x2.shape
    fdim = wg.shape[2]
    return pl.pallas_call(
        _ffn_body,
        grid=(t // tm, fdim // tf),
        in_specs=[pl.BlockSpec((tm, d), lambda i, f: (i, 0)),
                  pl.BlockSpec((tm, d), lambda i, f: (i, 0)),
                  pl.BlockSpec((None, d, tf), lambda i, f: (j, 0, f)),
                  pl.BlockSpec((None, d, tf), lambda i, f: (j, 0, f)),
                  pl.BlockSpec((None, tf, d), lambda i, f: (j, f, 0))],
        out_specs=pl.BlockSpec((tm, d), lambda i, f: (i, 0)),
        out_shape=jax.ShapeDtypeStruct((t, d), F32),
        scratch_shapes=[pltpu.VMEM((tm, d), F32)],
        compiler_params=_cparams(("arbitrary", "arbitrary")),
        name="ffn",
    )(hn, x2, wg, wu, wd)


def _moe_row_copy(h_hbm, xg_ref, sem, tok, slot, r):
    return pltpu.make_async_copy(h_hbm.at[pl.ds(tok, 1), :], xg_ref.at[slot, pl.ds(r, 1), :], sem.at[slot])


def _moe_ffn_body(tok_ref, be_ref, nu_ref, h_hbm, wg_ref, wu_ref, wd_ref, o_ref, xg_ref, xb_ref, acc_ref, sem):
    i = pl.program_id(0)
    f = pl.program_id(1)
    tm = xb_ref.shape[0]
    n_used = nu_ref[0]
    active = i < n_used
    slot = i % 2

    def start_gather(blk, dst_slot):
        def issue(r8, carry):
            for k in range(DMA_UNROLL):
                r = r8 * DMA_UNROLL + k
                _moe_row_copy(h_hbm, xg_ref, sem, tok_ref[blk * tm + r], dst_slot, r).start()
            return carry
        lax.fori_loop(0, tm // DMA_UNROLL, issue, 0)

    @pl.when((f == 0) & (i == 0))
    def _():
        start_gather(0, 0)

    nf = pl.num_programs(1)
    share = tm // MOE_ISSUE_STEPS
    first = active & (f == 0)
    last = active & (f == nf - 1)
    fetch = active & (f >= 1) & (f <= MOE_ISSUE_STEPS) & (i + 1 < n_used)

    def swiglu_down(h):
        g = jnp.dot(h, wg_ref[...], preferred_element_type=F32)
        u = jnp.dot(h, wu_ref[...], preferred_element_type=F32)
        return jnp.dot((_silu(g) * u).astype(BF16), wd_ref[...], preferred_element_type=F32)

    @pl.when(first)
    def _():
        pltpu.make_async_copy(h_hbm.at[pl.ds(0, tm), :], xg_ref.at[slot], sem.at[slot]).wait()
        for p in range(MOE_FIRST_SPLIT):
            rows = slice(p * tm // MOE_FIRST_SPLIT, (p + 1) * tm // MOE_FIRST_SPLIT)
            h = xg_ref[slot, rows, :].astype(BF16)
            xb_ref[rows, :] = h
            acc_ref[rows, :] = swiglu_down(h)

    @pl.when(fetch)
    def _():
        acc_ref[...] += swiglu_down(xb_ref[...])
        base = (f - 1) * share
        for k in range(share):
            r = base + k
            _moe_row_copy(h_hbm, xg_ref, sem, tok_ref[(i + 1) * tm + r], 1 - slot, r).start()

    @pl.when(active & jnp.logical_not(first | last | fetch))
    def _():
        acc_ref[...] += swiglu_down(xb_ref[...])

    @pl.when(last)
    def _():
        o_ref[...] = acc_ref[...] + swiglu_down(xb_ref[...])

    @pl.when(jnp.logical_not(active) & (f == 0))
    def _():
        o_ref[...] = jnp.zeros_like(o_ref)


def _moe_ffn(row_tok, blk_expert, n_used, h32, wg, wu, wd, j, tm, tf):
    n_rows = row_tok.shape[0]
    d = h32.shape[1]
    fdim = wg.shape[3]
    nf = fdim // tf
    assert nf > MOE_ISSUE_STEPS and tm % MOE_ISSUE_STEPS == 0 and tm % DMA_UNROLL == 0
    fidx = lambda i, f, nu: jnp.where(i < nu[0], f, nf - 1)
    grid_spec = pltpu.PrefetchScalarGridSpec(
        num_scalar_prefetch=3,
        grid=(n_rows // tm, nf),
        in_specs=[pl.BlockSpec(memory_space=pl.ANY),
                  pl.BlockSpec((None, None, d, tf), lambda i, f, tok, be, nu: (j, be[i], 0, fidx(i, f, nu))),
                  pl.BlockSpec((None, None, d, tf), lambda i, f, tok, be, nu: (j, be[i], 0, fidx(i, f, nu))),
                  pl.BlockSpec((None, None, tf, d), lambda i, f, tok, be, nu: (j, be[i], fidx(i, f, nu), 0))],
        out_specs=pl.BlockSpec((tm, d), lambda i, f, tok, be, nu: (i, 0)),
        scratch_shapes=[pltpu.VMEM((2, tm, d), F32), pltpu.VMEM((tm, d), BF16),
                        pltpu.VMEM((tm, d), F32), pltpu.SemaphoreType.DMA((2,))],
    )
    return pl.pallas_call(
        _moe_ffn_body,
        grid_spec=grid_spec,
        out_shape=jax.ShapeDtypeStruct((n_rows, d), F32),
        compiler_params=_cparams(("arbitrary", "arbitrary")),
        name="moe_ffn",
    )(row_tok, blk_expert, n_used, h32, wg, wu, wd)


def _combine_row_copy(y_hbm, buf_ref, sem, src, slot, k, r):
    return pltpu.make_async_copy(y_hbm.at[pl.ds(src, 1), :], buf_ref.at[slot, k, pl.ds(r, 1), :], sem.at[slot])


def _combine_body(pos_ref, y_hbm, x_ref, rt_ref, gf_ref, o_ref, buf_ref, sem, *, final):
    i = pl.program_id(0)
    tm = x_ref.shape[0]
    slot = i % 2

    def start_gather(step, dst_slot):
        def issue(r4, carry):
            for u in range(DMA_UNROLL // 2):
                r = r4 * (DMA_UNROLL // 2) + u
                for k in range(2):
                    _combine_row_copy(y_hbm, buf_ref, sem, pos_ref[2 * (step * tm + r) + k], dst_slot, k, r).start()
            return carry
        lax.fori_loop(0, tm // (DMA_UNROLL // 2), issue, 0)

    @pl.when(i == 0)
    def _():
        start_gather(0, 0)

    @pl.when(i + 1 < pl.num_programs(0))
    def _():
        start_gather(i + 1, 1 - slot)

    for k in range(2):
        pltpu.make_async_copy(y_hbm.at[pl.ds(0, tm), :], buf_ref.at[slot, k], sem.at[slot]).wait()

    rt = rt_ref[...]
    out = x_ref[...] + rt[:, 2:3] * buf_ref[slot, 0] + rt[:, 3:4] * buf_ref[slot, 1]
    if final:
        ms = jnp.mean(out * out, axis=-1, keepdims=True)
        out = out * lax.rsqrt(ms + NORM_EPS) * gf_ref[...]
    o_ref[...] = out


def _combine(pos_flat, yb, x2, route, g_final, tm, final):
    t, d = x2.shape
    grid_spec = pltpu.PrefetchScalarGridSpec(
        num_scalar_prefetch=1,
        grid=(t // tm,),
        in_specs=[pl.BlockSpec(memory_space=pl.ANY),
                  pl.BlockSpec((tm, d), lambda i, pos: (i, 0)),
                  pl.BlockSpec((tm, LANES), lambda i, pos: (i, 0)),
                  pl.BlockSpec((1, d), lambda i, pos: (0, 0))],
        out_specs=pl.BlockSpec((tm, d), lambda i, pos: (i, 0)),
        scratch_shapes=[pltpu.VMEM((2, 2, tm, d), F32), pltpu.SemaphoreType.DMA((2,))],
    )
    return pl.pallas_call(
        functools.partial(_combine_body, final=final),
        grid_spec=grid_spec,
        out_shape=jax.ShapeDtypeStruct((t, d), F32),
        compiler_params=_cparams(("arbitrary",)),
        name="moe_combine",
    )(pos_flat, yb, x2, route, g_final)


def _route_plan(route, t, tm):
    e_flat = route[:, 0:2].astype(I32).reshape(-1)
    n_assign = e_flat.shape[0]
    onehot = (e_flat[:, None] == jnp.arange(N_EXPERTS, dtype=I32)[None, :]).astype(I32)
    csum = jnp.cumsum(onehot, axis=0)
    rank = jnp.sum(onehot * (csum - 1), axis=1)
    counts = csum[-1]
    padded = ((counts + tm - 1) // tm) * tm
    pend = jnp.cumsum(padded)
    pstart = pend - padded
    dest = (pstart[e_flat] + rank).astype(I32)
    n_rows = n_assign + N_EXPERTS * tm
    row_tok = jnp.zeros((n_rows,), I32).at[dest].set(jnp.arange(n_assign, dtype=I32) // 2)
    blk_start = jnp.arange(n_rows // tm, dtype=I32) * tm
    blk_expert = jnp.minimum(jnp.searchsorted(pend, blk_start, side='right'), N_EXPERTS - 1).astype(I32)
    n_used = (pend[-1:] // tm).astype(I32)
    return row_tok, blk_expert, n_used, dest


def _rope_tables(seq):
    half = HEAD_DIM // 2
    inv = ROPE_THETA ** (-jnp.arange(0, HEAD_DIM, 2, dtype=F32) / HEAD_DIM)
    ang = jnp.arange(seq, dtype=F32)[:, None] * inv[None, :]
    cos, sin = jnp.cos(ang), jnp.sin(ang)
    reps = LANES // HEAD_DIM
    cos_t = jnp.concatenate([cos, cos] * reps, axis=1)
    sin_t = jnp.concatenate([-sin, sin] * reps, axis=1)
    del half
    return cos_t, sin_t


def _pick_tile(n, pref):
    while n % pref:
        pref //= 2
    return pref


def kernel(x, norm_mix, w_in, na_rpb, conv_w, conv_b, dt_bias, a_log, d_skip, ssm_norm, w_out, norm_ffn,
           ffn_w_gate, ffn_w_up, ffn_w_down, router_w, exp_w_gate, exp_w_up, exp_w_down, norm_final):
    bsz, seq, d = x.shape
    t = bsz * seq
    depth = w_in.shape[0]
    assert w_in.shape[2] == MAIN_COLS + 2 * SSM_HEADS and seq % (GRID_W * NA_RB) == 0
    x2 = x.reshape(t, d).astype(F32)

    tm_proj = _pick_tile(seq, 1024)
    tm_out = _pick_tile(t, 512)
    tm_ffn = _pick_tile(t, 512)
    tf = 512
    tm_moe = _pick_tile(t, 512)
    tm_comb = _pick_tile(t, 256)
    conv_rows = _pick_tile(seq, 1024)

    cos_t, sin_t = _rope_tables(seq)
    w_main = w_in.astype(BF16)
    w_dt = jnp.pad(w_in[:, :, MAIN_COLS:], ((0, 0), (0, 0), (0, LANES - 2 * SSM_HEADS))).astype(BF16)
    w_out_b = w_out.astype(BF16)
    wg_b, wu_b, wd_b = ffn_w_gate.astype(BF16), ffn_w_up.astype(BF16), ffn_w_down.astype(BF16)
    eg_b, eu_b, ed_b = exp_w_gate.astype(BF16), exp_w_up.astype(BF16), exp_w_down.astype(BF16)
    e_fwd, e_bwd = _expand_matrix(False), _expand_matrix(True)
    na_tabs = _na_bias_tables(na_rpb, seq // GRID_W)
    pad_row = lambda v: jnp.pad(v.reshape(1, -1).astype(F32), ((0, 0), (0, LANES - v.size)))

    for layer in range(depth):
        proj, dt_raw = _inproj(x2, norm_mix[layer].reshape(1, d), w_main, w_dt, layer,
                               cos_t, sin_t, seq, tm_proj)
        y_na = _na_attention(proj, na_tabs, layer, bsz, seq)
        y_dil = _dilated_attention(proj, bsz, seq)
        xc = _conv_silu(proj, conv_w[layer].astype(F32), conv_b[layer].reshape(1, -1).astype(F32), seq, conv_rows)
        bias_row = pad_row(dt_bias[layer])
        a_row = pad_row(-jnp.exp(a_log[layer].astype(F32)))
        y_fwd = _ssd_scan(xc, dt_raw, bias_row, a_row, e_fwd, bsz, seq, False)
        dsk_row = jnp.repeat(d_skip[layer].astype(F32), HEAD_DIM).reshape(1, SSM_W)
        y_ssd = _ssd_scan(xc, dt_raw, bias_row, a_row, e_bwd, bsz, seq, True,
                          extra=(y_fwd, proj, dsk_row, ssm_norm[layer].reshape(1, SSM_W).astype(F32)))
        j = layer // 2
        g_ffn = norm_ffn[layer].reshape(1, d).astype(F32)
        if layer % 2 == 0:
            x2, hn = _outproj(y_na, y_dil, y_ssd, x2, w_out_b, layer, g_ffn, None, tm_out, BF16)
            x2 = _ffn(hn, x2, wg_b, wu_b, wd_b, j, tm_ffn, tf)
        else:
            wr = jnp.pad(router_w[j].astype(F32), ((0, 0), (0, LANES - N_EXPERTS)))
            x2, h32, route = _outproj(y_na, y_dil, y_ssd, x2, w_out_b, layer, g_ffn, wr, tm_out, F32)
            row_tok, blk_expert, n_used, dest = _route_plan(route, t, tm_moe)
            yb = _moe_ffn(row_tok, blk_expert, n_used, h32, eg_b, eu_b, ed_b, j, tm_moe, tf)
            final = layer == depth - 1
            x2 = _combine(dest, yb, x2, route, norm_final.reshape(1, d).astype(F32), tm_comb, final)
    if depth % 2 == 1:
        raise NotImplementedError("final norm is fused into the expert combine of the last (odd) layer")
    return x2.reshape(bsz, seq, d).astype(x.dtype)
```

```python
import functools

import numpy as np
import jax
import jax.numpy as jnp
from jax import lax
from jax.experimental import pallas as pl
from jax.experimental.pallas import tpu as pltpu

F32 = jnp.float32
BF16 = jnp.bfloat16
I32 = jnp.int32

GRID_W = 64
HEAD_DIM = 64
NA_KH = 8
NA_KW = 16
N_HEADS = 8
ATT_W = N_HEADS * HEAD_DIM
DIL_BRANCHES = ((128, 1), (512, 4), (2048, 16))
DIL_QBLK = 128
DIL_HALF = 64
SSM_HEADS = 16
SSM_GROUPS = 2
SSM_STATE = 128
SSM_CONV = 5
SSM_CHUNK = 128
SSM_W = SSM_HEADS * HEAD_DIM
SSM_BC = SSM_GROUPS * SSM_STATE
SSM_CONV_CH = SSM_W + 2 * SSM_BC
N_EXPERTS = 8
ROPE_THETA = 10000.0
NORM_EPS = 1e-6
NEG = -1e30

LANES = 128
VMEM_LIMIT = 56 * 1024 * 1024
PROJ_TN = 512
MAIN_COLS = 11 * PROJ_TN
INPROJ_TN = 11 * LANES
_GB = ATT_W // LANES
Q_LANE_BLOCKS = frozenset(range(0, _GB)) | frozenset(range(3 * _GB, 4 * _GB))
ROPE_LANE_BLOCKS = frozenset(range(3 * _GB, 5 * _GB))
NA_RB = 4
INPROJ_SPLIT = 4
OUTPROJ_SPLIT = 4
MOE_FIRST_SPLIT = 2
MOE_ISSUE_STEPS = 8
SSM_STEP_CHUNKS = 8
DMA_UNROLL = 8


def _cparams(sem):
    return pltpu.CompilerParams(dimension_semantics=sem, vmem_limit_bytes=VMEM_LIMIT)


def _split3(x):
    hi = x.astype(BF16)
    r = x - hi.astype(F32)
    mid = r.astype(BF16)
    lo = (r - mid.astype(F32)).astype(BF16)
    return hi, mid, lo


def _dot01_l(m01, x):
    hi, mid, lo = _split3(x)
    d = lambda a: jnp.dot(m01, a, preferred_element_type=F32)
    return d(hi) + d(mid) + d(lo)


def _dot01_r(x, m01):
    hi, mid, lo = _split3(x)
    d = lambda a: jnp.dot(a, m01, preferred_element_type=F32)
    return d(hi) + d(mid) + d(lo)


def _silu(x):
    return x / (1.0 + jnp.exp(-x))


def _inproj_body(x_ref, g_ref, w_ref, wdt_ref, cos_ref, sin_ref, o_ref, odt_ref, xn_ref):
    j = pl.program_id(1)
    tm = x_ref.shape[0]
    nblk = INPROJ_TN // LANES
    parts = [slice(p * tm // INPROJ_SPLIT, (p + 1) * tm // INPROJ_SPLIT) for p in range(INPROJ_SPLIT)]
    lane = lax.broadcasted_iota(I32, (tm // INPROJ_SPLIT, LANES), 1)
    first = (lane % HEAD_DIM) < (HEAD_DIM // 2)

    def rotary(xs, cos, sin):
        partner = jnp.where(first, pltpu.roll(xs, LANES - HEAD_DIM // 2, 1), pltpu.roll(xs, HEAD_DIM // 2, 1))
        return xs * cos + partner * sin

    def tile(jt):
        blocks = range(jt * nblk, (jt + 1) * nblk)
        for rows in parts:
            if jt == 0:
                x = x_ref[rows, :]
                ms = jnp.mean(x * x, axis=-1, keepdims=True)
                xn = (x * lax.rsqrt(ms + NORM_EPS) * g_ref[...]).astype(BF16)
                xn_ref[rows, :] = xn
                odt_ref[rows, :] = jnp.dot(xn, wdt_ref[...], preferred_element_type=F32)
            else:
                xn = xn_ref[rows, :]
            acc = jnp.dot(xn, w_ref[...], preferred_element_type=F32)
            pieces = []
            for c, gb in enumerate(blocks):
                xs = acc[:, c * LANES:(c + 1) * LANES]
                if gb in Q_LANE_BLOCKS:
                    xs = xs * HEAD_DIM ** -0.5
                if gb in ROPE_LANE_BLOCKS:
                    xs = rotary(xs, cos_ref[rows, :], sin_ref[rows, :])
                pieces.append(xs.astype(BF16))
            o_ref[rows, :] = jnp.concatenate(pieces, axis=1)

    for jt in range(MAIN_COLS // INPROJ_TN):
        pl.when(j == jt)(functools.partial(tile, jt))


def _inproj(x2, g, w_main, w_dt, layer, cos_t, sin_t, seq, tm):
    t, d = x2.shape
    n_s = seq // tm
    return pl.pallas_call(
        _inproj_body,
        grid=(t // tm, MAIN_COLS // INPROJ_TN),
        in_specs=[
            pl.BlockSpec((tm, d), lambda i, j: (i, 0)),
            pl.BlockSpec((1, d), lambda i, j: (0, 0)),
            pl.BlockSpec((None, d, INPROJ_TN), lambda i, j: (layer, 0, j)),
            pl.BlockSpec((None, d, LANES), lambda i, j: (layer, 0, 0)),
            pl.BlockSpec((tm, LANES), lambda i, j: (i % n_s, 0)),
            pl.BlockSpec((tm, LANES), lambda i, j: (i % n_s, 0)),
        ],
        out_specs=[
            pl.BlockSpec((tm, INPROJ_TN), lambda i, j: (i, j)),
            pl.BlockSpec((tm, LANES), lambda i, j: (i, 0)),
        ],
        out_shape=[jax.ShapeDtypeStruct((t, MAIN_COLS), BF16),
                   jax.ShapeDtypeStruct((t, LANES), F32)],
        scratch_shapes=[pltpu.VMEM((tm, d), BF16)],
        compiler_params=_cparams(("arbitrary", "arbitrary")),
        name="inproj",
    )(x2, g, w_main, w_dt, cos_t, sin_t)


def _na_bias_tables(rpb, rows):
    nb = rows // NA_RB
    c = np.arange(GRID_W)
    cs = np.clip(c - NA_KW // 2, 0, GRID_W - NA_KW)
    ck = np.arange(GRID_W)
    col_ok = (ck[None, :] >= cs[:, None]) & (ck[None, :] < cs[:, None] + NA_KW)
    dc = np.clip(ck[None, :] - c[:, None] + NA_KW - 1, 0, 2 * NA_KW - 2)
    onehot = (dc.reshape(-1)[None, :] == np.arange(2 * NA_KW - 1)[:, None]).astype(np.float32)
    dr = np.clip(np.arange(3 * NA_RB)[None, :] - NA_RB - np.arange(NA_RB)[:, None] + NA_KH - 1, 0, 2 * NA_KH - 2)
    by_row = rpb.astype(F32)[:, :, dr, :]
    full = jnp.einsum('lhqkd,dx->lhqkx', by_row, jnp.asarray(onehot), precision=lax.Precision.HIGHEST)
    full = full.reshape(full.shape[:4] + (GRID_W, GRID_W))
    full = jnp.transpose(full, (0, 1, 2, 4, 3, 5))
    full = full.reshape(full.shape[:2] + (NA_RB * GRID_W, 3 * NA_RB * GRID_W))
    tabs = []
    for i in (0, 1, nb - 1):
        r = NA_RB * i + np.arange(NA_RB)
        rs = np.clip(r - NA_KH // 2, 0, rows - NA_KH)
        absrow = NA_RB * (i - 1) + np.arange(3 * NA_RB)
        row_ok = (absrow[None, :] >= rs[:, None]) & (absrow[None, :] < rs[:, None] + NA_KH)
        ok = row_ok[:, None, :, None] & col_ok[None, :, None, :]
        tabs.append(jnp.where(ok.reshape(full.shape[2:])[None, None], full, NEG))
    return jnp.stack(tabs, axis=1)


def _na_body(q_ref, kp_ref, kc_ref, kn_ref, vp_ref, vc_ref, vn_ref, b_ref, o_ref):
    for h in range(N_HEADS):
        sl = slice(h * HEAD_DIM, (h + 1) * HEAD_DIM)
        q = q_ref[:, sl]
        k = jnp.concatenate([kp_ref[:, sl], kc_ref[:, sl], kn_ref[:, sl]], axis=0)
        v = jnp.concatenate([vp_ref[:, sl], vc_ref[:, sl], vn_ref[:, sl]], axis=0)
        s = lax.dot_general(q, k, (((1,), (1,)), ((), ())), preferred_element_type=F32)
        s = s + b_ref[h]
        m = jnp.max(s, axis=-1, keepdims=True)
        p = jnp.exp(s - m)
        l = jnp.sum(p, axis=-1, keepdims=True)
        o = jnp.dot(p.astype(BF16), v, preferred_element_type=F32) / l
        o_ref[:, sl] = o.astype(BF16)


def _na_attention(proj, bias_tabs, layer, bsz, seq):
    t = bsz * seq
    tq = NA_RB * GRID_W
    nb = seq // tq
    assert nb >= 3 and seq // GRID_W >= NA_KH
    sel = lambda i: jnp.where(i == 0, 0, jnp.where(i == nb - 1, 2, 1))
    prev = lambda b, i: b * nb + jnp.maximum(i - 1, 0)
    cur = lambda b, i: b * nb + i
    nxt = lambda b, i: b * nb + jnp.minimum(i + 1, nb - 1)
    blk = lambda rowf, col: pl.BlockSpec((tq, ATT_W), lambda b, i: (rowf(b, i), col))
    return pl.pallas_call(
        _na_body,
        grid=(bsz, nb),
        in_specs=[blk(cur, 0), blk(prev, 1), blk(cur, 1), blk(nxt, 1),
                  blk(prev, 2), blk(cur, 2), blk(nxt, 2),
                  pl.BlockSpec((None, None, N_HEADS, tq, 3 * tq), lambda b, i: (layer, sel(i), 0, 0, 0))],
        out_specs=pl.BlockSpec((tq, ATT_W), lambda b, i: (cur(b, i), 0)),
        out_shape=jax.ShapeDtypeStruct((t, ATT_W), BF16),
        compiler_params=_cparams(("arbitrary", "arbitrary")),
        name="na_attn",
    )(proj, proj, proj, proj, proj, proj, proj, bias_tabs)


DIL_SUPER = 2048
DIL_STEP_Q = 1024


def _dil_split_body(x_ref, *rest):
    (o_a, o_b), (scr, scr2) = rest[:2], rest[2:]
    da, db = o_a.shape[2], o_b.shape[2]
    na, nb = o_a.shape[3], o_b.shape[3]
    step = db // da
    nl = ATT_W // LANES
    for c in range(nl):
        scr[c] = x_ref[:, c * LANES:(c + 1) * LANES].astype(F32)
    for ra in range(da):
        for c in range(nl):
            piece = scr[c, pl.ds(ra, na, stride=da), :]
            o_a[0, 0, ra, :, c * LANES:(c + 1) * LANES] = piece.astype(BF16)
            scr2[c, ra * na:(ra + 1) * na, :] = piece
    for ra in range(da):
        for r2 in range(step):
            for c in range(nl):
                piece = scr2[c, pl.ds(ra * na + r2, nb, stride=step), :]
                o_b[0, 0, ra + da * r2, :, c * LANES:(c + 1) * LANES] = piece.astype(BF16)


def _dil_split(proj, bsz, seq, dils):
    t = bsz * seq
    nsb = seq // DIL_SUPER
    out_specs = [pl.BlockSpec((1, 1, dil, DIL_SUPER // dil, ATT_W),
                              lambda i, c: (c, i // nsb, 0, i % nsb, 0)) for dil in dils]
    out_shape = [jax.ShapeDtypeStruct((3, bsz, dil, seq // dil, ATT_W), BF16) for dil in dils]
    return pl.pallas_call(
        _dil_split_body,
        grid=(t // DIL_SUPER, 3),
        in_specs=[pl.BlockSpec((DIL_SUPER, ATT_W), lambda i, c: (i, 3 + c))],
        out_specs=out_specs, out_shape=out_shape,
        scratch_shapes=[pltpu.VMEM((ATT_W // LANES, DIL_SUPER, LANES), F32)] * 2,
        compiler_params=_cparams(("arbitrary", "arbitrary")),
        name="dil_split",
    )(proj)


def _dil_body(q_ref, kp_ref, kc_ref, kn_ref, vp_ref, vc_ref, vn_ref, o_ref, l_ref, *, n_sub):
    u = pl.program_id(2)
    qb = q_ref.shape[0]
    nsb = qb // DIL_QBLK
    nk = 2 * DIL_QBLK
    a = lax.broadcasted_iota(I32, (DIL_QBLK, nk), 0)
    kk = lax.broadcasted_iota(I32, (DIL_QBLK, nk), 1)
    band = (kk >= a) & (kk <= a + 2 * DIL_HALF)
    oks = []
    for j in range(nsb):
        uk = u * qb + j * DIL_QBLK - DIL_HALF + kk
        oks.append(band & (uk >= 0) & (uk < n_sub))
    ok = jnp.concatenate(oks, axis=0) if nsb > 1 else oks[0]
    lane = lax.broadcasted_iota(I32, (qb, LANES), 1)
    lse_tile = jnp.zeros((qb, LANES), F32)
    dn = (((1,), (1,)), ((), ()))
    for h in range(N_HEADS):
        sl = slice(h * HEAD_DIM, (h + 1) * HEAD_DIM)
        k = jnp.concatenate([kp_ref[:, sl], kc_ref[:, sl], kn_ref[:, sl]], axis=0)
        v = jnp.concatenate([vp_ref[:, sl], vc_ref[:, sl], vn_ref[:, sl]], axis=0)
        parts = [lax.dot_general(q_ref[j * DIL_QBLK:(j + 1) * DIL_QBLK, sl], k[j * DIL_QBLK:j * DIL_QBLK + nk],
                                 dn, preferred_element_type=F32) for j in range(nsb)]
        s = jnp.concatenate(parts, axis=0) if nsb > 1 else parts[0]
        s = jnp.where(ok, s, NEG)
        m = jnp.max(s, axis=-1, keepdims=True)
        p = jnp.exp(s - m)
        l = jnp.sum(p, axis=-1, keepdims=True)
        pb = p.astype(BF16)
        outs = [jnp.dot(pb[j * DIL_QBLK:(j + 1) * DIL_QBLK], v[j * DIL_QBLK:j * DIL_QBLK + nk],
                        preferred_element_type=F32) for j in range(nsb)]
        o = jnp.concatenate(outs, axis=0) if nsb > 1 else outs[0]
        o_ref[:, sl] = (o / l).astype(BF16)
        lse_tile = jnp.where(lane == h, m + jnp.log(l), lse_tile)
    l_ref[...] = lse_tile


def _dil_branch(src, bsz, seq, dil):
    n_sub = seq // dil
    qb = min(DIL_STEP_Q, n_sub)
    nu = n_sub // qb
    hpb = qb // DIL_HALF
    nhalo = n_sub // DIL_HALF
    prev = lambda u: jnp.maximum(u * hpb - 1, 0)
    nxt = lambda u: jnp.minimum((u + 1) * hpb, nhalo - 1)
    if dil == 1:
        cur = lambda c: pl.BlockSpec((qb, ATT_W), lambda b, r, u: (b * nu + u, 3 + c))
        halo = lambda hf, c: pl.BlockSpec((DIL_HALF, ATT_W), lambda b, r, u: (b * nhalo + hf(u), 3 + c))
    else:
        cur = lambda c: pl.BlockSpec((None, None, None, qb, ATT_W), lambda b, r, u: (c, b, r, u, 0))
        halo = lambda hf, c: pl.BlockSpec((None, None, None, DIL_HALF, ATT_W),
                                          lambda b, r, u: (c, b, r, hf(u), 0))
    out_blk = lambda w: pl.BlockSpec((None, None, qb, w), lambda b, r, u: (b, r, u, 0))
    return pl.pallas_call(
        functools.partial(_dil_body, n_sub=n_sub),
        grid=(bsz, dil, nu),
        in_specs=[cur(0), halo(prev, 1), cur(1), halo(nxt, 1), halo(prev, 2), cur(2), halo(nxt, 2)],
        out_specs=[out_blk(ATT_W), out_blk(LANES)],
        out_shape=[jax.ShapeDtypeStruct((bsz, dil, n_sub, ATT_W), BF16),
                   jax.ShapeDtypeStruct((bsz, dil, n_sub, LANES), F32)],
        compiler_params=_cparams(("arbitrary", "arbitrary", "arbitrary")),
        name=f"dil_attn_d{dil}",
    )(*([src] * 7))


def _dil_merge_body(*refs, dils):
    nbr = len(dils)
    o_refs, l_refs = refs[:nbr], refs[nbr:2 * nbr]
    e_ref, out_ref = refs[2 * nbr], refs[2 * nbr + 1]
    scr = refs[2 * nbr + 2:]
    outs, lses = [], []
    si = 0
    for n, dil in enumerate(dils):
        if dil == 1:
            outs.append(o_refs[n][0, 0].astype(F32))
            lses.append(l_refs[n][0, 0])
            continue
        so, sl = scr[si], scr[si + 1]
        si += 2
        rows = DIL_SUPER // dil
        for rho in range(dil):
            for c in range(ATT_W // LANES):
                so[c, pl.ds(rho, rows, stride=dil), :] = (
                    o_refs[n][0, rho, :, c * LANES:(c + 1) * LANES].astype(F32))
            sl[pl.ds(rho, rows, stride=dil), :] = l_refs[n][0, rho]
        outs.append(jnp.concatenate([so[c] for c in range(ATT_W // LANES)], axis=1))
        lses.append(sl[...])
    m = functools.reduce(jnp.maximum, lses)
    es = [jnp.exp(l - m) for l in lses]
    tot = functools.reduce(lambda x, y: x + y, es)
    acc = None
    for o, e in zip(outs, es):
        w = jnp.dot((e / tot).astype(BF16), e_ref[...], preferred_element_type=F32)
        acc = w * o if acc is None else acc + w * o
    out_ref[...] = acc.astype(BF16)


def _dil_merge(outs, lses, bsz, seq, dils):
    t = bsz * seq
    nsb = seq // DIL_SUPER
    spec = lambda dil, w: pl.BlockSpec((1, dil, DIL_SUPER // dil, w), lambda i: (i // nsb, 0, i % nsb, 0))
    e8 = np.zeros((LANES, ATT_W), np.float32)
    for h in range(N_HEADS):
        e8[h, h * HEAD_DIM:(h + 1) * HEAD_DIM] = 1.0
    scratch = []
    for dil in dils:
        if dil != 1:
            scratch += [pltpu.VMEM((ATT_W // LANES, DIL_SUPER, LANES), F32), pltpu.VMEM((DIL_SUPER, LANES), F32)]
    return pl.pallas_call(
        functools.partial(_dil_merge_body, dils=dils),
        grid=(t // DIL_SUPER,),
        in_specs=[spec(dil, ATT_W) for dil in dils] + [spec(dil, LANES) for dil in dils]
                 + [pl.BlockSpec((LANES, ATT_W), lambda i: (0, 0))],
        out_specs=pl.BlockSpec((DIL_SUPER, ATT_W), lambda i: (i, 0)),
        out_shape=jax.ShapeDtypeStruct((t, ATT_W), BF16),
        scratch_shapes=scratch,
        compiler_params=_cparams(("arbitrary",)),
        name="dil_merge",
    )(*outs, *lses, jnp.asarray(e8, BF16))


def _dilated_attention(proj, bsz, seq):
    dils = tuple(dil for _, dil in DIL_BRANCHES)
    for win, dil in DIL_BRANCHES:
        assert win // 2 // dil == DIL_HALF and DIL_SUPER % (dil * DIL_QBLK) == 0
    assert seq % DIL_SUPER == 0 and dils[0] == 1 and len(dils) == 3 and dils[2] % dils[1] == 0
    split = _dil_split(proj, bsz, seq, dils[1:])
    outs, lses = [], []
    for n, dil in enumerate(dils):
        o, l = _dil_branch(proj if dil == 1 else split[n - 1], bsz, seq, dil)
        outs.append(o)
        lses.append(l)
    return _dil_merge(outs, lses, bsz, seq, dils)


CONV_HALO = 16


def _conv_body(xp_ref, xc_ref, xn_ref, w_ref, b_ref, o_ref, *, blocks_per_seq):
    i = pl.program_id(0)
    r = xc_ref.shape[0]
    pos = i % blocks_per_seq
    keep_p = jnp.where(pos == 0, 0.0, 1.0).astype(F32)
    keep_n = jnp.where(pos == blocks_per_seq - 1, 0.0, 1.0).astype(F32)
    ext = jnp.concatenate([xp_ref[...].astype(F32) * keep_p, xc_ref[...].astype(F32),
                           xn_ref[...].astype(F32) * keep_n], axis=0)
    w = w_ref[...]
    half = SSM_CONV // 2
    y = b_ref[...] + jnp.zeros((r, w.shape[1]), F32)
    for k in range(SSM_CONV):
        st = CONV_HALO + k - half
        y = y + ext[st:st + r, :] * w[k:k + 1, :]
    o_ref[...] = _silu(y).astype(BF16)


def _conv_silu(proj, conv_w, conv_b, seq, rblk):
    t = proj.shape[0]
    ncol = SSM_CONV_CH // PROJ_TN
    col0 = (MAIN_COLS - SSM_CONV_CH) // PROJ_TN
    rh = rblk // CONV_HALO
    nh = t // CONV_HALO
    return pl.pallas_call(
        functools.partial(_conv_body, blocks_per_seq=seq // rblk),
        grid=(t // rblk, ncol),
        in_specs=[
            pl.BlockSpec((CONV_HALO, PROJ_TN), lambda i, c: (jnp.maximum(i * rh - 1, 0), col0 + c)),
            pl.BlockSpec((rblk, PROJ_TN), lambda i, c: (i, col0 + c)),
            pl.BlockSpec((CONV_HALO, PROJ_TN), lambda i, c: (jnp.minimum((i + 1) * rh, nh - 1), col0 + c)),
            pl.BlockSpec((SSM_CONV, PROJ_TN), lambda i, c: (0, c)),
            pl.BlockSpec((1, PROJ_TN), lambda i, c: (0, c)),
        ],
        out_specs=pl.BlockSpec((rblk, PROJ_TN), lambda i, c: (i, c)),
        out_shape=jax.ShapeDtypeStruct((t, SSM_CONV_CH), BF16),
        compiler_params=_cparams(("arbitrary", "arbitrary")),
        name="ssd_conv",
    )(proj, proj, proj, conv_w, conv_b)


def _ssd_body(*refs, reverse, final):
    if final:
        (x_ref, b_ref, c_ref, dt_ref, bias_ref, a_ref, e_ref,
         yf_ref, z_ref, dsk_ref, ng_ref, o_ref, state_ref, y_ref) = refs
    else:
        x_ref, b_ref, c_ref, dt_ref, bias_ref, a_ref, e_ref, o_ref, state_ref, y_ref = refs
    q = SSM_CHUNK
    gw = SSM_W // SSM_GROUPS
    hpg = SSM_HEADS // SSM_GROUPS
    off = SSM_HEADS if reverse else 0
    end = 0 if reverse else q - 1

    @pl.when(pl.program_id(1) == 0)
    def _():
        state_ref[...] = jnp.zeros_like(state_ref)

    row = lax.broadcasted_iota(I32, (q, q), 0)
    col = lax.broadcasted_iota(I32, (q, q), 1)
    tri = (col >= row) if reverse else (row >= col)
    tri01 = jnp.where(tri, 1.0, 0.0).astype(BF16)
    lane_lo = lax.broadcasted_iota(I32, (q, 2 * HEAD_DIM), 1) < HEAD_DIM
    n_chunks = x_ref.shape[0] // q
    for k in (range(n_chunks - 1, -1, -1) if reverse else range(n_chunks)):
        rows = slice(k * q, (k + 1) * q)
        xr = dt_ref[rows, :] + bias_ref[...]
        dt = jnp.maximum(xr, 0.0) + jnp.log(1.0 + jnp.exp(-jnp.abs(xr)))
        dta = dt * a_ref[...]
        acs = _dot01_l(tri01, dta)
        acs_t = acs.T
        acs_end = acs[end:end + 1, :]
        wide = jnp.dot(jnp.concatenate([dt, jnp.exp(acs), jnp.exp(acs_end - acs)], axis=0).astype(BF16),
                       e_ref[...], preferred_element_type=F32)
        dt_x, e_acs, ds_x = wide[:q], wide[q:2 * q], wide[2 * q:]
        cd_x = _dot01_r(jnp.broadcast_to(jnp.exp(acs_end), (8, LANES)), e_ref[...])[0:1, :]
        xs = x_ref[rows, :].astype(F32)
        xdt = xs * dt_x
        xds = (xdt * ds_x).astype(BF16)
        xdt = xdt.astype(BF16)

        for g in range(SSM_GROUPS):
            gs = slice(g * gw, (g + 1) * gw)
            bg = b_ref[rows, g * SSM_STATE:(g + 1) * SSM_STATE]
            cg = c_ref[rows, g * SSM_STATE:(g + 1) * SSM_STATE]
            cb = lax.dot_general(cg, bg, (((1,), (1,)), ((), ())), preferred_element_type=F32)
            for r in range(0, hpg, 2):
                hh = g * hpg + r
                ms = []
                for h2 in (hh, hh + 1):
                    diff = acs[:, off + h2:off + h2 + 1] - acs_t[off + h2:off + h2 + 1, :]
                    ms.append((cb * jnp.exp(jnp.where(tri, diff, -jnp.inf))).astype(BF16))
                ps = slice(hh * HEAD_DIM, (hh + 2) * HEAD_DIM)
                xp = xdt[:, ps]
                zero = jnp.zeros_like(xp)
                rhs = jnp.concatenate([jnp.where(lane_lo, xp, zero), jnp.where(lane_lo, zero, xp)], axis=0)
                y_ref[rows, ps] = jnp.dot(jnp.concatenate(ms, axis=1), rhs, preferred_element_type=F32)
            prev = state_ref[g]
            y_off = jnp.dot(cg, prev.astype(BF16), preferred_element_type=F32) * e_acs[:, gs]
            y_ref[rows, gs] = y_ref[rows, gs] + y_off
            bt = bg.astype(F32).T.astype(BF16)
            st = jnp.dot(bt, xds[:, gs], preferred_element_type=F32)
            state_ref[g] = cd_x[:, gs] * prev + st

        if final:
            y = yf_ref[rows, :] + y_ref[rows, :] + dsk_ref[...] * xs
            y = y * _silu(z_ref[rows, :].astype(F32))
            for g in range(SSM_GROUPS):
                gs = slice(g * gw, (g + 1) * gw)
                yg = y[:, gs]
                yg = yg * lax.rsqrt(jnp.mean(yg * yg, axis=-1, keepdims=True) + NORM_EPS)
                o_ref[rows, gs] = (yg * ng_ref[:, gs]).astype(o_ref.dtype)
        else:
            o_ref[rows, :] = y_ref[rows, :]


def _ssd_scan(xc, dt_raw, bias_row, a_row, e_mat, bsz, seq, reverse, extra=None):
    t = bsz * seq
    q = SSM_CHUNK * SSM_STEP_CHUNKS
    assert seq % q == 0
    nc = seq // q
    final = extra is not None
    rowf =(lambda b, c: b * nc + (nc - 1 - c)) if reverse else (lambda b, c: b * nc + c)
    const = lambda w: pl.BlockSpec((1, w), lambda b, c: (0, 0))
    in_specs = [
        pl.BlockSpec((q, SSM_W), lambda b, c: (rowf(b, c), 0)),
        pl.BlockSpec((q, SSM_BC), lambda b, c: (rowf(b, c), SSM_W // SSM_BC)),
        pl.BlockSpec((q, SSM_BC), lambda b, c: (rowf(b, c), SSM_W // SSM_BC + 1)),
        pl.BlockSpec((q, LANES), lambda b, c: (rowf(b, c), 0)),
        const(LANES), const(LANES),
        pl.BlockSpec((LANES, SSM_W), lambda b, c: (0, 0)),
    ]
    args = [xc, xc, xc, dt_raw, bias_row, a_row, e_mat]
    if final:
        y_fwd, proj, dsk_row, ng_row = extra
        zcol = (3 * ATT_W * 2) // SSM_W
        in_specs += [pl.BlockSpec((q, SSM_W), lambda b, c: (rowf(b, c), 0)),
                     pl.BlockSpec((q, SSM_W), lambda b, c: (rowf(b, c), zcol)),
                     const(SSM_W), const(SSM_W)]
        args += [y_fwd, proj, dsk_row, ng_row]
    return pl.pallas_call(
        functools.partial(_ssd_body, reverse=reverse, final=final),
        grid=(bsz, nc),
        in_specs=in_specs,
        out_specs=pl.BlockSpec((q, SSM_W), lambda b, c: (rowf(b, c), 0)),
        out_shape=jax.ShapeDtypeStruct((t, SSM_W), BF16 if final else F32),
        scratch_shapes=[pltpu.VMEM((SSM_GROUPS, SSM_STATE, SSM_W // SSM_GROUPS), F32),
                        pltpu.VMEM((q, SSM_W), F32)],
        compiler_params=_cparams(("arbitrary", "arbitrary")),
        name="ssd_bwd" if reverse else "ssd_fwd",
    )(*args)


def _expand_matrix(reverse):
    e = np.zeros((LANES, SSM_W), np.float32)
    off = SSM_HEADS if reverse else 0
    for h in range(SSM_HEADS):
        e[off + h, h * HEAD_DIM:(h + 1) * HEAD_DIM] = 1.0
    return jnp.asarray(e, BF16)


def _outproj_body(*refs, router):
    if router:
        yna_ref, ydil_ref, yssd_ref, x_ref, w_ref, g_ref, wr_ref, xo_ref, hn_ref, rt_ref = refs
    else:
        yna_ref, ydil_ref, yssd_ref, x_ref, w_ref, g_ref, xo_ref, hn_ref = refs
    tm = x_ref.shape[0]
    n_parts = OUTPROJ_SPLIT if router else 1
    for part in range(n_parts):
        rows = slice(part * tm // n_parts, (part + 1) * tm // n_parts)
        acc = x_ref[rows, :]
        acc = acc + jnp.dot(yna_ref[rows, :], w_ref[0:ATT_W, :], preferred_element_type=F32)
        acc = acc + jnp.dot(ydil_ref[rows, :], w_ref[ATT_W:2 * ATT_W, :], preferred_element_type=F32)
        acc = acc + jnp.dot(yssd_ref[rows, :], w_ref[2 * ATT_W:, :], preferred_element_type=F32)
        xo_ref[rows, :] = acc
        ms = jnp.mean(acc * acc, axis=-1, keepdims=True)
        h = acc * lax.rsqrt(ms + NORM_EPS) * g_ref[...]
        hn_ref[rows, :] = h.astype(hn_ref.dtype)
        if router:
            h_hi = h.astype(BF16)
            h_lo = (h - h_hi.astype(F32)).astype(BF16)
            wr = wr_ref[...]
            w_hi = wr.astype(BF16)
            w_lo = (wr - w_hi.astype(F32)).astype(BF16)
            d = lambda a, b: jnp.dot(a, b, preferred_element_type=F32)
            logits = d(h_hi, w_hi) + d(h_hi, w_lo) + d(h_lo, w_hi)
            lane = lax.broadcasted_iota(I32, logits.shape, 1)
            logits = jnp.where(lane < N_EXPERTS, logits, -jnp.inf)
            m1 = jnp.max(logits, axis=-1, keepdims=True)
            i1 = jnp.min(jnp.where(logits == m1, lane, LANES), axis=-1, keepdims=True)
            rest = jnp.where(lane == i1, -jnp.inf, logits)
            m2 = jnp.max(rest, axis=-1, keepdims=True)
            i2 = jnp.min(jnp.where(rest == m2, lane, LANES), axis=-1, keepdims=True)
            e = jnp.exp(m2 - m1)
            g1 = 1.0 / (1.0 + e)
            g2 = e / (1.0 + e)
            rt = jnp.where(lane == 0, i1.astype(F32),
                           jnp.where(lane == 1, i2.astype(F32),
                                     jnp.where(lane == 2, g1, jnp.where(lane == 3, g2, 0.0))))
            rt_ref[rows, :] = rt


def _outproj(y_na, y_dil, y_ssd, x2, w_out, layer, g, w_router, tm, hn_dtype):
    t, d = x2.shape
    router = w_router is not None
    row = lambda w: pl.BlockSpec((tm, w), lambda i: (i, 0))
    in_specs = [row(ATT_W), row(ATT_W), row(SSM_W), row(d),
                pl.BlockSpec((None, d, d), lambda i: (layer, 0, 0)),
                pl.BlockSpec((1, d), lambda i: (0, 0))]
    args = [y_na, y_dil, y_ssd, x2, w_out, g]
    out_specs = [row(d), row(d)]
    out_shape = [jax.ShapeDtypeStruct((t, d), F32), jax.ShapeDtypeStruct((t, d), hn_dtype)]
    if router:
        in_specs.append(pl.BlockSpec((d, LANES), lambda i: (0, 0)))
        args.append(w_router)
        out_specs.append(row(LANES))
        out_shape.append(jax.ShapeDtypeStruct((t, LANES), F32))
    return pl.pallas_call(
        functools.partial(_outproj_body, router=router),
        grid=(t // tm,),
        in_specs=in_specs, out_specs=out_specs, out_shape=out_shape,
        compiler_params=_cparams(("arbitrary",)),
        name="outproj_router" if router else "outproj",
    )(*args)


def _ffn_body(hn_ref, x_ref, wg_ref, wu_ref, wd_ref, o_ref, acc_ref):
    f = pl.program_id(1)
    last = pl.num_programs(1) - 1

    def swiglu_down():
        h = hn_ref[...]
        g = jnp.dot(h, wg_ref[...], preferred_element_type=F32)
        u = jnp.dot(h, wu_ref[...], preferred_element_type=F32)
        return jnp.dot((_silu(g) * u).astype(BF16), wd_ref[...], preferred_element_type=F32)

    @pl.when(f == 0)
    def _():
        acc_ref[...] = x_ref[...] + swiglu_down()

    @pl.when((f != 0) & (f != last))
    def _():
        acc_ref[...] += swiglu_down()

    @pl.when(f == last)
    def _():
        o_ref[...] = acc_ref[...] + swiglu_down()


def _ffn(hn, x2, wg, wu, wd, j, tm, tf):
    t, d = x2.shape
    fdim = wg.shape[2]
    return pl.pallas_call(
        _ffn_body,
        grid=(t // tm, fdim // tf),
        in_specs=[pl.BlockSpec((tm, d), lambda i, f: (i, 0)),
                  pl.BlockSpec((tm, d), lambda i, f: (i, 0)),
                  pl.BlockSpec((None, d, tf), lambda i, f: (j, 0, f)),
                  pl.BlockSpec((None, d, tf), lambda i, f: (j, 0, f)),
                  pl.BlockSpec((None, tf, d), lambda i, f: (j, f, 0))],
        out_specs=pl.BlockSpec((tm, d), lambda i, f: (i, 0)),
        out_shape=jax.ShapeDtypeStruct((t, d), F32),
        scratch_shapes=[pltpu.VMEM((tm, d), F32)],
        compiler_params=_cparams(("arbitrary", "arbitrary")),
        name="ffn",
    )(hn, x2, wg, wu, wd)


def _moe_row_copy(h_hbm, xg_ref, sem, tok, slot, r):
    return pltpu.make_async_copy(h_hbm.at[pl.ds(tok, 1), :], xg_ref.at[slot, pl.ds(r, 1), :], sem.at[slot])


def _moe_ffn_body(tok_ref, be_ref, nu_ref, h_hbm, wg_ref, wu_ref, wd_ref, o_ref, xg_ref, xb_ref, acc_ref, sem):
    i = pl.program_id(0)
    f = pl.program_id(1)
    tm = xb_ref.shape[0]
    n_used = nu_ref[0]
    active = i < n_used
    slot = i % 2

    def start_gather(blk, dst_slot):
        def issue(r8, carry):
            for k in range(DMA_UNROLL):
                r = r8 * DMA_UNROLL + k
                _moe_row_copy(h_hbm, xg_ref, sem, tok_ref[blk * tm + r], dst_slot, r).start()
            return carry
        lax.fori_loop(0, tm // DMA_UNROLL, issue, 0)

    @pl.when((f == 0) & (i == 0))
    def _():
        start_gather(0, 0)

    nf = pl.num_programs(1)
    share = tm // MOE_ISSUE_STEPS
    first = active & (f == 0)
    last = active & (f == nf - 1)
    fetch = active & (f >= 1) & (f <= MOE_ISSUE_STEPS) & (i + 1 < n_used)

    def swiglu_down(h):
        g = jnp.dot(h, wg_ref[...], preferred_element_type=F32)
        u = jnp.dot(h, wu_ref[...], preferred_element_type=F32)
        return jnp.dot((_silu(g) * u).astype(BF16), wd_ref[...], preferred_element_type=F32)

    @pl.when(first)
    def _():
        pltpu.make_async_copy(h_hbm.at[pl.ds(0, tm), :], xg_ref.at[slot], sem.at[slot]).wait()
        for p in range(MOE_FIRST_SPLIT):
            rows = slice(p * tm // MOE_FIRST_SPLIT, (p + 1) * tm // MOE_FIRST_SPLIT)
            h = xg_ref[slot, rows, :].astype(BF16)
            xb_ref[rows, :] = h
            acc_ref[rows, :] = swiglu_down(h)

    @pl.when(fetch)
    def _():
        acc_ref[...] += swiglu_down(xb_ref[...])
        base = (f - 1) * share
        for k in range(share):
            r = base + k
            _moe_row_copy(h_hbm, xg_ref, sem, tok_ref[(i + 1) * tm + r], 1 - slot, r).start()

    @pl.when(active & jnp.logical_not(first | last | fetch))
    def _():
        acc_ref[...] += swiglu_down(xb_ref[...])

    @pl.when(last)
    def _():
        o_ref[...] = acc_ref[...] + swiglu_down(xb_ref[...])

    @pl.when(jnp.logical_not(active) & (f == 0))
    def _():
        o_ref[...] = jnp.zeros_like(o_ref)


def _moe_ffn(row_tok, blk_expert, n_used, h32, wg, wu, wd, j, tm, tf):
    n_rows = row_tok.shape[0]
    d = h32.shape[1]
    fdim = wg.shape[3]
    nf = fdim // tf
    assert nf > MOE_ISSUE_STEPS and tm % MOE_ISSUE_STEPS == 0 and tm % DMA_UNROLL == 0
    fidx = lambda i, f, nu: jnp.where(i < nu[0], f, nf - 1)
    grid_spec = pltpu.PrefetchScalarGridSpec(
        num_scalar_prefetch=3,
        grid=(n_rows // tm, nf),
        in_specs=[pl.BlockSpec(memory_space=pl.ANY),
                  pl.BlockSpec((None, None, d, tf), lambda i, f, tok, be, nu: (j, be[i], 0, fidx(i, f, nu))),
                  pl.BlockSpec((None, None, d, tf), lambda i, f, tok, be, nu: (j, be[i], 0, fidx(i, f, nu))),
                  pl.BlockSpec((None, None, tf, d), lambda i, f, tok, be, nu: (j, be[i], fidx(i, f, nu), 0))],
        out_specs=pl.BlockSpec((tm, d), lambda i, f, tok, be, nu: (i, 0)),
        scratch_shapes=[pltpu.VMEM((2, tm, d), F32), pltpu.VMEM((tm, d), BF16),
                        pltpu.VMEM((tm, d), F32), pltpu.SemaphoreType.DMA((2,))],
    )
    return pl.pallas_call(
        _moe_ffn_body,
        grid_spec=grid_spec,
        out_shape=jax.ShapeDtypeStruct((n_rows, d), F32),
        compiler_params=_cparams(("arbitrary", "arbitrary")),
        name="moe_ffn",
    )(row_tok, blk_expert, n_used, h32, wg, wu, wd)


def _combine_row_copy(y_hbm, buf_ref, sem, src, slot, k, r):
    return pltpu.make_async_copy(y_hbm.at[pl.ds(src, 1), :], buf_ref.at[slot, k, pl.ds(r, 1), :], sem.at[slot])


def _combine_body(pos_ref, y_hbm, x_ref, rt_ref, gf_ref, o_ref, buf_ref, sem, *, final):
    i = pl.program_id(0)
    tm = x_ref.shape[0]
    slot = i % 2

    def start_gather(step, dst_slot):
        def issue(r4, carry):
            for u in range(DMA_UNROLL // 2):
                r = r4 * (DMA_UNROLL // 2) + u
                for k in range(2):
                    _combine_row_copy(y_hbm, buf_ref, sem, pos_ref[2 * (step * tm + r) + k], dst_slot, k, r).start()
            return carry
        lax.fori_loop(0, tm // (DMA_UNROLL // 2), issue, 0)

    @pl.when(i == 0)
    def _():
        start_gather(0, 0)

    @pl.when(i + 1 < pl.num_programs(0))
    def _():
        start_gather(i + 1, 1 - slot)

    for k in range(2):
        pltpu.make_async_copy(y_hbm.at[pl.ds(0, tm), :], buf_ref.at[slot, k], sem.at[slot]).wait()

    rt = rt_ref[...]
    out = x_ref[...] + rt[:, 2:3] * buf_ref[slot, 0] + rt[:, 3:4] * buf_ref[slot, 1]
    if final:
        ms = jnp.mean(out * out, axis=-1, keepdims=True)
        out = out * lax.rsqrt(ms + NORM_EPS) * gf_ref[...]
    o_ref[...] = out


def _combine(pos_flat, yb, x2, route, g_final, tm, final):
    t, d = x2.shape
    grid_spec = pltpu.PrefetchScalarGridSpec(
        num_scalar_prefetch=1,
        grid=(t // tm,),
        in_specs=[pl.BlockSpec(memory_space=pl.ANY),
                  pl.BlockSpec((tm, d), lambda i, pos: (i, 0)),
                  pl.BlockSpec((tm, LANES), lambda i, pos: (i, 0)),
                  pl.BlockSpec((1, d), lambda i, pos: (0, 0))],
        out_specs=pl.BlockSpec((tm, d), lambda i, pos: (i, 0)),
        scratch_shapes=[pltpu.VMEM((2, 2, tm, d), F32), pltpu.SemaphoreType.DMA((2,))],
    )
    return pl.pallas_call(
        functools.partial(_combine_body, final=final),
        grid_spec=grid_spec,
        out_shape=jax.ShapeDtypeStruct((t, d), F32),
        compiler_params=_cparams(("arbitrary",)),
        name="moe_combine",
    )(pos_flat, yb, x2, route, g_final)


def _route_plan(route, t, tm):
    e_flat = route[:, 0:2].astype(I32).reshape(-1)
    n_assign = e_flat.shape[0]
    onehot = (e_flat[:, None] == jnp.arange(N_EXPERTS, dtype=I32)[None, :]).astype(I32)
    csum = jnp.cumsum(onehot, axis=0)
    rank = jnp.sum(onehot * (csum - 1), axis=1)
    counts = csum[-1]
    padded = ((counts + tm - 1) // tm) * tm
    pend = jnp.cumsum(padded)
    pstart = pend - padded
    dest = (pstart[e_flat] + rank).astype(I32)
    n_rows = n_assign + N_EXPERTS * tm
    row_tok = jnp.zeros((n_rows,), I32).at[dest].set(jnp.arange(n_assign, dtype=I32) // 2)
    blk_start = jnp.arange(n_rows // tm, dtype=I32) * tm
    blk_expert = jnp.minimum(jnp.searchsorted(pend, blk_start, side='right'), N_EXPERTS - 1).astype(I32)
    n_used = (pend[-1:] // tm).astype(I32)
    return row_tok, blk_expert, n_used, dest


def _rope_tables(seq):
    half = HEAD_DIM // 2
    inv = ROPE_THETA ** (-jnp.arange(0, HEAD_DIM, 2, dtype=F32) / HEAD_DIM)
    ang = jnp.arange(seq, dtype=F32)[:, None] * inv[None, :]
    cos, sin = jnp.cos(ang), jnp.sin(ang)
    reps = LANES // HEAD_DIM
    cos_t = jnp.concatenate([cos, cos] * reps, axis=1)
    sin_t = jnp.concatenate([-sin, sin] * reps, axis=1)
    del half
    return cos_t, sin_t


def _pick_tile(n, pref):
    while n % pref:
        pref //= 2
    return pref


def kernel(x, norm_mix, w_in, na_rpb, conv_w, conv_b, dt_bias, a_log, d_skip, ssm_norm, w_out, norm_ffn,
           ffn_w_gate, ffn_w_up, ffn_w_down, router_w, exp_w_gate, exp_w_up, exp_w_down, norm_final):
    bsz, seq, d = x.shape
    t = bsz * seq
    depth = w_in.shape[0]
    assert w_in.shape[2] == MAIN_COLS + 2 * SSM_HEADS and seq % (GRID_W * NA_RB) == 0
    x2 = x.reshape(t, d).astype(F32)

    tm_proj = _pick_tile(seq, 1024)
    tm_out = _pick_tile(t, 512)
    tm_ffn = _pick_tile(t, 512)
    tf = 512
    tm_moe = _pick_tile(t, 512)
    tm_comb = _pick_tile(t, 256)
    conv_rows = _pick_tile(seq, 1024)

    cos_t, sin_t = _rope_tables(seq)
    w_main = w_in.astype(BF16)
    w_dt = jnp.pad(w_in[:, :, MAIN_COLS:], ((0, 0), (0, 0), (0, LANES - 2 * SSM_HEADS))).astype(BF16)
    w_out_b = w_out.astype(BF16)
    wg_b, wu_b, wd_b = ffn_w_gate.astype(BF16), ffn_w_up.astype(BF16), ffn_w_down.astype(BF16)
    eg_b, eu_b, ed_b = exp_w_gate.astype(BF16), exp_w_up.astype(BF16), exp_w_down.astype(BF16)
    e_fwd, e_bwd = _expand_matrix(False), _expand_matrix(True)
    na_tabs = _na_bias_tables(na_rpb, seq // GRID_W)
    pad_row = lambda v: jnp.pad(v.reshape(1, -1).astype(F32), ((0, 0), (0, LANES - v.size)))

    for layer in range(depth):
        proj, dt_raw = _inproj(x2, norm_mix[layer].reshape(1, d), w_main, w_dt, layer,
                               cos_t, sin_t, seq, tm_proj)
        y_na = _na_attention(proj, na_tabs, layer, bsz, seq)
        y_dil = _dilated_attention(proj, bsz, seq)
        xc = _conv_silu(proj, conv_w[layer].astype(F32), conv_b[layer].reshape(1, -1).astype(F32), seq, conv_rows)
        bias_row = pad_row(dt_bias[layer])
        a_row = pad_row(-jnp.exp(a_log[layer].astype(F32)))
        y_fwd = _ssd_scan(xc, dt_raw, bias_row, a_row, e_fwd, bsz, seq, False)
        dsk_row = jnp.repeat(d_skip[layer].astype(F32), HEAD_DIM).reshape(1, SSM_W)
        y_ssd = _ssd_scan(xc, dt_raw, bias_row, a_row, e_bwd, bsz, seq, True,
                          extra=(y_fwd, proj, dsk_row, ssm_norm[layer].reshape(1, SSM_W).astype(F32)))
        j = layer // 2
        g_ffn = norm_ffn[layer].reshape(1, d).astype(F32)
        if layer % 2 == 0:
            x2, hn = _outproj(y_na, y_dil, y_ssd, x2, w_out_b, layer, g_ffn, None, tm_out, BF16)
            x2 = _ffn(hn, x2, wg_b, wu_b, wd_b, j, tm_ffn, tf)
        else:
            wr = jnp.pad(router_w[j].astype(F32), ((0, 0), (0, LANES - N_EXPERTS)))
            x2, h32, route = _outproj(y_na, y_dil, y_ssd, x2, w_out_b, layer, g_ffn, wr, tm_out, F32)
            row_tok, blk_expert, n_used, dest = _route_plan(route, t, tm_moe)
            yb = _moe_ffn(row_tok, blk_expert, n_used, h32, eg_b, eu_b, ed_b, j, tm_moe, tf)
            final = layer == depth - 1
            x2 = _combine(dest, yb, x2, route, norm_final.reshape(1, d).astype(F32), tm_comb, final)
    if depth % 2 == 1:
        raise NotImplementedError("final norm is fused into the expert combine of the last (odd) layer")
    return x2.reshape(bsz, seq, d).astype(x.dtype)
```
